```python
import jax
import jax.numpy as jnp
from jax import lax
import numpy as np

D_MODEL = 1024
BATCH = 16
SEQ = 256
DEPTH = 1
DEC_BATCH = 4
DEC_SEQ = 2048
PAST_LEN = 512

GRID_W = 64
CHUNK = 64
RET_HEADS = 4
RET_DK = 128
RET_DV = 256
DN_HEADS = 4
DN_DK = 128
DN_DV = 256
CONV_WIDTH = 3
ROPE_BASE = 10000.0
EPS = 1e-6
D_FF = (((8 * D_MODEL + 2) // 3) + 255) // 256 * 256
RET_QK = RET_HEADS * RET_DK
RET_VW = RET_HEADS * RET_DV
DN_QK = DN_HEADS * DN_DK
DN_VW = DN_HEADS * DN_DV
DN_CONV_CH = 2 * DN_QK + DN_VW
IN_WIDTHS = (RET_QK, RET_QK, RET_VW, RET_VW, DN_QK, DN_QK, DN_VW, DN_VW, 2 * DN_HEADS, 2 * DN_HEADS, D_MODEL, D_MODEL)
IN_COLS = 2 * RET_QK + 2 * RET_VW + 2 * DN_QK + 2 * DN_VW + 4 * DN_HEADS + 2 * D_MODEL

kernel_name = 'hybrid_retention_gdn_diffusion_step'


def _split_cols(t, widths):
    idx, acc = [], 0
    for w in widths[:-1]:
        acc += w
        idx.append(acc)
    return jnp.split(t, idx, axis=-1)


def _rmsnorm(x, w):
    xf = x.astype(jnp.float32)
    y = xf * lax.rsqrt(jnp.mean(xf * xf, axis=-1, keepdims=True) + EPS)
    return (y * w.astype(jnp.float32)).astype(x.dtype)


def _l2norm(x):
    xf = x.astype(jnp.float32)
    return xf * lax.rsqrt(jnp.sum(xf * xf, axis=-1, keepdims=True) + EPS)


def _flip(t):
    return t[:, ::-1]


def _to_chunks(t):
    B, L = t.shape[:2]
    t = t.reshape((B, L // CHUNK, CHUNK) + t.shape[2:])
    perm = (1, 0, 3, 2) + tuple(range(4, t.ndim))
    return jnp.transpose(t, perm)


def _from_chunks(t):
    n, B, H, C, d = t.shape
    return jnp.transpose(t, (1, 0, 3, 2, 4)).reshape(B, n * C, H, d)


def _grid_rope(L):
    rows = L // GRID_W
    row_idx = jnp.repeat(jnp.arange(rows, dtype=jnp.float32), GRID_W)
    col_idx = (jnp.arange(L) % GRID_W).astype(jnp.float32)
    n_freq = RET_DK // 4
    freqs = ROPE_BASE ** (-jnp.arange(n_freq, dtype=jnp.float32) / n_freq)
    ang = jnp.concatenate([row_idx[:, None] * freqs, col_idx[:, None] * freqs], axis=-1)
    return jnp.cos(ang), jnp.sin(ang)


def _apply_rope(x, cos, sin):
    xp = x.astype(jnp.float32).reshape(x.shape[:-1] + (x.shape[-1] // 2, 2))
    x0, x1 = xp[..., 0], xp[..., 1]
    c, s = cos[None, :, None, :], sin[None, :, None, :]
    out = jnp.stack([x0 * c - x1 * s, x0 * s + x1 * c], axis=-1)
    return out.reshape(x.shape).astype(x.dtype)


def _short_conv(x, w):
    C = x.shape[-1]
    pad = CONV_WIDTH // 2
    return lax.conv_general_dilated(x, w[:, None, :].astype(x.dtype), window_strides=(1,), padding=[(pad, pad)], dimension_numbers=('NWC', 'WIO', 'NWC'), feature_group_count=C)


def _retention_chunked(q, k, v, log_gamma, s0):
    qc, kc, vc = (_to_chunks(t.astype(jnp.float32)) for t in (q, k, v))
    pos = jnp.arange(CHUNK, dtype=jnp.float32)
    diff = pos[:, None] - pos[None, :]
    lower = diff >= 0
    lg = log_gamma.astype(jnp.float32)[:, None, None]
    intra = jnp.where(lower, jnp.exp(lg * jnp.where(lower, diff, 0.0)), 0.0)
    q_dec = jnp.exp(lg[:, :, 0] * (pos + 1.0))
    k_dec = jnp.exp(lg[:, :, 0] * (CHUNK - 1.0 - pos))
    c_dec = jnp.exp(lg[:, 0, 0] * CHUNK)

    def step(S, inp):
        qi, ki, vi = inp
        scores = jnp.einsum('bhid,bhjd->bhij', qi, ki) * intra
        o = jnp.einsum('bhij,bhjv->bhiv', scores, vi) + jnp.einsum('bhid,bhdv->bhiv', qi, S) * q_dec[..., None]
        S = S * c_dec[:, None, None] + jnp.einsum('bhjd,bhjv->bhdv', ki * k_dec[..., None], vi)
        return S, o

    S, oc = lax.scan(step, s0.astype(jnp.float32), (qc, kc, vc))
    return _from_chunks(oc), S


def _gated_delta_chunked(q, k, v, beta, log_alpha, s0):
    qc, kc, vc = (_to_chunks(t.astype(jnp.float32)) for t in (q, k, v))
    bc, ac = _to_chunks(beta.astype(jnp.float32)), _to_chunks(log_alpha.astype(jnp.float32))
    tril = jnp.tril(jnp.ones((CHUNK, CHUNK), dtype=bool))
    strict = jnp.tril(jnp.ones((CHUNK, CHUNK), dtype=bool), -1)
    eye = jnp.eye(CHUNK, dtype=jnp.float32)

    def step(S, inp):
        qi, ki, vi, bi, ai = inp
        g = jnp.cumsum(ai, axis=-1)
        decay = jnp.exp(jnp.where(tril, g[..., :, None] - g[..., None, :], -jnp.inf))
        kb = ki * bi[..., None]
        a_mat = eye + jnp.where(strict, jnp.einsum('bhid,bhjd->bhij', kb, ki) * decay, 0.0)
        u = lax.linalg.triangular_solve(a_mat, vi * bi[..., None], left_side=True, lower=True, unit_diagonal=True)
        w = lax.linalg.triangular_solve(a_mat, kb * jnp.exp(g)[..., None], left_side=True, lower=True, unit_diagonal=True)
        v_new = u - jnp.einsum('bhcd,bhdv->bhcv', w, S)
        scores = jnp.einsum('bhid,bhjd->bhij', qi, ki) * decay
        o = jnp.einsum('bhcd,bhdv->bhcv', qi * jnp.exp(g)[..., None], S) + jnp.einsum('bhij,bhjv->bhiv', scores, v_new)
        g_last = g[..., -1:]
        S = S * jnp.exp(g_last)[..., None] + jnp.einsum('bhcd,bhcv->bhdv', ki * jnp.exp(g_last - g)[..., None], v_new)
        return S, o

    S, oc = lax.scan(step, s0.astype(jnp.float32), (qc, kc, vc, bc, ac))
    return _from_chunks(oc), S


def _token_mixers(h, p, s_ret0, s_dn0, rope):
    B, L, _ = h.shape
    proj = h @ p['w_in']
    r_q, r_k, r_v, r_g, d_q, d_k, d_v, d_z, d_b, d_a, g_r, g_d = _split_cols(proj, IN_WIDTHS)

    rq = r_q.reshape(B, L, RET_HEADS, RET_DK) * (RET_DK ** -0.5)
    rk = r_k.reshape(B, L, RET_HEADS, RET_DK)
    rv = r_v.reshape(B, L, RET_HEADS, RET_DV)
    if rope is not None:
        rq = _apply_rope(rq, rope[0], rope[1])
        rk = _apply_rope(rk, rope[0], rope[1])
    log_gamma = jax.nn.log_sigmoid(p['ret_decay_logit'].astype(jnp.float32))
    o_f, sr_f = _retention_chunked(rq, rk, rv, log_gamma[0], s_ret0[:, 0])
    o_b, sr_b = _retention_chunked(_flip(rq), _flip(rk), _flip(rv), log_gamma[1], s_ret0[:, 1])
    o_r = o_f + _flip(o_b)
    mu = jnp.mean(o_r, axis=-1, keepdims=True)
    var = jnp.mean(jnp.square(o_r - mu), axis=-1, keepdims=True)
    o_r = (o_r - mu) * lax.rsqrt(var + EPS) * p['ret_gn_w'].astype(jnp.float32).reshape(RET_HEADS, RET_DV)
    y_r = (jax.nn.silu(r_g) * o_r.reshape(B, L, RET_VW).astype(h.dtype)) @ p['w_ret_o']

    qkv = jax.nn.silu(_short_conv(jnp.concatenate([d_q, d_k, d_v], axis=-1), p['conv_w']))
    dq, dk, dv = _split_cols(qkv, (DN_QK, DN_QK, DN_VW))
    dq = _l2norm(dq.reshape(B, L, DN_HEADS, DN_DK)) * (DN_DK ** -0.5)
    dk = _l2norm(dk.reshape(B, L, DN_HEADS, DN_DK))
    dv = dv.reshape(B, L, DN_HEADS, DN_DV)
    beta = jax.nn.sigmoid(d_b.astype(jnp.float32)).reshape(B, L, 2, DN_HEADS)
    log_alpha = -jnp.exp(p['dn_a_log'].astype(jnp.float32)) * jax.nn.softplus(d_a.astype(jnp.float32).reshape(B, L, 2, DN_HEADS) + p['dn_dt_bias'].astype(jnp.float32))
    od_f, sd_f = _gated_delta_chunked(dq, dk, dv, beta[:, :, 0], log_alpha[:, :, 0], s_dn0[:, 0])
    od_b, sd_b = _gated_delta_chunked(_flip(dq), _flip(dk), _flip(dv), _flip(beta[:, :, 1]), _flip(log_alpha[:, :, 1]), s_dn0[:, 1])
    o_d = od_f + _flip(od_b)
    o_d = o_d * lax.rsqrt(jnp.mean(o_d * o_d, axis=-1, keepdims=True) + EPS) * p['dn_norm_w'].astype(jnp.float32)
    o_d = o_d * jax.nn.silu(d_z.astype(jnp.float32).reshape(B, L, DN_HEADS, DN_DV))
    y_d = o_d.reshape(B, L, DN_VW).astype(h.dtype) @ p['w_dn_o']

    merged = jax.nn.sigmoid(g_r) * y_r + jax.nn.sigmoid(g_d) * y_d
    out = merged @ p['w_out']
    return out, jnp.stack([sr_f, sr_b], axis=1), jnp.stack([sd_f, sd_b], axis=1)


def _layer(x, cond, s_ret0, s_dn0, rope, p):
    mod = (jax.nn.silu(cond) @ p['w_mod'] + p['b_mod'])[:, None, :]
    sh1, sc1, g1, sh2, sc2, g2 = jnp.split(mod, 6, axis=-1)
    nw = p['norm_w']
    h = _rmsnorm(x, nw[0]) * (1 + sc1) + sh1
    m, s_ret, s_dn = _token_mixers(h, p, s_ret0, s_dn0, rope)
    x = x + g1 * _rmsnorm(m, nw[1])
    h = _rmsnorm(x, nw[2]) * (1 + sc2) + sh2
    gate, up = jnp.split(h @ p['w_ffn_in'], 2, axis=-1)
    f = (jax.nn.silu(gate) * up) @ p['w_ffn_out']
    x = x + g2 * _rmsnorm(f, nw[3])
    return x, s_ret, s_dn


def setup_inputs(seed: int = 0) -> dict:
    key = jax.random.key(seed)
    ks = jax.random.split(key, 24)
    f32 = jnp.float32

    def nrm(k, shape, scale):
        return jax.random.normal(k, shape, f32) * scale

    heads_exp = 5.0 + jnp.arange(RET_HEADS, dtype=f32)
    ret_logit0 = jnp.log(jnp.exp2(heads_exp) - 1.0)
    dt = jnp.exp(jax.random.uniform(ks[12], (DEPTH, 2, DN_HEADS), f32, np.log(1e-3), np.log(1e-1)))
    return {
        'x_prompt': nrm(ks[0], (BATCH, SEQ, D_MODEL), 1.0),
        'x_sample': nrm(ks[1], (DEC_BATCH, DEC_SEQ, D_MODEL), 1.0),
        'c': nrm(ks[2], (DEC_BATCH, D_MODEL), 1.0),
        'state_ret': nrm(ks[3], (DEC_BATCH, DEPTH, 2, RET_HEADS, RET_DK, RET_DV), 0.1),
        'state_dn': nrm(ks[4], (DEC_BATCH, DEPTH, 2, DN_HEADS, DN_DK, DN_DV), 0.1),
        'c_ctx': nrm(ks[5], (D_MODEL,), 1.0),
        'w_mod': nrm(ks[6], (DEPTH, D_MODEL, 6 * D_MODEL), 0.5 * D_MODEL ** -0.5),
        'b_mod': nrm(ks[7], (DEPTH, 6 * D_MODEL), 0.02),
        'norm_w': 1.0 + nrm(ks[8], (DEPTH, 4, D_MODEL), 0.02),
        'w_in': nrm(ks[9], (DEPTH, D_MODEL, IN_COLS), D_MODEL ** -0.5),
        'conv_w': nrm(ks[10], (DEPTH, CONV_WIDTH, DN_CONV_CH), CONV_WIDTH ** -0.5),
        'ret_decay_logit': ret_logit0 + nrm(ks[11], (DEPTH, 2, RET_HEADS), 0.1),
        'ret_gn_w': 1.0 + nrm(ks[13], (DEPTH, RET_VW), 0.02),
        'dn_a_log': jnp.log(jax.random.uniform(ks[14], (DEPTH, 2, DN_HEADS), f32, 1.0, 16.0)),
        'dn_dt_bias': dt + jnp.log(-jnp.expm1(-dt)),
        'dn_norm_w': 1.0 + nrm(ks[15], (DEPTH, DN_DV), 0.02),
        'w_ret_o': nrm(ks[16], (DEPTH, RET_VW, D_MODEL), RET_VW ** -0.5),
        'w_dn_o': nrm(ks[17], (DEPTH, DN_VW, D_MODEL), DN_VW ** -0.5),
        'w_out': nrm(ks[18], (DEPTH, D_MODEL, D_MODEL), D_MODEL ** -0.5),
        'w_ffn_in': nrm(ks[19], (DEPTH, D_MODEL, 2 * D_FF), D_MODEL ** -0.5),
        'w_ffn_out': nrm(ks[20], (DEPTH, D_FF, D_MODEL), D_FF ** -0.5),
    }


def reference(x_prompt, x_sample, c, state_ret, state_dn, c_ctx, w_mod, b_mod, norm_w, w_in, conv_w, ret_decay_logit, ret_gn_w, dn_a_log, dn_dt_bias, dn_norm_w, w_ret_o, w_dn_o, w_out, w_ffn_in, w_ffn_out):
    rope = _grid_rope(x_sample.shape[1])
    ctx_cond = c_ctx[None, :]
    n_ctx = x_prompt.shape[0]
    zero_ret = jnp.zeros((n_ctx, 2, RET_HEADS, RET_DK, RET_DV), jnp.float32)
    zero_dn = jnp.zeros((n_ctx, 2, DN_HEADS, DN_DK, DN_DV), jnp.float32)
    y_prompt, y_sample = x_prompt, x_sample
    ret_states, dn_states = [], []
    for l in range(DEPTH):
        p = {'w_mod': w_mod[l], 'b_mod': b_mod[l], 'norm_w': norm_w[l], 'w_in': w_in[l], 'conv_w': conv_w[l],
             'ret_decay_logit': ret_decay_logit[l], 'ret_gn_w': ret_gn_w[l], 'dn_a_log': dn_a_log[l],
             'dn_dt_bias': dn_dt_bias[l], 'dn_norm_w': dn_norm_w[l], 'w_ret_o': w_ret_o[l], 'w_dn_o': w_dn_o[l],
             'w_out': w_out[l], 'w_ffn_in': w_ffn_in[l], 'w_ffn_out': w_ffn_out[l]}
        y_prompt, s_ret, s_dn = _layer(y_prompt, ctx_cond, zero_ret, zero_dn, None, p)
        ret_states.append(s_ret)
        dn_states.append(s_dn)
        y_sample, _, _ = _layer(y_sample, c, state_ret[:, l], state_dn[:, l], rope, p)
    new_state_ret = jnp.stack(ret_states, axis=1).astype(x_prompt.dtype)
    new_state_dn = jnp.stack(dn_states, axis=1).astype(x_prompt.dtype)
    return (y_prompt, y_sample, new_state_ret, new_state_dn)
```

```python
import functools

import jax
import jax.numpy as jnp
from jax import lax
from jax.experimental import pallas as pl
from jax.experimental.pallas import tpu as pltpu

F32 = jnp.float32
BF16 = jnp.bfloat16

D_MODEL = 1024
HEADS = 4
DK = 128
DV = 256
QK_W = HEADS * DK
V_W = HEADS * DV
GDN_CHUNK = 64
BLK = 256
CONV_WIDTH = 3
GRID_W = 64
ROPE_BASE = 10000.0
EPS = 1e-6
D_FF = 2816
FF_CHUNK = 1408
N_GATE = 2 * HEADS

COL_RQ, COL_RK, COL_RV, COL_RG = 0, 512, 1024, 2048
COL_DQ, COL_DK, COL_DV, COL_DZ = 3072, 3584, 4096, 5120
COL_GR, COL_GD = 6144, 7168
PROJ_W = 8192

VMEM_LIMIT = 56 * 1024 * 1024


def _cparams(sem):
    return pltpu.CompilerParams(dimension_semantics=sem, vmem_limit_bytes=VMEM_LIMIT)


def _mm(a, b):
    return jnp.dot(a.astype(BF16), b.astype(BF16), preferred_element_type=F32)


def _mm_nt(a, b):
    return lax.dot_general(a.astype(BF16), b.astype(BF16), (((1,), (1,)), ((), ())),
                           preferred_element_type=F32)


def _mm_tn(a, b):
    return lax.dot_general(a.astype(BF16), b.astype(BF16), (((0,), (0,)), ((), ())),
                           preferred_element_type=F32)


def _mm_f32(a, b):
    return jnp.dot(a, b, precision=lax.Precision.HIGHEST, preferred_element_type=F32)


def _sigmoid(x):
    return 1.0 / (1.0 + jnp.exp(-x))


def _silu(x):
    return x * _sigmoid(x)


def _softplus(x):
    return jnp.maximum(x, 0.0) + jnp.log(1.0 + jnp.exp(-jnp.abs(x)))


def _log_sigmoid(x):
    return -_softplus(-x)


def _rms(x, w):
    return x * lax.rsqrt(jnp.mean(x * x, axis=-1, keepdims=True) + EPS) * w


def _mod_kernel(c_ref, w_ref, b_ref, o_ref):
    o_ref[...] = _mm(_silu(c_ref[...]), w_ref[...]) + b_ref[...]


def _modulation(cond8, w_mod, b_mod):
    n = w_mod.shape[1]
    tn = 1536
    return pl.pallas_call(
        _mod_kernel,
        grid=(n // tn,),
        in_specs=[pl.BlockSpec((8, D_MODEL), lambda j: (0, 0)),
                  pl.BlockSpec((D_MODEL, tn), lambda j: (0, j)),
                  pl.BlockSpec((1, tn), lambda j: (0, j))],
        out_specs=pl.BlockSpec((8, tn), lambda j: (0, j)),
        out_shape=jax.ShapeDtypeStruct((8, n), F32),
        compiler_params=_cparams(("arbitrary",)),
        name="modulation",
    )(cond8, w_mod, b_mod)


def _inproj_kernel(x_ref, mod_ref, nw_ref, w_ref, wba_ref, o_ref, ba_ref, h_sc, *, tm, seq_len):
    i = pl.program_id(0)
    j = pl.program_id(1)

    @pl.when(j == 0)
    def _():
        ci = (i * tm) // seq_len
        m = mod_ref[pl.ds(ci, 1), :]
        sh1 = m[:, 0:D_MODEL]
        sc1 = m[:, D_MODEL:2 * D_MODEL]
        h = _rms(x_ref[...], nw_ref[0:1, :]) * (1.0 + sc1) + sh1
        hb = h.astype(BF16)
        h_sc[...] = hb
        ba_ref[...] = jnp.dot(hb, wba_ref[...], preferred_element_type=F32)

    o_ref[...] = jnp.dot(h_sc[...], w_ref[...], preferred_element_type=F32)


def _inproj(x, mod, norm_w, w_main, w_ba, seq_len):
    t = x.shape[0]
    tm, tn = 512, 2048
    kern = functools.partial(_inproj_kernel, tm=tm, seq_len=seq_len)
    return pl.pallas_call(
        kern,
        grid=(t // tm, PROJ_W // tn),
        in_specs=[pl.BlockSpec((tm, D_MODEL), lambda i, j: (i, 0)),
                  pl.BlockSpec(mod.shape, lambda i, j: (0, 0)),
                  pl.BlockSpec(norm_w.shape, lambda i, j: (0, 0)),
                  pl.BlockSpec((D_MODEL, tn), lambda i, j: (0, j)),
                  pl.BlockSpec((D_MODEL, 2 * N_GATE), lambda i, j: (0, 0))],
        out_specs=[pl.BlockSpec((tm, tn), lambda i, j: (i, j)),
                   pl.BlockSpec((tm, 2 * N_GATE), lambda i, j: (i, 0))],
        out_shape=[jax.ShapeDtypeStruct((t, PROJ_W), F32),
                   jax.ShapeDtypeStruct((t, 2 * N_GATE), F32)],
        scratch_shapes=[pltpu.VMEM((tm, D_MODEL), BF16)],
        compiler_params=_cparams(("arbitrary", "arbitrary")),
        name="inproj",
    )(x, mod, norm_w, w_main, w_ba)


def _pair_swap(x):
    lane = lax.broadcasted_iota(jnp.int32, x.shape, 1)
    nxt = pltpu.roll(x, x.shape[1] - 1, 1)
    prv = pltpu.roll(x, 1, 1)
    return jnp.where((lane & 1) == 0, nxt, prv)


def _ret_kernel(*refs, seq_len, rope, has_s0, emit_state):
    it = iter(refs)
    q_ref, k_ref, v_ref = next(it), next(it), next(it)
    cos_ref = next(it) if rope else None
    sin_ref = next(it) if rope else None
    logit_ref, gnw_ref = next(it), next(it)
    s0_ref = next(it) if has_s0 else None
    o_ref = next(it)
    st_ref = next(it) if emit_state else None
    q_sc, k_sc, sf_sc, sb_sc = next(it), next(it), next(it), next(it)

    nb = seq_len // BLK
    q = q_ref[...] * (DK ** -0.5)
    k = k_ref[...]
    if rope:
        cos = cos_ref[...]
        sin = sin_ref[...]
        q = q * cos + _pair_swap(q) * sin
        k = k * cos + _pair_swap(k) * sin
    q_sc[...] = q
    k_sc[...] = k

    lg = _log_sigmoid(logit_ref[...])
    lgf, lgb = lg[0:1, :], lg[1:2, :]
    lgf1, lgb1 = lgf[:, 0:1], lgb[:, 0:1]
    pos = lax.broadcasted_iota(jnp.int32, (BLK, DK), 0).astype(F32)
    kdec_f = jnp.exp(lgf * (BLK - 1.0 - pos))
    kdec_b = jnp.exp(lgb * pos)
    qdec_f = jnp.exp(lgf * (pos + 1.0))
    qdec_b = jnp.exp(lgb * (BLK - pos))
    cdec_f = jnp.exp(lgf1 * float(BLK))
    cdec_b = jnp.exp(lgb1 * float(BLK))
    ii = lax.broadcasted_iota(jnp.int32, (BLK, BLK), 0)
    jj = lax.broadcasted_iota(jnp.int32, (BLK, BLK), 1)
    diff = (ii - jj).astype(F32)
    dmat = (jnp.where(diff >= 0, jnp.exp(lgf1 * jnp.maximum(diff, 0.0)), 0.0)
            + jnp.where(diff <= 0, jnp.exp(lgb1 * jnp.maximum(-diff, 0.0)), 0.0))

    def contrib(c, dec):
        kc = k_sc[c * BLK:(c + 1) * BLK, :]
        vc = v_ref[c * BLK:(c + 1) * BLK, :]
        return _mm_tn(kc * dec, vc)

    s = s0_ref[0] if has_s0 else None
    for c in range(nb):
        if s is not None:
            sf_sc[c] = s
        if c < nb - 1 or emit_state:
            a = contrib(c, kdec_f)
            s = a if s is None else s * cdec_f + a
    if emit_state:
        st_ref[0] = s
    s = s0_ref[1] if has_s0 else None
    for c in range(nb - 1, -1, -1):
        if s is not None:
            sb_sc[c] = s
        if c > 0 or emit_state:
            a = contrib(c, kdec_b)
            s = a if s is None else s * cdec_b + a
    if emit_state:
        st_ref[1] = s

    gnw = gnw_ref[...]
    for c in range(nb):
        rows = slice(c * BLK, (c + 1) * BLK)
        qc = q_sc[rows, :]
        kc = k_sc[rows, :]
        vc = v_ref[rows, :]
        o = _mm(_mm_nt(qc, kc) * dmat, vc)
        if has_s0 or c > 0:
            o = o + _mm(qc * qdec_f, sf_sc[c])
        if has_s0 or c < nb - 1:
            o = o + _mm(qc * qdec_b, sb_sc[c])
        mu = jnp.mean(o, axis=-1, keepdims=True)
        oc = o - mu
        var = jnp.mean(oc * oc, axis=-1, keepdims=True)
        o_ref[rows, :] = oc * lax.rsqrt(var + EPS) * gnw


def _retention(proj, logit_rep, gn_w, n_seq, seq_len, rope_tabs=None, s0=None, emit_state=False):
    t = proj.shape[0]
    rope = rope_tabs is not None
    has_s0 = s0 is not None
    kern = functools.partial(_ret_kernel, seq_len=seq_len, rope=rope, has_s0=has_s0,
                             emit_state=emit_state)
    in_specs = [pl.BlockSpec((seq_len, DK), lambda b, h: (b, COL_RQ // DK + h)),
                pl.BlockSpec((seq_len, DK), lambda b, h: (b, COL_RK // DK + h)),
                pl.BlockSpec((seq_len, DV), lambda b, h: (b, COL_RV // DV + h))]
    args = [proj, proj, proj]
    if rope:
        in_specs += [pl.BlockSpec((seq_len, DK), lambda b, h: (0, 0))] * 2
        args += list(rope_tabs)
    in_specs += [pl.BlockSpec((None, 2, DK), lambda b, h: (h, 0, 0)),
                 pl.BlockSpec((1, DV), lambda b, h: (0, h))]
    args += [logit_rep, gn_w]
    if has_s0:
        in_specs.append(pl.BlockSpec((None, None, 2, None, DK, DV), lambda b, h: (b, 0, 0, h, 0, 0)))
        args.append(s0)
    out_specs = [pl.BlockSpec((seq_len, DV), lambda b, h: (b, h))]
    out_shape = [jax.ShapeDtypeStruct((t, V_W), F32)]
    if emit_state:
        out_specs.append(pl.BlockSpec((None, None, 2, None, DK, DV), lambda b, h: (b, 0, 0, h, 0, 0)))
        out_shape.append(jax.ShapeDtypeStruct((n_seq, 1, 2, HEADS, DK, DV), F32))
    nb = seq_len // BLK
    res = pl.pallas_call(
        kern,
        grid=(n_seq, HEADS),
        in_specs=in_specs,
        out_specs=out_specs,
        out_shape=out_shape,
        scratch_shapes=[pltpu.VMEM((seq_len, DK), F32), pltpu.VMEM((seq_len, DK), F32),
                        pltpu.VMEM((nb, DK, DV), F32), pltpu.VMEM((nb, DK, DV), F32)],
        compiler_params=_cparams(("arbitrary", "arbitrary")),
        name="retention",
    )(*args)
    return res


def _tri_inverse(lm):
    n = lm.shape[0]
    eye = (lax.broadcasted_iota(jnp.int32, (n, n), 0)
           == lax.broadcasted_iota(jnp.int32, (n, n), 1)).astype(F32)
    x = -lm
    p = eye + x
    steps = GDN_CHUNK.bit_length() - 2
    for _ in range(steps):
        x = _mm_f32(x, x)
        p = p + _mm_f32(p, x)
    return p


def _gdn_kernel(*refs, seq_len, has_s0, emit_state):
    it = iter(refs)
    q_ref, k_ref, v_ref, z_ref, ba_ref = (next(it) for _ in range(5))
    cwq_ref, cwk_ref, cwv_ref = next(it), next(it), next(it)
    alog_ref, dtb_ref, nw_ref = next(it), next(it), next(it)
    s0_ref = next(it) if has_s0 else None
    o_ref = next(it)
    st_ref = next(it) if emit_state else None
    (qn_sc, kn_sc, vn_sc, a_sc, b_sc, od_sc, u_sc, w_sc, s_sc, qg_sc, kd_sc, eg_sc,
     st_sc) = (next(it) for _ in range(13))

    h = pl.program_id(1)
    nb = seq_len // BLK
    nck = BLK // GDN_CHUNK

    def conv_block(ref, cw_ref, r0):
        cur = ref[pl.ds(r0, BLK), :]
        width = cur.shape[1]
        row = lax.broadcasted_iota(jnp.int32, (BLK, width), 0)
        before = ref[pl.ds(jnp.maximum(r0 - 1, 0), 1), :]
        after = ref[pl.ds(jnp.minimum(r0 + BLK, seq_len - 1), 1), :]
        before = jnp.where(r0 > 0, before, 0.0)
        after = jnp.where(r0 + BLK < seq_len, after, 0.0)
        prev = jnp.where(row == 0, before, pltpu.roll(cur, 1, 0))
        nxt = jnp.where(row == BLK - 1, after, pltpu.roll(cur, BLK - 1, 0))
        cw = cw_ref[...]
        return _silu(prev * cw[0:1, :] + cur * cw[1:2, :] + nxt * cw[2:3, :])

    def l2n(x):
        return x * lax.rsqrt(jnp.sum(x * x, axis=-1, keepdims=True) + EPS)

    lane8 = lax.broadcasted_iota(jnp.int32, (1, N_GATE), 1)

    def prep(bi, carry):
        r0 = pl.multiple_of(bi * BLK, BLK)
        qn_sc[pl.ds(r0, BLK), :] = l2n(conv_block(q_ref, cwq_ref, r0)) * (DK ** -0.5)
        kn_sc[pl.ds(r0, BLK), :] = l2n(conv_block(k_ref, cwk_ref, r0))
        vn_sc[pl.ds(r0, BLK), :] = conv_block(v_ref, cwv_ref, r0)
        ba = ba_ref[pl.ds(r0, BLK), :]
        beta = _sigmoid(ba[:, 0:N_GATE])
        la = -jnp.exp(alog_ref[...]) * _softplus(ba[:, N_GATE:2 * N_GATE] + dtb_ref[...])
        for d in range(2):
            oh = lane8 == (d * HEADS + h)
            bcol = jnp.sum(jnp.where(oh, beta, 0.0), axis=1, keepdims=True)
            acol = jnp.sum(jnp.where(oh, la, 0.0), axis=1, keepdims=True)
            b_sc[d, pl.ds(r0, BLK), :] = jnp.broadcast_to(bcol, (BLK, DK))
            a_sc[d, pl.ds(r0, BLK), :] = jnp.broadcast_to(acol, (BLK, DK))
        return carry

    lax.fori_loop(0, nb, prep, 0)

    for d in range(2):
        if has_s0:
            st_sc[d] = s0_ref[d]
        else:
            st_sc[d] = jnp.zeros((DK, DV), F32)

    ii = lax.broadcasted_iota(jnp.int32, (BLK, BLK), 0)
    jj = lax.broadcasted_iota(jnp.int32, (BLK, BLK), 1)
    chunk_shift = GDN_CHUNK.bit_length() - 1
    same = lax.shift_right_logical(ii, chunk_shift) == lax.shift_right_logical(jj, chunk_shift)
    diag = ii == jj

    def to_col(row):
        return jnp.sum(jnp.where(diag, jnp.broadcast_to(row, (BLK, BLK)), 0.0), axis=1, keepdims=True)

    def block_step(step, carry):
        for d in range(2):
            bi = step if d == 0 else nb - 1 - step
            r0 = pl.multiple_of(bi * BLK, BLK)
            q = qn_sc[pl.ds(r0, BLK), :]
            k = kn_sc[pl.ds(r0, BLK), :]
            v = vn_sc[pl.ds(r0, BLK), :]
            a = a_sc[d, pl.ds(r0, BLK), :]
            b = b_sc[d, pl.ds(r0, BLK), :]
            a2 = jnp.concatenate([a, a], axis=1)
            b2 = jnp.concatenate([b, b], axis=1)
            if d == 0:
                earlier_eq = same & (jj <= ii)
                earlier = same & (jj < ii)
                t_earlier_eq = same & (ii <= jj)
                t_later = same & (ii > jj)
            else:
                earlier_eq = same & (jj >= ii)
                earlier = same & (jj > ii)
                t_earlier_eq = same & (ii >= jj)
                t_later = same & (ii < jj)
            g_row = jnp.sum(jnp.where(t_earlier_eq, a2, 0.0), axis=0, keepdims=True)
            r_row = jnp.sum(jnp.where(t_later, a2, 0.0), axis=0, keepdims=True)
            g_col = to_col(g_row)
            r_col = to_col(r_row)
            eg_col = jnp.exp(g_col)
            decay = jnp.where(earlier_eq, jnp.exp(jnp.minimum(g_col - g_row, 0.0)), 0.0)
            kb = k * b
            lm = jnp.where(earlier, _mm_nt(kb, k) * decay, 0.0)
            tinv = _tri_inverse(lm)
            rhs = jnp.concatenate([v * b2, kb * eg_col], axis=1)
            uw = _mm(tinv, rhs)
            u_sc[d] = uw[:, 0:DV]
            w_sc[d] = uw[:, DV:DV + DK]
            s_sc[d] = jnp.where(earlier_eq, _mm_nt(q, k) * decay, 0.0)
            qg_sc[d] = q * eg_col
            kd_sc[d] = k * jnp.exp(r_col)
            eg_sc[d] = jnp.broadcast_to(jnp.exp(g_row + r_row), (8, BLK))

            for ci in range(nck):
                cc = ci if d == 0 else nck - 1 - ci
                rs = slice(cc * GDN_CHUNK, (cc + 1) * GDN_CHUNK)
                s_prev = st_sc[d]
                wq = jnp.concatenate([w_sc[d, rs, :], qg_sc[d, rs, :]], axis=0)
                ws_qs = _mm(wq, s_prev)
                v_new = u_sc[d, rs, :] - ws_qs[0:GDN_CHUNK]
                o = ws_qs[GDN_CHUNK:] + _mm(s_sc[d, rs, rs], v_new)
                od_sc[d, pl.ds(r0 + cc * GDN_CHUNK, GDN_CHUNK), :] = o
                egl = eg_sc[d, 0:1, cc * GDN_CHUNK:cc * GDN_CHUNK + 1]
                st_sc[d] = s_prev * egl + _mm_tn(kd_sc[d, rs, :], v_new)
        return carry

    lax.fori_loop(0, nb, block_step, 0)

    if emit_state:
        st_ref[0] = st_sc[0]
        st_ref[1] = st_sc[1]

    def finish(bi, carry):
        r0 = pl.multiple_of(bi * BLK, BLK)
        o = od_sc[0, pl.ds(r0, BLK), :] + od_sc[1, pl.ds(r0, BLK), :]
        o = _rms(o, nw_ref[...])
        o_ref[pl.ds(r0, BLK), :] = o * _silu(z_ref[pl.ds(r0, BLK), :])
        return carry

    lax.fori_loop(0, nb, finish, 0)


def _gdn(proj, ba, conv_w, a_log, dt_bias, norm_w, n_seq, seq_len, s0=None, emit_state=False):
    t = proj.shape[0]
    has_s0 = s0 is not None
    kern = functools.partial(_gdn_kernel, seq_len=seq_len, has_s0=has_s0, emit_state=emit_state)
    in_specs = [pl.BlockSpec((seq_len, DK), lambda b, h: (b, COL_DQ // DK + h)),
                pl.BlockSpec((seq_len, DK), lambda b, h: (b, COL_DK // DK + h)),
                pl.BlockSpec((seq_len, DV), lambda b, h: (b, COL_DV // DV + h)),
                pl.BlockSpec((seq_len, DV), lambda b, h: (b, COL_DZ // DV + h)),
                pl.BlockSpec((seq_len, 2 * N_GATE), lambda b, h: (b, 0)),
                pl.BlockSpec((CONV_WIDTH, DK), lambda b, h: (0, h)),
                pl.BlockSpec((CONV_WIDTH, DK), lambda b, h: (0, QK_W // DK + h)),
                pl.BlockSpec((CONV_WIDTH, DV), lambda b, h: (0, 2 * QK_W // DV + h)),
                pl.BlockSpec((1, N_GATE), lambda b, h: (0, 0)),
                pl.BlockSpec((1, N_GATE), lambda b, h: (0, 0)),
                pl.BlockSpec((1, DV), lambda b, h: (0, 0))]
    args = [proj, proj, proj, proj, ba, conv_w, conv_w, conv_w, a_log, dt_bias, norm_w]
    if has_s0:
        in_specs.append(pl.BlockSpec((None, None, 2, None, DK, DV), lambda b, h: (b, 0, 0, h, 0, 0)))
        args.append(s0)
    out_specs = [pl.BlockSpec((seq_len, DV), lambda b, h: (b, h))]
    out_shape = [jax.ShapeDtypeStruct((t, V_W), F32)]
    if emit_state:
        out_specs.append(pl.BlockSpec((None, None, 2, None, DK, DV), lambda b, h: (b, 0, 0, h, 0, 0)))
        out_shape.append(jax.ShapeDtypeStruct((n_seq, 1, 2, HEADS, DK, DV), F32))
    scratch = [pltpu.VMEM((seq_len, DK), F32), pltpu.VMEM((seq_len, DK), F32),
               pltpu.VMEM((seq_len, DV), F32),
               pltpu.VMEM((2, seq_len, DK), F32), pltpu.VMEM((2, seq_len, DK), F32),
               pltpu.VMEM((2, seq_len, DV), F32),
               pltpu.VMEM((2, BLK, DV), F32), pltpu.VMEM((2, BLK, DK), F32),
               pltpu.VMEM((2, BLK, BLK), F32), pltpu.VMEM((2, BLK, DK), F32),
               pltpu.VMEM((2, BLK, DK), F32), pltpu.VMEM((2, 8, BLK), F32),
               pltpu.VMEM((2, DK, DV), F32)]
    return pl.pallas_call(
        kern,
        grid=(n_seq, HEADS),
        in_specs=in_specs,
        out_specs=out_specs,
        out_shape=out_shape,
        scratch_shapes=scratch,
        compiler_params=_cparams(("arbitrary", "arbitrary")),
        name="gated_delta",
    )(*args)


def _merge_kernel(x_ref, or_ref, od_ref, rg_ref, gr_ref, gd_ref, mod_ref, nw_ref,
                  wr_ref, wd_ref, wo_ref, o_ref, *, tm, seq_len):
    i = pl.program_id(0)
    ci = (i * tm) // seq_len
    m = mod_ref[pl.ds(ci, 1), :]
    g1 = m[:, 2 * D_MODEL:3 * D_MODEL]
    y_r = _mm(_silu(rg_ref[...]) * or_ref[...], wr_ref[...])
    y_d = _mm(od_ref[...], wd_ref[...])
    merged = _sigmoid(gr_ref[...]) * y_r + _sigmoid(gd_ref[...]) * y_d
    mo = _mm(merged, wo_ref[...])
    o_ref[...] = x_ref[...] + g1 * _rms(mo, nw_ref[1:2, :])


def _merge(x, o_r, o_d, proj, mod, norm_w, w_ret_o, w_dn_o, w_out, seq_len):
    t = x.shape[0]
    tm = 256
    kern = functools.partial(_merge_kernel, tm=tm, seq_len=seq_len)
    row = lambda i: (i, 0)
    const = lambda i: (0, 0)
    wspec = pl.BlockSpec((D_MODEL, D_MODEL), const)
    return pl.pallas_call(
        kern,
        grid=(t // tm,),
        in_specs=[pl.BlockSpec((tm, D_MODEL), row),
                  pl.BlockSpec((tm, V_W), row),
                  pl.BlockSpec((tm, V_W), row),
                  pl.BlockSpec((tm, V_W), lambda i: (i, COL_RG // V_W)),
                  pl.BlockSpec((tm, D_MODEL), lambda i: (i, COL_GR // D_MODEL)),
                  pl.BlockSpec((tm, D_MODEL), lambda i: (i, COL_GD // D_MODEL)),
                  pl.BlockSpec(mod.shape, const),
                  pl.BlockSpec(norm_w.shape, const),
                  wspec, wspec, wspec],
        out_specs=pl.BlockSpec((tm, D_MODEL), row),
        out_shape=jax.ShapeDtypeStruct((t, D_MODEL), F32),
        compiler_params=_cparams(("arbitrary",)),
        name="merge_out",
    )(x, o_r, o_d, proj, proj, proj, mod, norm_w, w_ret_o, w_dn_o, w_out)


def _ffn_kernel(x_ref, mod_ref, nw_ref, wi_ref, wo_ref, o_ref, *, tm, seq_len):
    i = pl.program_id(0)
    ci = (i * tm) // seq_len
    m = mod_ref[pl.ds(ci, 1), :]
    sh2 = m[:, 3 * D_MODEL:4 * D_MODEL]
    sc2 = m[:, 4 * D_MODEL:5 * D_MODEL]
    g2 = m[:, 5 * D_MODEL:6 * D_MODEL]
    x = x_ref[...]
    hb = (_rms(x, nw_ref[2:3, :]) * (1.0 + sc2) + sh2).astype(BF16)
    f = None
    for c in range(D_FF // FF_CHUNK):
        lo = c * FF_CHUNK
        gate = jnp.dot(hb, wi_ref[:, lo:lo + FF_CHUNK], preferred_element_type=F32)
        up = jnp.dot(hb, wi_ref[:, D_FF + lo:D_FF + lo + FF_CHUNK], preferred_element_type=F32)
        part = _mm(_silu(gate) * up, wo_ref[lo:lo + FF_CHUNK, :])
        f = part if f is None else f + part
    o_ref[...] = x + g2 * _rms(f, nw_ref[3:4, :])


def _ffn(x, mod, norm_w, w_in, w_out, seq_len):
    t = x.shape[0]
    tm = 256
    kern = functools.partial(_ffn_kernel, tm=tm, seq_len=seq_len)
    row = lambda i: (i, 0)
    const = lambda i: (0, 0)
    return pl.pallas_call(
        kern,
        grid=(t // tm,),
        in_specs=[pl.BlockSpec((tm, D_MODEL), row),
                  pl.BlockSpec(mod.shape, const),
                  pl.BlockSpec(norm_w.shape, const),
                  pl.BlockSpec((D_MODEL, 2 * D_FF), const),
                  pl.BlockSpec((D_FF, D_MODEL), const)],
        out_specs=pl.BlockSpec((tm, D_MODEL), row),
        out_shape=jax.ShapeDtypeStruct((t, D_MODEL), F32),
        compiler_params=_cparams(("arbitrary",)),
        name="swiglu",
    )(x, mod, norm_w, w_in, w_out)


def _rope_tables(seq_len):
    rows = seq_len // GRID_W
    row_idx = jnp.repeat(jnp.arange(rows, dtype=F32), GRID_W)
    col_idx = (jnp.arange(seq_len) % GRID_W).astype(F32)
    n_freq = DK // 4
    freqs = ROPE_BASE ** (-jnp.arange(n_freq, dtype=F32) / n_freq)
    ang = jnp.concatenate([row_idx[:, None] * freqs, col_idx[:, None] * freqs], axis=-1)
    cos = jnp.repeat(jnp.cos(ang), 2, axis=-1)
    sin = jnp.repeat(jnp.sin(ang), 2, axis=-1)
    sign = jnp.tile(jnp.array([-1.0, 1.0], F32), DK // 2)
    return cos, sin * sign


def _one_path(x, mod, seq_len, n_seq, wts, rope_tabs, s_ret0, s_dn0, emit_state):
    rows_per_cond = seq_len if mod.shape[0] > 1 else x.shape[0]
    proj, ba = _inproj(x, mod, wts["norm_w"], wts["w_main"], wts["w_ba"], rows_per_cond)
    ret = _retention(proj, wts["logit_rep"], wts["ret_gn_w"], n_seq, seq_len,
                     rope_tabs=rope_tabs, s0=s_ret0, emit_state=emit_state)
    gdn = _gdn(proj, ba, wts["conv_w"], wts["a_log"], wts["dt_bias"], wts["dn_norm_w"],
               n_seq, seq_len, s0=s_dn0, emit_state=emit_state)
    x1 = _merge(x, ret[0], gdn[0], proj, mod, wts["norm_w"], wts["w_ret_o"], wts["w_dn_o"],
                wts["w_out"], rows_per_cond)
    y = _ffn(x1, mod, wts["norm_w"], wts["w_ffn_in"], wts["w_ffn_out"], rows_per_cond)
    if emit_state:
        return y, ret[1], gdn[1]
    return y, None, None


def kernel(x_prompt, x_sample, c, state_ret, state_dn, c_ctx, w_mod, b_mod, norm_w, w_in, conv_w,
           ret_decay_logit, ret_gn_w, dn_a_log, dn_dt_bias, dn_norm_w, w_ret_o, w_dn_o, w_out,
           w_ffn_in, w_ffn_out):
    n_ctx, l_ctx, _ = x_prompt.shape
    n_lat, l_lat, _ = x_sample.shape
    assert w_mod.shape[0] == 1, "single-layer kernel"

    cond8 = jnp.zeros((8, D_MODEL), F32).at[0].set(c_ctx).at[1:1 + n_lat].set(c)
    mod = _modulation(cond8, w_mod[0], b_mod)
    mod_ctx, mod_lat = mod[0:1], mod[1:1 + n_lat]

    wi = w_in[0]
    ba_lo = 2 * QK_W + 2 * V_W + 2 * QK_W + 2 * V_W
    w_main = jnp.concatenate([wi[:, :ba_lo], wi[:, ba_lo + 2 * N_GATE:]], axis=1).astype(BF16)
    wts = {
        "norm_w": norm_w[0],
        "w_main": w_main,
        "w_ba": wi[:, ba_lo:ba_lo + 2 * N_GATE].astype(BF16),
        "logit_rep": jnp.broadcast_to(ret_decay_logit[0].T[:, :, None], (HEADS, 2, DK)),
        "ret_gn_w": ret_gn_w,
        "conv_w": conv_w[0],
        "a_log": dn_a_log.reshape(1, N_GATE),
        "dt_bias": dn_dt_bias.reshape(1, N_GATE),
        "dn_norm_w": dn_norm_w,
        "w_ret_o": w_ret_o[0].astype(BF16),
        "w_dn_o": w_dn_o[0].astype(BF16),
        "w_out": w_out[0].astype(BF16),
        "w_ffn_in": w_ffn_in[0].astype(BF16),
        "w_ffn_out": w_ffn_out[0].astype(BF16),
    }

    y_p, s_ret, s_dn = _one_path(x_prompt.reshape(n_ctx * l_ctx, D_MODEL), mod_ctx, l_ctx, n_ctx,
                                 wts, None, None, None, True)
    y_s, _, _ = _one_path(x_sample.reshape(n_lat * l_lat, D_MODEL), mod_lat, l_lat, n_lat,
                          wts, _rope_tables(l_lat), state_ret, state_dn, False)
    return (y_p.reshape(x_prompt.shape), y_s.reshape(x_sample.shape), s_ret, s_dn)
```

```python
import functools

import jax
import jax.numpy as jnp
from jax import lax
from jax.experimental import pallas as pl
from jax.experimental.pallas import tpu as pltpu

F32 = jnp.float32
BF16 = jnp.bfloat16

D_MODEL = 1024
HEADS = 4
DK = 128
DV = 256
QK_W = HEADS * DK
V_W = HEADS * DV
GDN_CHUNK = 64
BLK = 256
CONV_WIDTH = 3
GRID_W = 64
ROPE_BASE = 10000.0
EPS = 1e-6
D_FF = 2816
FF_CHUNK = 1408
N_GATE = 2 * HEADS

COL_RQ, COL_RK, COL_RV, COL_RG = 0, 512, 1024, 2048
COL_DQ, COL_DK, COL_DV, COL_DZ = 3072, 3584, 4096, 5120
COL_GR, COL_GD = 6144, 7168
PROJ_W = 8192

VMEM_LIMIT = 56 * 1024 * 1024


def _cparams(sem):
    return pltpu.CompilerParams(dimension_semantics=sem, vmem_limit_bytes=VMEM_LIMIT)


def _mm(a, b):
    return jnp.dot(a.astype(BF16), b.astype(BF16), preferred_element_type=F32)


def _mm_nt(a, b):
    return lax.dot_general(a.astype(BF16), b.astype(BF16), (((1,), (1,)), ((), ())),
                           preferred_element_type=F32)


def _mm_tn(a, b):
    return lax.dot_general(a.astype(BF16), b.astype(BF16), (((0,), (0,)), ((), ())),
                           preferred_element_type=F32)


def _mm_f32(a, b):
    return jnp.dot(a, b, precision=lax.Precision.HIGHEST, preferred_element_type=F32)


def _sigmoid(x):
    return 1.0 / (1.0 + jnp.exp(-x))


def _silu(x):
    return x * _sigmoid(x)


def _softplus(x):
    return jnp.maximum(x, 0.0) + jnp.log(1.0 + jnp.exp(-jnp.abs(x)))


def _log_sigmoid(x):
    return -_softplus(-x)


def _rms(x, w):
    return x * lax.rsqrt(jnp.mean(x * x, axis=-1, keepdims=True) + EPS) * w


def _mod_kernel(c_ref, w_ref, b_ref, o_ref):
    o_ref[...] = _mm(_silu(c_ref[...]), w_ref[...]) + b_ref[...]


def _modulation(cond8, w_mod, b_mod):
    n = w_mod.shape[1]
    tn = 1536
    return pl.pallas_call(
        _mod_kernel,
        grid=(n // tn,),
        in_specs=[pl.BlockSpec((8, D_MODEL), lambda j: (0, 0)),
                  pl.BlockSpec((D_MODEL, tn), lambda j: (0, j)),
                  pl.BlockSpec((1, tn), lambda j: (0, j))],
        out_specs=pl.BlockSpec((8, tn), lambda j: (0, j)),
        out_shape=jax.ShapeDtypeStruct((8, n), F32),
        compiler_params=_cparams(("arbitrary",)),
        name="modulation",
    )(cond8, w_mod, b_mod)


def _inproj_kernel(x_ref, mod_ref, nw_ref, w_ref, wba_ref, o_ref, ba_ref, h_sc, *, tm, seq_len):
    i = pl.program_id(0)
    j = pl.program_id(1)

    @pl.when(j == 0)
    def _():
        ci = (i * tm) // seq_len
        m = mod_ref[pl.ds(ci, 1), :]
        sh1 = m[:, 0:D_MODEL]
        sc1 = m[:, D_MODEL:2 * D_MODEL]
        h = _rms(x_ref[...], nw_ref[0:1, :]) * (1.0 + sc1) + sh1
        hb = h.astype(BF16)
        h_sc[...] = hb
        ba_ref[...] = jnp.dot(hb, wba_ref[...], preferred_element_type=F32)

    o_ref[...] = jnp.dot(h_sc[...], w_ref[...], preferred_element_type=F32)


def _inproj(x, mod, norm_w, w_main, w_ba, seq_len):
    t = x.shape[0]
    tm, tn = 512, 2048
    kern = functools.partial(_inproj_kernel, tm=tm, seq_len=seq_len)
    return pl.pallas_call(
        kern,
        grid=(t // tm, PROJ_W // tn),
        in_specs=[pl.BlockSpec((tm, D_MODEL), lambda i, j: (i, 0)),
                  pl.BlockSpec(mod.shape, lambda i, j: (0, 0)),
                  pl.BlockSpec(norm_w.shape, lambda i, j: (0, 0)),
                  pl.BlockSpec((D_MODEL, tn), lambda i, j: (0, j)),
                  pl.BlockSpec((D_MODEL, 2 * N_GATE), lambda i, j: (0, 0))],
        out_specs=[pl.BlockSpec((tm, tn), lambda i, j: (i, j)),
                   pl.BlockSpec((tm, 2 * N_GATE), lambda i, j: (i, 0))],
        out_shape=[jax.ShapeDtypeStruct((t, PROJ_W), F32),
                   jax.ShapeDtypeStruct((t, 2 * N_GATE), F32)],
        scratch_shapes=[pltpu.VMEM((tm, D_MODEL), BF16)],
        compiler_params=_cparams(("arbitrary", "arbitrary")),
        name="inproj",
    )(x, mod, norm_w, w_main, w_ba)


def _pair_swap(x):
    lane = lax.broadcasted_iota(jnp.int32, x.shape, 1)
    nxt = pltpu.roll(x, x.shape[1] - 1, 1)
    prv = pltpu.roll(x, 1, 1)
    return jnp.where((lane & 1) == 0, nxt, prv)


def _ret_kernel(*refs, seq_len, rope, has_s0, emit_state):
    it = iter(refs)
    q_ref, k_ref, v_ref = next(it), next(it), next(it)
    cos_ref = next(it) if rope else None
    sin_ref = next(it) if rope else None
    logit_ref, gnw_ref = next(it), next(it)
    s0_ref = next(it) if has_s0 else None
    o_ref = next(it)
    st_ref = next(it) if emit_state else None
    q_sc, k_sc, sf_sc, sb_sc = next(it), next(it), next(it), next(it)

    nb = seq_len // BLK
    q = q_ref[...] * (DK ** -0.5)
    k = k_ref[...]
    if rope:
        cos = cos_ref[...]
        sin = sin_ref[...]
        q = q * cos + _pair_swap(q) * sin
        k = k * cos + _pair_swap(k) * sin
    q_sc[...] = q
    k_sc[...] = k

    lg = _log_sigmoid(logit_ref[...])
    lgf, lgb = lg[0:1, :], lg[1:2, :]
    lgf1, lgb1 = lgf[:, 0:1], lgb[:, 0:1]
    pos = lax.broadcasted_iota(jnp.int32, (BLK, DK), 0).astype(F32)
    kdec_f = jnp.exp(lgf * (BLK - 1.0 - pos))
    kdec_b = jnp.exp(lgb * pos)
    qdec_f = jnp.exp(lgf * (pos + 1.0))
    qdec_b = jnp.exp(lgb * (BLK - pos))
    cdec_f = jnp.exp(lgf1 * float(BLK))
    cdec_b = jnp.exp(lgb1 * float(BLK))
    ii = lax.broadcasted_iota(jnp.int32, (BLK, BLK), 0)
    jj = lax.broadcasted_iota(jnp.int32, (BLK, BLK), 1)
    diff = (ii - jj).astype(F32)
    dmat = (jnp.where(diff >= 0, jnp.exp(lgf1 * jnp.maximum(diff, 0.0)), 0.0)
            + jnp.where(diff <= 0, jnp.exp(lgb1 * jnp.maximum(-diff, 0.0)), 0.0))

    def contrib(c, dec):
        kc = k_sc[c * BLK:(c + 1) * BLK, :]
        vc = v_ref[c * BLK:(c + 1) * BLK, :]
        return _mm_tn(kc * dec, vc)

    s = s0_ref[0] if has_s0 else None
    for c in range(nb):
        if s is not None:
            sf_sc[c] = s
        if c < nb - 1 or emit_state:
            a = contrib(c, kdec_f)
            s = a if s is None else s * cdec_f + a
    if emit_state:
        st_ref[0] = s
    s = s0_ref[1] if has_s0 else None
    for c in range(nb - 1, -1, -1):
        if s is not None:
            sb_sc[c] = s
        if c > 0 or emit_state:
            a = contrib(c, kdec_b)
            s = a if s is None else s * cdec_b + a
    if emit_state:
        st_ref[1] = s

    gnw = gnw_ref[...]
    for c in range(nb):
        rows = slice(c * BLK, (c + 1) * BLK)
        qc = q_sc[rows, :]
        kc = k_sc[rows, :]
        vc = v_ref[rows, :]
        o = _mm(_mm_nt(qc, kc) * dmat, vc)
        if has_s0 or c > 0:
            o = o + _mm(qc * qdec_f, sf_sc[c])
        if has_s0 or c < nb - 1:
            o = o + _mm(qc * qdec_b, sb_sc[c])
        mu = jnp.mean(o, axis=-1, keepdims=True)
        oc = o - mu
        var = jnp.mean(oc * oc, axis=-1, keepdims=True)
        o_ref[rows, :] = oc * lax.rsqrt(var + EPS) * gnw


def _retention(proj, logit_rep, gn_w, n_seq, seq_len, rope_tabs=None, s0=None, emit_state=False):
    t = proj.shape[0]
    rope = rope_tabs is not None
    has_s0 = s0 is not None
    kern = functools.partial(_ret_kernel, seq_len=seq_len, rope=rope, has_s0=has_s0,
                             emit_state=emit_state)
    in_specs = [pl.BlockSpec((seq_len, DK), lambda b, h: (b, COL_RQ // DK + h)),
                pl.BlockSpec((seq_len, DK), lambda b, h: (b, COL_RK // DK + h)),
                pl.BlockSpec((seq_len, DV), lambda b, h: (b, COL_RV // DV + h))]
    args = [proj, proj, proj]
    if rope:
        in_specs += [pl.BlockSpec((seq_len, DK), lambda b, h: (0, 0))] * 2
        args += list(rope_tabs)
    in_specs += [pl.BlockSpec((None, 2, DK), lambda b, h: (h, 0, 0)),
                 pl.BlockSpec((1, DV), lambda b, h: (0, h))]
    args += [logit_rep, gn_w]
    if has_s0:
        in_specs.append(pl.BlockSpec((None, None, 2, None, DK, DV), lambda b, h: (b, 0, 0, h, 0, 0)))
        args.append(s0)
    out_specs = [pl.BlockSpec((seq_len, DV), lambda b, h: (b, h))]
    out_shape = [jax.ShapeDtypeStruct((t, V_W), F32)]
    if emit_state:
        out_specs.append(pl.BlockSpec((None, None, 2, None, DK, DV), lambda b, h: (b, 0, 0, h, 0, 0)))
        out_shape.append(jax.ShapeDtypeStruct((n_seq, 1, 2, HEADS, DK, DV), F32))
    nb = seq_len // BLK
    res = pl.pallas_call(
        kern,
        grid=(n_seq, HEADS),
        in_specs=in_specs,
        out_specs=out_specs,
        out_shape=out_shape,
        scratch_shapes=[pltpu.VMEM((seq_len, DK), F32), pltpu.VMEM((seq_len, DK), F32),
                        pltpu.VMEM((nb, DK, DV), F32), pltpu.VMEM((nb, DK, DV), F32)],
        compiler_params=_cparams(("arbitrary", "arbitrary")),
        name="retention",
    )(*args)
    return res


TRI_BASE = 8


def _tri_inverse(lm, ii, jj):
    def blk(v, size):
        return lax.shift_right_logical(v, size.bit_length() - 1)

    eye = (ii == jj).astype(F32)
    x = jnp.where(blk(ii, TRI_BASE) == blk(jj, TRI_BASE), -lm, 0.0)
    t = eye + x
    for _ in range(TRI_BASE.bit_length() - 2):
        x = _mm(x, x)
        t = t + _mm(t, x)
    size = TRI_BASE
    while size < GDN_CHUNK:
        pair = (blk(ii, 2 * size) == blk(jj, 2 * size)) & (blk(ii, size) != blk(jj, size))
        e = jnp.where(pair, lm, 0.0)
        t = t - _mm(t, _mm(e, t))
        size *= 2
    return t


def _gdn_kernel(*refs, seq_len, has_s0, emit_state):
    it = iter(refs)
    q_ref, k_ref, v_ref, z_ref, ba_ref = (next(it) for _ in range(5))
    cwq_ref, cwk_ref, cwv_ref = next(it), next(it), next(it)
    alog_ref, dtb_ref, nw_ref = next(it), next(it), next(it)
    s0_ref = next(it) if has_s0 else None
    o_ref = next(it)
    st_ref = next(it) if emit_state else None
    (qn_sc, kn_sc, vn_sc, a_sc, b_sc, od_sc, u_sc, w_sc, s_sc, qg_sc, kd_sc, eg_sc,
     st_sc) = (next(it) for _ in range(13))

    h = pl.program_id(1)
    nb = seq_len // BLK
    nck = BLK // GDN_CHUNK

    def conv_block(ref, cw_ref, r0):
        cur = ref[pl.ds(r0, BLK), :]
        width = cur.shape[1]
        row = lax.broadcasted_iota(jnp.int32, (BLK, width), 0)
        before = ref[pl.ds(jnp.maximum(r0 - 1, 0), 1), :]
        after = ref[pl.ds(jnp.minimum(r0 + BLK, seq_len - 1), 1), :]
        before = jnp.where(r0 > 0, before, 0.0)
        after = jnp.where(r0 + BLK < seq_len, after, 0.0)
        prev = jnp.where(row == 0, before, pltpu.roll(cur, 1, 0))
        nxt = jnp.where(row == BLK - 1, after, pltpu.roll(cur, BLK - 1, 0))
        cw = cw_ref[...]
        return _silu(prev * cw[0:1, :] + cur * cw[1:2, :] + nxt * cw[2:3, :])

    def l2n(x):
        return x * lax.rsqrt(jnp.sum(x * x, axis=-1, keepdims=True) + EPS)

    lane8 = lax.broadcasted_iota(jnp.int32, (1, N_GATE), 1)

    def prep(bi, carry):
        r0 = pl.multiple_of(bi * BLK, BLK)
        qn_sc[pl.ds(r0, BLK), :] = l2n(conv_block(q_ref, cwq_ref, r0)) * (DK ** -0.5)
        kn_sc[pl.ds(r0, BLK), :] = l2n(conv_block(k_ref, cwk_ref, r0))
        vn_sc[pl.ds(r0, BLK), :] = conv_block(v_ref, cwv_ref, r0)
        ba = ba_ref[pl.ds(r0, BLK), :]
        beta = _sigmoid(ba[:, 0:N_GATE])
        la = -jnp.exp(alog_ref[...]) * _softplus(ba[:, N_GATE:2 * N_GATE] + dtb_ref[...])
        for d in range(2):
            oh = lane8 == (d * HEADS + h)
            bcol = jnp.sum(jnp.where(oh, beta, 0.0), axis=1, keepdims=True)
            acol = jnp.sum(jnp.where(oh, la, 0.0), axis=1, keepdims=True)
            b_sc[d, pl.ds(r0, BLK), :] = jnp.broadcast_to(bcol, (BLK, DK))
            a_sc[d, pl.ds(r0, BLK), :] = jnp.broadcast_to(acol, (BLK, DK))
        return carry

    lax.fori_loop(0, nb, prep, 0)

    for d in range(2):
        if has_s0:
            st_sc[d] = s0_ref[d]
        else:
            st_sc[d] = jnp.zeros((DK, DV), F32)

    ii = lax.broadcasted_iota(jnp.int32, (BLK, BLK), 0)
    jj = lax.broadcasted_iota(jnp.int32, (BLK, BLK), 1)
    chunk_shift = GDN_CHUNK.bit_length() - 1
    same = lax.shift_right_logical(ii, chunk_shift) == lax.shift_right_logical(jj, chunk_shift)
    diag = ii == jj

    def to_col(row):
        return jnp.sum(jnp.where(diag, jnp.broadcast_to(row, (BLK, BLK)), 0.0), axis=1, keepdims=True)

    def block_step(step, carry):
        for d in range(2):
            bi = step if d == 0 else nb - 1 - step
            r0 = pl.multiple_of(bi * BLK, BLK)
            q = qn_sc[pl.ds(r0, BLK), :]
            k = kn_sc[pl.ds(r0, BLK), :]
            v = vn_sc[pl.ds(r0, BLK), :]
            a = a_sc[d, pl.ds(r0, BLK), :]
            b = b_sc[d, pl.ds(r0, BLK), :]
            a2 = jnp.concatenate([a, a], axis=1)
            b2 = jnp.concatenate([b, b], axis=1)
            if d == 0:
                earlier_eq = same & (jj <= ii)
                earlier = same & (jj < ii)
                t_earlier_eq = same & (ii <= jj)
                t_later = same & (ii > jj)
            else:
                earlier_eq = same & (jj >= ii)
                earlier = same & (jj > ii)
                t_earlier_eq = same & (ii >= jj)
                t_later = same & (ii < jj)
            g_row = jnp.sum(jnp.where(t_earlier_eq, a2, 0.0), axis=0, keepdims=True)
            r_row = jnp.sum(jnp.where(t_later, a2, 0.0), axis=0, keepdims=True)
            g_col = to_col(g_row)
            r_col = to_col(r_row)
            eg_col = jnp.exp(g_col)
            decay = jnp.where(earlier_eq, jnp.exp(jnp.minimum(g_col - g_row, 0.0)), 0.0)
            kb = k * b
            lm = jnp.where(earlier, _mm_nt(kb, k) * decay, 0.0)
            tinv = _tri_inverse(lm, ii, jj)
            rhs = jnp.concatenate([v * b2, kb * eg_col], axis=1)
            uw = _mm(tinv, rhs)
            u_sc[d] = uw[:, 0:DV]
            w_sc[d] = uw[:, DV:DV + DK]
            s_sc[d] = jnp.where(earlier_eq, _mm_nt(q, k) * decay, 0.0)
            qg_sc[d] = q * eg_col
            kd_sc[d] = k * jnp.exp(r_col)
            eg_sc[d] = jnp.broadcast_to(jnp.exp(g_row + r_row), (8, BLK))

            for ci in range(nck):
                cc = ci if d == 0 else nck - 1 - ci
                rs = slice(cc * GDN_CHUNK, (cc + 1) * GDN_CHUNK)
                s_prev = st_sc[d]
                wq = jnp.concatenate([w_sc[d, rs, :], qg_sc[d, rs, :]], axis=0)
                ws_qs = _mm(wq, s_prev)
                v_new = u_sc[d, rs, :] - ws_qs[0:GDN_CHUNK]
                o = ws_qs[GDN_CHUNK:] + _mm(s_sc[d, rs, rs], v_new)
                od_sc[d, pl.ds(r0 + cc * GDN_CHUNK, GDN_CHUNK), :] = o
                egl = eg_sc[d, 0:1, cc * GDN_CHUNK:cc * GDN_CHUNK + 1]
                st_sc[d] = s_prev * egl + _mm_tn(kd_sc[d, rs, :], v_new)
        return carry

    lax.fori_loop(0, nb, block_step, 0)

    if emit_state:
        st_ref[0] = st_sc[0]
        st_ref[1] = st_sc[1]

    def finish(bi, carry):
        r0 = pl.multiple_of(bi * BLK, BLK)
        o = od_sc[0, pl.ds(r0, BLK), :] + od_sc[1, pl.ds(r0, BLK), :]
        o = _rms(o, nw_ref[...])
        o_ref[pl.ds(r0, BLK), :] = o * _silu(z_ref[pl.ds(r0, BLK), :])
        return carry

    lax.fori_loop(0, nb, finish, 0)


def _gdn(proj, ba, conv_w, a_log, dt_bias, norm_w, n_seq, seq_len, s0=None, emit_state=False):
    t = proj.shape[0]
    has_s0 = s0 is not None
    kern = functools.partial(_gdn_kernel, seq_len=seq_len, has_s0=has_s0, emit_state=emit_state)
    in_specs = [pl.BlockSpec((seq_len, DK), lambda b, h: (b, COL_DQ // DK + h)),
                pl.BlockSpec((seq_len, DK), lambda b, h: (b, COL_DK // DK + h)),
                pl.BlockSpec((seq_len, DV), lambda b, h: (b, COL_DV // DV + h)),
                pl.BlockSpec((seq_len, DV), lambda b, h: (b, COL_DZ // DV + h)),
                pl.BlockSpec((seq_len, 2 * N_GATE), lambda b, h: (b, 0)),
                pl.BlockSpec((CONV_WIDTH, DK), lambda b, h: (0, h)),
                pl.BlockSpec((CONV_WIDTH, DK), lambda b, h: (0, QK_W // DK + h)),
                pl.BlockSpec((CONV_WIDTH, DV), lambda b, h: (0, 2 * QK_W // DV + h)),
                pl.BlockSpec((1, N_GATE), lambda b, h: (0, 0)),
                pl.BlockSpec((1, N_GATE), lambda b, h: (0, 0)),
                pl.BlockSpec((1, DV), lambda b, h: (0, 0))]
    args = [proj, proj, proj, proj, ba, conv_w, conv_w, conv_w, a_log, dt_bias, norm_w]
    if has_s0:
        in_specs.append(pl.BlockSpec((None, None, 2, None, DK, DV), lambda b, h: (b, 0, 0, h, 0, 0)))
        args.append(s0)
    out_specs = [pl.BlockSpec((seq_len, DV), lambda b, h: (b, h))]
    out_shape = [jax.ShapeDtypeStruct((t, V_W), F32)]
    if emit_state:
        out_specs.append(pl.BlockSpec((None, None, 2, None, DK, DV), lambda b, h: (b, 0, 0, h, 0, 0)))
        out_shape.append(jax.ShapeDtypeStruct((n_seq, 1, 2, HEADS, DK, DV), F32))
    scratch = [pltpu.VMEM((seq_len, DK), F32), pltpu.VMEM((seq_len, DK), F32),
               pltpu.VMEM((seq_len, DV), F32),
               pltpu.VMEM((2, seq_len, DK), F32), pltpu.VMEM((2, seq_len, DK), F32),
               pltpu.VMEM((2, seq_len, DV), F32),
               pltpu.VMEM((2, BLK, DV), F32), pltpu.VMEM((2, BLK, DK), F32),
               pltpu.VMEM((2, BLK, BLK), F32), pltpu.VMEM((2, BLK, DK), F32),
               pltpu.VMEM((2, BLK, DK), F32), pltpu.VMEM((2, 8, BLK), F32),
               pltpu.VMEM((2, DK, DV), F32)]
    return pl.pallas_call(
        kern,
        grid=(n_seq, HEADS),
        in_specs=in_specs,
        out_specs=out_specs,
        out_shape=out_shape,
        scratch_shapes=scratch,
        compiler_params=_cparams(("arbitrary", "arbitrary")),
        name="gated_delta",
    )(*args)


def _merge_kernel(x_ref, or_ref, od_ref, rg_ref, gr_ref, gd_ref, mod_ref, nw_ref,
                  wr_ref, wd_ref, wo_ref, o_ref, *, tm, seq_len):
    i = pl.program_id(0)
    ci = (i * tm) // seq_len
    m = mod_ref[pl.ds(ci, 1), :]
    g1 = m[:, 2 * D_MODEL:3 * D_MODEL]
    y_r = _mm(_silu(rg_ref[...]) * or_ref[...], wr_ref[...])
    y_d = _mm(od_ref[...], wd_ref[...])
    merged = _sigmoid(gr_ref[...]) * y_r + _sigmoid(gd_ref[...]) * y_d
    mo = _mm(merged, wo_ref[...])
    o_ref[...] = x_ref[...] + g1 * _rms(mo, nw_ref[1:2, :])


def _merge(x, o_r, o_d, proj, mod, norm_w, w_ret_o, w_dn_o, w_out, seq_len):
    t = x.shape[0]
    tm = 256
    kern = functools.partial(_merge_kernel, tm=tm, seq_len=seq_len)
    row = lambda i: (i, 0)
    const = lambda i: (0, 0)
    wspec = pl.BlockSpec((D_MODEL, D_MODEL), const)
    return pl.pallas_call(
        kern,
        grid=(t // tm,),
        in_specs=[pl.BlockSpec((tm, D_MODEL), row),
                  pl.BlockSpec((tm, V_W), row),
                  pl.BlockSpec((tm, V_W), row),
                  pl.BlockSpec((tm, V_W), lambda i: (i, COL_RG // V_W)),
                  pl.BlockSpec((tm, D_MODEL), lambda i: (i, COL_GR // D_MODEL)),
                  pl.BlockSpec((tm, D_MODEL), lambda i: (i, COL_GD // D_MODEL)),
                  pl.BlockSpec(mod.shape, const),
                  pl.BlockSpec(norm_w.shape, const),
                  wspec, wspec, wspec],
        out_specs=pl.BlockSpec((tm, D_MODEL), row),
        out_shape=jax.ShapeDtypeStruct((t, D_MODEL), F32),
        compiler_params=_cparams(("arbitrary",)),
        name="merge_out",
    )(x, o_r, o_d, proj, proj, proj, mod, norm_w, w_ret_o, w_dn_o, w_out)


def _ffn_kernel(x_ref, mod_ref, nw_ref, wi_ref, wo_ref, o_ref, *, tm, seq_len):
    i = pl.program_id(0)
    ci = (i * tm) // seq_len
    m = mod_ref[pl.ds(ci, 1), :]
    sh2 = m[:, 3 * D_MODEL:4 * D_MODEL]
    sc2 = m[:, 4 * D_MODEL:5 * D_MODEL]
    g2 = m[:, 5 * D_MODEL:6 * D_MODEL]
    x = x_ref[...]
    hb = (_rms(x, nw_ref[2:3, :]) * (1.0 + sc2) + sh2).astype(BF16)
    f = None
    for c in range(D_FF // FF_CHUNK):
        lo = c * FF_CHUNK
        gate = jnp.dot(hb, wi_ref[:, lo:lo + FF_CHUNK], preferred_element_type=F32)
        up = jnp.dot(hb, wi_ref[:, D_FF + lo:D_FF + lo + FF_CHUNK], preferred_element_type=F32)
        part = _mm(_silu(gate) * up, wo_ref[lo:lo + FF_CHUNK, :])
        f = part if f is None else f + part
    o_ref[...] = x + g2 * _rms(f, nw_ref[3:4, :])


def _ffn(x, mod, norm_w, w_in, w_out, seq_len):
    t = x.shape[0]
    tm = 256
    kern = functools.partial(_ffn_kernel, tm=tm, seq_len=seq_len)
    row = lambda i: (i, 0)
    const = lambda i: (0, 0)
    return pl.pallas_call(
        kern,
        grid=(t // tm,),
        in_specs=[pl.BlockSpec((tm, D_MODEL), row),
                  pl.BlockSpec(mod.shape, const),
                  pl.BlockSpec(norm_w.shape, const),
                  pl.BlockSpec((D_MODEL, 2 * D_FF), const),
                  pl.BlockSpec((D_FF, D_MODEL), const)],
        out_specs=pl.BlockSpec((tm, D_MODEL), row),
        out_shape=jax.ShapeDtypeStruct((t, D_MODEL), F32),
        compiler_params=_cparams(("arbitrary",)),
        name="swiglu",
    )(x, mod, norm_w, w_in, w_out)


def _rope_tables(seq_len):
    rows = seq_len // GRID_W
    row_idx = jnp.repeat(jnp.arange(rows, dtype=F32), GRID_W)
    col_idx = (jnp.arange(seq_len) % GRID_W).astype(F32)
    n_freq = DK // 4
    freqs = ROPE_BASE ** (-jnp.arange(n_freq, dtype=F32) / n_freq)
    ang = jnp.concatenate([row_idx[:, None] * freqs, col_idx[:, None] * freqs], axis=-1)
    cos = jnp.repeat(jnp.cos(ang), 2, axis=-1)
    sin = jnp.repeat(jnp.sin(ang), 2, axis=-1)
    sign = jnp.tile(jnp.array([-1.0, 1.0], F32), DK // 2)
    return cos, sin * sign


def _one_path(x, mod, seq_len, n_seq, wts, rope_tabs, s_ret0, s_dn0, emit_state):
    rows_per_cond = seq_len if mod.shape[0] > 1 else x.shape[0]
    proj, ba = _inproj(x, mod, wts["norm_w"], wts["w_main"], wts["w_ba"], rows_per_cond)
    ret = _retention(proj, wts["logit_rep"], wts["ret_gn_w"], n_seq, seq_len,
                     rope_tabs=rope_tabs, s0=s_ret0, emit_state=emit_state)
    gdn = _gdn(proj, ba, wts["conv_w"], wts["a_log"], wts["dt_bias"], wts["dn_norm_w"],
               n_seq, seq_len, s0=s_dn0, emit_state=emit_state)
    x1 = _merge(x, ret[0], gdn[0], proj, mod, wts["norm_w"], wts["w_ret_o"], wts["w_dn_o"],
                wts["w_out"], rows_per_cond)
    y = _ffn(x1, mod, wts["norm_w"], wts["w_ffn_in"], wts["w_ffn_out"], rows_per_cond)
    if emit_state:
        return y, ret[1], gdn[1]
    return y, None, None


def kernel(x_prompt, x_sample, c, state_ret, state_dn, c_ctx, w_mod, b_mod, norm_w, w_in, conv_w,
           ret_decay_logit, ret_gn_w, dn_a_log, dn_dt_bias, dn_norm_w, w_ret_o, w_dn_o, w_out,
           w_ffn_in, w_ffn_out):
    n_ctx, l_ctx, _ = x_prompt.shape
    n_lat, l_lat, _ = x_sample.shape
    assert w_mod.shape[0] == 1, "single-layer kernel"

    cond8 = jnp.zeros((8, D_MODEL), F32).at[0].set(c_ctx).at[1:1 + n_lat].set(c)
    mod = _modulation(cond8, w_mod[0], b_mod)
    mod_ctx, mod_lat = mod[0:1], mod[1:1 + n_lat]

    wi = w_in[0]
    ba_lo = 2 * QK_W + 2 * V_W + 2 * QK_W + 2 * V_W
    w_main = jnp.concatenate([wi[:, :ba_lo], wi[:, ba_lo + 2 * N_GATE:]], axis=1).astype(BF16)
    wts = {
        "norm_w": norm_w[0],
        "w_main": w_main,
        "w_ba": wi[:, ba_lo:ba_lo + 2 * N_GATE].astype(BF16),
        "logit_rep": jnp.broadcast_to(ret_decay_logit[0].T[:, :, None], (HEADS, 2, DK)),
        "ret_gn_w": ret_gn_w,
        "conv_w": conv_w[0],
        "a_log": dn_a_log.reshape(1, N_GATE),
        "dt_bias": dn_dt_bias.reshape(1, N_GATE),
        "dn_norm_w": dn_norm_w,
        "w_ret_o": w_ret_o[0].astype(BF16),
        "w_dn_o": w_dn_o[0].astype(BF16),
        "w_out": w_out[0].astype(BF16),
        "w_ffn_in": w_ffn_in[0].astype(BF16),
        "w_ffn_out": w_ffn_out[0].astype(BF16),
    }

    y_p, s_ret, s_dn = _one_path(x_prompt.reshape(n_ctx * l_ctx, D_MODEL), mod_ctx, l_ctx, n_ctx,
                                 wts, None, None, None, True)
    y_s, _, _ = _one_path(x_sample.reshape(n_lat * l_lat, D_MODEL), mod_lat, l_lat, n_lat,
                          wts, _rope_tables(l_lat), state_ret, state_dn, False)
    return (y_p.reshape(x_prompt.shape), y_s.reshape(x_sample.shape), s_ret, s_dn)
```

```python
import functools

import jax
import jax.numpy as jnp
from jax import lax
from jax.experimental import pallas as pl
from jax.experimental.pallas import tpu as pltpu

F32 = jnp.float32
BF16 = jnp.bfloat16

D_MODEL = 1024
HEADS = 4
DK = 128
DV = 256
QK_W = HEADS * DK
V_W = HEADS * DV
GDN_CHUNK = 64
TRI_BASE = 8
BLK = 256
SUB = 8
CONV_WIDTH = 3
GRID_W = 64
ROPE_BASE = 10000.0
EPS = 1e-6
D_FF = 2816
FF_CHUNK = 1408
N_GATE = 2 * HEADS

COL_RQ, COL_RK, COL_RV, COL_RG = 0, 512, 1024, 2048
COL_DZ = 3072
COL_DQKV = 4096
COL_GR, COL_GD = 6144, 7168
PROJ_W = 8192
DQKV_W = 2 * QK_W + V_W

VMEM_LIMIT = 56 * 1024 * 1024


def _cparams(sem):
    return pltpu.CompilerParams(dimension_semantics=sem, vmem_limit_bytes=VMEM_LIMIT)


def _mm(a, b):
    return jnp.dot(a.astype(BF16), b.astype(BF16), preferred_element_type=F32)


def _mm_nt(a, b):
    return lax.dot_general(a.astype(BF16), b.astype(BF16), (((1,), (1,)), ((), ())),
                           preferred_element_type=F32)


def _mm_tn(a, b):
    return lax.dot_general(a.astype(BF16), b.astype(BF16), (((0,), (0,)), ((), ())),
                           preferred_element_type=F32)


def _sigmoid(x):
    return 1.0 / (1.0 + jnp.exp(-x))


def _silu(x):
    return x * _sigmoid(x)


def _softplus(x):
    return jnp.maximum(x, 0.0) + jnp.log(1.0 + jnp.exp(-jnp.abs(x)))


def _log_sigmoid(x):
    return -_softplus(-x)


def _rms(x, w):
    return x * lax.rsqrt(jnp.mean(x * x, axis=-1, keepdims=True) + EPS) * w


def _mod_kernel(c_ref, w_ref, b_ref, o_ref):
    o_ref[...] = _mm(_silu(c_ref[...]), w_ref[...]) + b_ref[...]


def _modulation(cond8, w_mod, b_mod):
    n = w_mod.shape[1]
    tn = 1536
    return pl.pallas_call(
        _mod_kernel,
        grid=(n // tn,),
        in_specs=[pl.BlockSpec((8, D_MODEL), lambda j: (0, 0)),
                  pl.BlockSpec((D_MODEL, tn), lambda j: (0, j)),
                  pl.BlockSpec((1, tn), lambda j: (0, j))],
        out_specs=pl.BlockSpec((8, tn), lambda j: (0, j)),
        out_shape=jax.ShapeDtypeStruct((8, n), F32),
        compiler_params=_cparams(("arbitrary",)),
        name="modulation",
    )(cond8, w_mod, b_mod)


def _inproj_kernel(x_ref, mod_ref, nw_ref, w_ref, wg_ref, o_ref, g_ref, h_sc, *, tm, rows_per_cond):
    i = pl.program_id(0)
    j = pl.program_id(1)

    @pl.when(j == 0)
    def _():
        ci = (i * tm) // rows_per_cond
        m = mod_ref[pl.ds(ci, 1), :]
        sh1 = m[:, 0:D_MODEL]
        sc1 = m[:, D_MODEL:2 * D_MODEL]
        h = _rms(x_ref[...], nw_ref[0:1, :]) * (1.0 + sc1) + sh1
        hb = h.astype(BF16)
        h_sc[...] = hb
        g_ref[...] = lax.dot_general(wg_ref[...], hb, (((1,), (1,)), ((), ())),
                                     preferred_element_type=F32)

    o_ref[...] = jnp.dot(h_sc[...], w_ref[...], preferred_element_type=F32)


def _inproj(x, mod, norm_w, w_main, w_gate_t, rows_per_cond):
    t = x.shape[0]
    tm, tn = 512, 2048
    kern = functools.partial(_inproj_kernel, tm=tm, rows_per_cond=rows_per_cond)
    return pl.pallas_call(
        kern,
        grid=(t // tm, PROJ_W // tn),
        in_specs=[pl.BlockSpec((tm, D_MODEL), lambda i, j: (i, 0)),
                  pl.BlockSpec(mod.shape, lambda i, j: (0, 0)),
                  pl.BlockSpec(norm_w.shape, lambda i, j: (0, 0)),
                  pl.BlockSpec((D_MODEL, tn), lambda i, j: (0, j)),
                  pl.BlockSpec((2 * N_GATE, D_MODEL), lambda i, j: (0, 0))],
        out_specs=[pl.BlockSpec((tm, tn), lambda i, j: (i, j)),
                   pl.BlockSpec((2 * N_GATE, tm), lambda i, j: (0, i))],
        out_shape=[jax.ShapeDtypeStruct((t, PROJ_W), F32),
                   jax.ShapeDtypeStruct((2 * N_GATE, t), F32)],
        scratch_shapes=[pltpu.VMEM((tm, D_MODEL), BF16)],
        compiler_params=_cparams(("arbitrary", "arbitrary")),
        name="inproj",
    )(x, mod, norm_w, w_main, w_gate_t)


def _pair_swap(x):
    lane = lax.broadcasted_iota(jnp.int32, x.shape, 1)
    nxt = pltpu.roll(x, x.shape[1] - 1, 1)
    prv = pltpu.roll(x, 1, 1)
    return jnp.where((lane & 1) == 0, nxt, prv)


def _ret_kernel(*refs, seq_len, rope, has_s0, emit_state):
    it = iter(refs)
    q_ref, k_ref, v_ref = next(it), next(it), next(it)
    cos_ref = next(it) if rope else None
    sin_ref = next(it) if rope else None
    logit_ref, gnw_ref = next(it), next(it)
    s0_ref = next(it) if has_s0 else None
    o_ref = next(it)
    st_ref = next(it) if emit_state else None
    q_sc, k_sc, sf_sc, sb_sc = next(it), next(it), next(it), next(it)

    nb = seq_len // BLK
    q = q_ref[...] * (DK ** -0.5)
    k = k_ref[...]
    if rope:
        cos = cos_ref[...]
        sin = sin_ref[...]
        q = q * cos + _pair_swap(q) * sin
        k = k * cos + _pair_swap(k) * sin
    q_sc[...] = q
    k_sc[...] = k

    lg = _log_sigmoid(logit_ref[...])
    lgf, lgb = lg[0:1, :], lg[1:2, :]
    lgf1, lgb1 = lgf[:, 0:1], lgb[:, 0:1]
    pos = lax.broadcasted_iota(jnp.int32, (BLK, DK), 0).astype(F32)
    kdec_f = jnp.exp(lgf * (BLK - 1.0 - pos))
    kdec_b = jnp.exp(lgb * pos)
    qdec_f = jnp.exp(lgf * (pos + 1.0))
    qdec_b = jnp.exp(lgb * (BLK - pos))
    cdec_f = jnp.exp(lgf1 * float(BLK))
    cdec_b = jnp.exp(lgb1 * float(BLK))
    ii = lax.broadcasted_iota(jnp.int32, (BLK, BLK), 0)
    jj = lax.broadcasted_iota(jnp.int32, (BLK, BLK), 1)
    diff = (ii - jj).astype(F32)
    dmat = (jnp.where(diff >= 0, jnp.exp(lgf1 * jnp.maximum(diff, 0.0)), 0.0)
            + jnp.where(diff <= 0, jnp.exp(lgb1 * jnp.maximum(-diff, 0.0)), 0.0))

    def contrib(c, dec):
        kc = k_sc[c * BLK:(c + 1) * BLK, :]
        vc = v_ref[c * BLK:(c + 1) * BLK, :]
        return _mm_tn(kc * dec, vc)

    s = s0_ref[0] if has_s0 else None
    for c in range(nb):
        if s is not None:
            sf_sc[c] = s
        if c < nb - 1 or emit_state:
            a = contrib(c, kdec_f)
            s = a if s is None else s * cdec_f + a
    if emit_state:
        st_ref[0] = s
    s = s0_ref[1] if has_s0 else None
    for c in range(nb - 1, -1, -1):
        if s is not None:
            sb_sc[c] = s
        if c > 0 or emit_state:
            a = contrib(c, kdec_b)
            s = a if s is None else s * cdec_b + a
    if emit_state:
        st_ref[1] = s

    gnw = gnw_ref[...]
    for c in range(nb):
        rows = slice(c * BLK, (c + 1) * BLK)
        qc = q_sc[rows, :]
        kc = k_sc[rows, :]
        vc = v_ref[rows, :]
        o = _mm(_mm_nt(qc, kc) * dmat, vc)
        if has_s0 or c > 0:
            o = o + _mm(qc * qdec_f, sf_sc[c])
        if has_s0 or c < nb - 1:
            o = o + _mm(qc * qdec_b, sb_sc[c])
        mu = jnp.mean(o, axis=-1, keepdims=True)
        oc = o - mu
        var = jnp.mean(oc * oc, axis=-1, keepdims=True)
        o_ref[rows, :] = oc * lax.rsqrt(var + EPS) * gnw


def _retention(proj, logit_rep, gn_w, n_seq, seq_len, rope_tabs=None, s0=None, emit_state=False):
    t = proj.shape[0]
    rope = rope_tabs is not None
    has_s0 = s0 is not None
    kern = functools.partial(_ret_kernel, seq_len=seq_len, rope=rope, has_s0=has_s0,
                             emit_state=emit_state)
    in_specs = [pl.BlockSpec((seq_len, DK), lambda b, h: (b, COL_RQ // DK + h)),
                pl.BlockSpec((seq_len, DK), lambda b, h: (b, COL_RK // DK + h)),
                pl.BlockSpec((seq_len, DV), lambda b, h: (b, COL_RV // DV + h))]
    args = [proj, proj, proj]
    if rope:
        in_specs += [pl.BlockSpec((seq_len, DK), lambda b, h: (0, 0))] * 2
        args += list(rope_tabs)
    in_specs += [pl.BlockSpec((None, 2, DK), lambda b, h: (h, 0, 0)),
                 pl.BlockSpec((1, DV), lambda b, h: (0, h))]
    args += [logit_rep, gn_w]
    if has_s0:
        in_specs.append(pl.BlockSpec((None, None, 2, None, DK, DV), lambda b, h: (b, 0, 0, h, 0, 0)))
        args.append(s0)
    out_specs = [pl.BlockSpec((seq_len, DV), lambda b, h: (b, h))]
    out_shape = [jax.ShapeDtypeStruct((t, V_W), F32)]
    if emit_state:
        out_specs.append(pl.BlockSpec((None, None, 2, None, DK, DV), lambda b, h: (b, 0, 0, h, 0, 0)))
        out_shape.append(jax.ShapeDtypeStruct((n_seq, 1, 2, HEADS, DK, DV), F32))
    nb = seq_len // BLK
    res = pl.pallas_call(
        kern,
        grid=(n_seq, HEADS),
        in_specs=in_specs,
        out_specs=out_specs,
        out_shape=out_shape,
        scratch_shapes=[pltpu.VMEM((seq_len, DK), F32), pltpu.VMEM((seq_len, DK), F32),
                        pltpu.VMEM((nb, DK, DV), F32), pltpu.VMEM((nb, DK, DV), F32)],
        compiler_params=_cparams(("arbitrary", "arbitrary")),
        name="retention",
    )(*args)
    return res


N_MASK = 8
(MASK_EQ_F, MASK_EQ_B, MASK_OFFDIAG, MASK_EYE, MASK_BASE, MASK_PAIR0) = 0, 1, 2, 3, 4, 5


def _build_masks(mask_sc, eye_sc):
    ii = lax.broadcasted_iota(jnp.int32, (BLK, BLK), 0)
    jj = lax.broadcasted_iota(jnp.int32, (BLK, BLK), 1)

    def blk(v, size):
        return lax.shift_right_logical(v, size.bit_length() - 1)

    same = blk(ii, GDN_CHUNK) == blk(jj, GDN_CHUNK)
    mask_sc[MASK_EQ_F] = (same & (jj <= ii)).astype(F32)
    mask_sc[MASK_EQ_B] = (same & (jj >= ii)).astype(F32)
    mask_sc[MASK_OFFDIAG] = (ii != jj).astype(F32)
    mask_sc[MASK_EYE] = (ii == jj).astype(F32)
    mask_sc[MASK_BASE] = (blk(ii, TRI_BASE) == blk(jj, TRI_BASE)).astype(F32)
    size, idx = TRI_BASE, MASK_PAIR0
    while size < GDN_CHUNK:
        pair = (blk(ii, 2 * size) == blk(jj, 2 * size)) & (blk(ii, size) != blk(jj, size))
        mask_sc[idx] = pair.astype(F32)
        size, idx = 2 * size, idx + 1
    eye_sc[...] = (ii == jj).astype(BF16)


def _seg_cumsums(x):
    n = x.shape[1]
    pos = lax.broadcasted_iota(jnp.int32, x.shape, 1) & (GDN_CHUNK - 1)
    up, down = x, x
    s = 1
    while s < GDN_CHUNK:
        up = up + jnp.where(pos >= s, pltpu.roll(up, s, 1), 0.0)
        down = down + jnp.where(pos < GDN_CHUNK - s, pltpu.roll(down, n - s, 1), 0.0)
        s *= 2
    return up, down


def _rows_to_cols(rows, eye_sc):
    p1 = rows.astype(BF16).astype(F32)
    r1 = rows - p1
    p2 = r1.astype(BF16).astype(F32)
    p3 = r1 - p2
    pieces = jnp.concatenate([p1, p2, p3], axis=0).astype(BF16)
    c3 = lax.dot_general(eye_sc[...], pieces, (((1,), (1,)), ((), ())), preferred_element_type=F32)
    return c3[:, 0:SUB] + c3[:, SUB:2 * SUB] + c3[:, 2 * SUB:3 * SUB]


def _gdn_kernel(*refs, nb, has_s0, emit_state):
    it = iter(refs)
    x_refs = (next(it), next(it))
    hp_refs = (next(it), next(it))
    hn_refs = (next(it), next(it))
    g_refs = (next(it), next(it))
    cw_ref, alog_ref, dtb_ref = next(it), next(it), next(it)
    s0_ref = next(it) if has_s0 else None
    o_refs = (next(it), next(it))
    st_ref = next(it) if emit_state else None
    st_sc, mask_sc, eye_sc = next(it), next(it), next(it)

    b = pl.program_id(0)
    step = pl.program_id(1)

    @pl.when((b == 0) & (step == 0))
    def _():
        _build_masks(mask_sc, eye_sc)

    @pl.when(step == 0)
    def _():
        for d in range(2):
            for h in range(HEADS):
                st_sc[d, h] = s0_ref[d, h] if has_s0 else jnp.zeros((DK, DV), F32)

    blk_of = (step, nb - 1 - step)

    row8 = lax.broadcasted_iota(jnp.int32, (N_GATE, BLK), 0)
    fwd_rows = row8 < HEADS
    logit_b = jnp.where(fwd_rows, g_refs[0][0:N_GATE, :], g_refs[1][0:N_GATE, :])
    logit_a = jnp.where(fwd_rows, g_refs[0][N_GATE:2 * N_GATE, :], g_refs[1][N_GATE:2 * N_GATE, :])
    beta = _sigmoid(logit_b)
    la = -jnp.exp(alog_ref[...]) * _softplus(logit_a + dtb_ref[...])
    up, down = _seg_cumsums(la)
    g8 = jnp.where(fwd_rows, up, down)
    r8 = jnp.where(fwd_rows, down, up) - la
    eg8 = jnp.exp(g8)
    erb8 = jnp.exp(r8) * beta
    etot8 = jnp.exp(up + down - la)
    gcols = _rows_to_cols(g8, eye_sc)

    nck = BLK // GDN_CHUNK

    def l2n(x):
        return x * lax.rsqrt(jnp.sum(x * x, axis=-1, keepdims=True) + EPS)

    def chain(d, h):
        x_ref, hp_ref, hn_ref = x_refs[d], hp_refs[d], hn_refs[d]
        has_prev = blk_of[d] > 0
        has_next = blk_of[d] < nb - 1

        def conv(cols):
            cur = x_ref[:, cols]
            row = lax.broadcasted_iota(jnp.int32, cur.shape, 0)
            before = jnp.where(has_prev, hp_ref[SUB - 1:SUB, cols], 0.0)
            after = jnp.where(has_next, hn_ref[0:1, cols], 0.0)
            prev = jnp.where(row == 0, before, pltpu.roll(cur, 1, 0))
            nxt = jnp.where(row == BLK - 1, after, pltpu.roll(cur, BLK - 1, 0))
            cw = cw_ref[:, cols]
            return _silu(prev * cw[0:1, :] + cur * cw[1:2, :] + nxt * cw[2:3, :])

        c = d * HEADS + h
        q = l2n(conv(slice(h * DK, (h + 1) * DK))) * (DK ** -0.5)
        k = l2n(conv(slice(QK_W + h * DK, QK_W + (h + 1) * DK)))
        v16 = conv(slice(2 * QK_W + h * DV, 2 * QK_W + (h + 1) * DV)).astype(BF16)
        k16 = k.astype(BF16)
        kk = _mm_nt(k16, k16)
        qk = _mm_nt(q, k16)
        yield
        g_col = gcols[:, c:c + 1]
        g_row = g8[c:c + 1, :]
        m_eq = mask_sc[MASK_EQ_F if d == 0 else MASK_EQ_B]
        decay_beta = (jnp.exp(jnp.minimum(g_col - g_row, 0.0)) * m_eq) * beta[c:c + 1, :]
        lp = kk * decay_beta * mask_sc[MASK_OFFDIAG]
        sc16 = (qk * decay_beta).astype(BF16)
        qg16 = (q * jnp.exp(g_col)).astype(BF16)
        kt16 = (k.T * erb8[c:c + 1, :]).astype(BF16)

        x = -(lp * mask_sc[MASK_BASE])
        t = mask_sc[MASK_EYE] + x
        for _ in range(TRI_BASE.bit_length() - 2):
            x = _mm(x, x)
            yield
            t = t + _mm(t, x)
            yield
        size, idx = TRI_BASE, MASK_PAIR0
        while size < GDN_CHUNK:
            et = _mm(lp * mask_sc[idx], t)
            yield
            t = t - _mm(t, et)
            yield
            size, idx = 2 * size, idx + 1
        u = _mm(t, v16)
        w16 = _mm(t * eg8[c:c + 1, :], k16).astype(BF16)
        yield

        order = range(nck) if d == 0 else range(nck - 1, -1, -1)
        for cc in order:
            rs = slice(cc * GDN_CHUNK, (cc + 1) * GDN_CHUNK)
            s_prev = st_sc[d, h]
            wq = jnp.concatenate([w16[rs, :], qg16[rs, :]], axis=0)
            ws_qs = _mm(wq, s_prev)
            yield
            v_new = (u[rs, :] - ws_qs[0:GDN_CHUNK]).astype(BF16)
            o = ws_qs[GDN_CHUNK:] + _mm(sc16[rs, rs], v_new)
            decay_all = etot8[c:c + 1, cc * GDN_CHUNK:cc * GDN_CHUNK + 1]
            st_sc[d, h] = s_prev * decay_all + _mm(kt16[:, rs], v_new)
            o_refs[d][rs, h * DV:(h + 1) * DV] = o
            yield

    chains = [chain(d, h) for d in range(2) for h in range(HEADS)]
    for _ in zip(*chains):
        pass


    if emit_state:
        @pl.when(step == nb - 1)
        def _():
            for d in range(2):
                for h in range(HEADS):
                    st_ref[d, h] = st_sc[d, h]


def _gdn(proj, gates, conv_w, a_log, dt_bias, n_seq, seq_len, s0=None, emit_state=False):
    t = proj.shape[0]
    nb = seq_len // BLK
    has_s0 = s0 is not None
    kern = functools.partial(_gdn_kernel, nb=nb, has_s0=has_s0, emit_state=emit_state)
    colb = COL_DQKV // DQKV_W
    halo_per_blk = BLK // SUB
    n_halo = t // SUB

    def fblk(b, s):
        return b * nb + s

    def bblk(b, s):
        return b * nb + nb - 1 - s

    def spec_x(blk):
        return pl.BlockSpec((BLK, DQKV_W), lambda b, s: (blk(b, s), colb))

    def spec_prev(blk):
        return pl.BlockSpec((SUB, DQKV_W), lambda b, s: (jnp.maximum(blk(b, s) * halo_per_blk - 1, 0), colb))

    def spec_next(blk):
        return pl.BlockSpec((SUB, DQKV_W),
                            lambda b, s: (jnp.minimum((blk(b, s) + 1) * halo_per_blk, n_halo - 1), colb))

    def spec_g(blk):
        return pl.BlockSpec((2 * N_GATE, BLK), lambda b, s: (0, blk(b, s)))

    const2 = lambda b, s: (0, 0)
    in_specs = [spec_x(fblk), spec_x(bblk), spec_prev(fblk), spec_prev(bblk),
                spec_next(fblk), spec_next(bblk), spec_g(fblk), spec_g(bblk),
                pl.BlockSpec((CONV_WIDTH, DQKV_W), const2),
                pl.BlockSpec((N_GATE, 1), const2),
                pl.BlockSpec((N_GATE, 1), const2)]
    args = [proj, proj, proj, proj, proj, proj, gates, gates, conv_w, a_log, dt_bias]
    state_spec = pl.BlockSpec((None, None, 2, HEADS, DK, DV), lambda b, s: (b, 0, 0, 0, 0, 0))
    if has_s0:
        in_specs.append(state_spec)
        args.append(s0)
    out_specs = [pl.BlockSpec((BLK, V_W), lambda b, s: (fblk(b, s), 0)),
                 pl.BlockSpec((BLK, V_W), lambda b, s: (bblk(b, s), 0))]
    out_shape = [jax.ShapeDtypeStruct((t, V_W), F32), jax.ShapeDtypeStruct((t, V_W), F32)]
    if emit_state:
        out_specs.append(state_spec)
        out_shape.append(jax.ShapeDtypeStruct((n_seq, 1, 2, HEADS, DK, DV), F32))
    scratch = [pltpu.VMEM((2, HEADS, DK, DV), F32),
               pltpu.VMEM((N_MASK, BLK, BLK), F32),
               pltpu.VMEM((BLK, BLK), BF16)]
    return pl.pallas_call(
        kern,
        grid=(n_seq, nb),
        in_specs=in_specs,
        out_specs=out_specs,
        out_shape=out_shape,
        scratch_shapes=scratch,
        compiler_params=_cparams(("arbitrary", "arbitrary")),
        name="gated_delta",
    )(*args)


def _merge_kernel(x_ref, or_ref, of_ref, ob_ref, rg_ref, dz_ref, gr_ref, gd_ref, mod_ref, nw_ref,
                  dnw_ref, wr_ref, wd_ref, wo_ref, o_ref, *, tm, rows_per_cond):
    i = pl.program_id(0)
    ci = (i * tm) // rows_per_cond
    m = mod_ref[pl.ds(ci, 1), :]
    g1 = m[:, 2 * D_MODEL:3 * D_MODEL]
    y_r = _mm(_silu(rg_ref[...]) * or_ref[...], wr_ref[...])
    dnw = dnw_ref[...]
    heads = []
    for h in range(HEADS):
        cols = slice(h * DV, (h + 1) * DV)
        od = _rms(of_ref[:, cols] + ob_ref[:, cols], dnw)
        heads.append((od * _silu(dz_ref[:, cols])).astype(BF16))
    y_d = jnp.dot(jnp.concatenate(heads, axis=1), wd_ref[...], preferred_element_type=F32)
    merged = _sigmoid(gr_ref[...]) * y_r + _sigmoid(gd_ref[...]) * y_d
    mo = _mm(merged, wo_ref[...])
    o_ref[...] = x_ref[...] + g1 * _rms(mo, nw_ref[1:2, :])


def _merge(x, o_r, o_f, o_b, proj, mod, norm_w, dn_norm_w, w_ret_o, w_dn_o, w_out, rows_per_cond):
    t = x.shape[0]
    tm = 256
    kern = functools.partial(_merge_kernel, tm=tm, rows_per_cond=rows_per_cond)
    row = lambda i: (i, 0)
    const = lambda i: (0, 0)
    wspec = pl.BlockSpec((D_MODEL, D_MODEL), const)
    return pl.pallas_call(
        kern,
        grid=(t // tm,),
        in_specs=[pl.BlockSpec((tm, D_MODEL), row),
                  pl.BlockSpec((tm, V_W), row),
                  pl.BlockSpec((tm, V_W), row),
                  pl.BlockSpec((tm, V_W), row),
                  pl.BlockSpec((tm, V_W), lambda i: (i, COL_RG // V_W)),
                  pl.BlockSpec((tm, V_W), lambda i: (i, COL_DZ // V_W)),
                  pl.BlockSpec((tm, D_MODEL), lambda i: (i, COL_GR // D_MODEL)),
                  pl.BlockSpec((tm, D_MODEL), lambda i: (i, COL_GD // D_MODEL)),
                  pl.BlockSpec(mod.shape, const),
                  pl.BlockSpec(norm_w.shape, const),
                  pl.BlockSpec(dn_norm_w.shape, const),
                  wspec, wspec, wspec],
        out_specs=pl.BlockSpec((tm, D_MODEL), row),
        out_shape=jax.ShapeDtypeStruct((t, D_MODEL), F32),
        compiler_params=_cparams(("arbitrary",)),
        name="merge_out",
    )(x, o_r, o_f, o_b, proj, proj, proj, proj, mod, norm_w, dn_norm_w, w_ret_o, w_dn_o, w_out)


def _ffn_kernel(x_ref, mod_ref, nw_ref, wi_ref, wo_ref, o_ref, *, tm, rows_per_cond):
    i = pl.program_id(0)
    ci = (i * tm) // rows_per_cond
    m = mod_ref[pl.ds(ci, 1), :]
    sh2 = m[:, 3 * D_MODEL:4 * D_MODEL]
    sc2 = m[:, 4 * D_MODEL:5 * D_MODEL]
    g2 = m[:, 5 * D_MODEL:6 * D_MODEL]
    x = x_ref[...]
    hb = (_rms(x, nw_ref[2:3, :]) * (1.0 + sc2) + sh2).astype(BF16)
    f = None
    for c in range(D_FF // FF_CHUNK):
        lo = c * FF_CHUNK
        gate = jnp.dot(hb, wi_ref[:, lo:lo + FF_CHUNK], preferred_element_type=F32)
        up = jnp.dot(hb, wi_ref[:, D_FF + lo:D_FF + lo + FF_CHUNK], preferred_element_type=F32)
        part = _mm(_silu(gate) * up, wo_ref[lo:lo + FF_CHUNK, :])
        f = part if f is None else f + part
    o_ref[...] = x + g2 * _rms(f, nw_ref[3:4, :])


def _ffn(x, mod, norm_w, w_in, w_out, rows_per_cond):
    t = x.shape[0]
    tm = 256
    kern = functools.partial(_ffn_kernel, tm=tm, rows_per_cond=rows_per_cond)
    row = lambda i: (i, 0)
    const = lambda i: (0, 0)
    return pl.pallas_call(
        kern,
        grid=(t // tm,),
        in_specs=[pl.BlockSpec((tm, D_MODEL), row),
                  pl.BlockSpec(mod.shape, const),
                  pl.BlockSpec(norm_w.shape, const),
                  pl.BlockSpec((D_MODEL, 2 * D_FF), const),
                  pl.BlockSpec((D_FF, D_MODEL), const)],
        out_specs=pl.BlockSpec((tm, D_MODEL), row),
        out_shape=jax.ShapeDtypeStruct((t, D_MODEL), F32),
        compiler_params=_cparams(("arbitrary",)),
        name="swiglu",
    )(x, mod, norm_w, w_in, w_out)


def _rope_tables(seq_len):
    rows = seq_len // GRID_W
    row_idx = jnp.repeat(jnp.arange(rows, dtype=F32), GRID_W)
    col_idx = (jnp.arange(seq_len) % GRID_W).astype(F32)
    n_freq = DK // 4
    freqs = ROPE_BASE ** (-jnp.arange(n_freq, dtype=F32) / n_freq)
    ang = jnp.concatenate([row_idx[:, None] * freqs, col_idx[:, None] * freqs], axis=-1)
    cos = jnp.repeat(jnp.cos(ang), 2, axis=-1)
    sin = jnp.repeat(jnp.sin(ang), 2, axis=-1)
    sign = jnp.tile(jnp.array([-1.0, 1.0], F32), DK // 2)
    return cos, sin * sign


def _one_path(x, mod, seq_len, n_seq, wts, rope_tabs, s_ret0, s_dn0, emit_state):
    rows_per_cond = seq_len if mod.shape[0] > 1 else x.shape[0]
    proj, gates = _inproj(x, mod, wts["norm_w"], wts["w_main"], wts["w_gate_t"], rows_per_cond)
    ret = _retention(proj, wts["logit_rep"], wts["ret_gn_w"], n_seq, seq_len,
                     rope_tabs=rope_tabs, s0=s_ret0, emit_state=emit_state)
    gdn = _gdn(proj, gates, wts["conv_w"], wts["a_log"], wts["dt_bias"], n_seq, seq_len,
               s0=s_dn0, emit_state=emit_state)
    x1 = _merge(x, ret[0], gdn[0], gdn[1], proj, mod, wts["norm_w"], wts["dn_norm_w"],
                wts["w_ret_o"], wts["w_dn_o"], wts["w_out"], rows_per_cond)
    y = _ffn(x1, mod, wts["norm_w"], wts["w_ffn_in"], wts["w_ffn_out"], rows_per_cond)
    if emit_state:
        return y, ret[1], gdn[2]
    return y, None, None


def kernel(x_prompt, x_sample, c, state_ret, state_dn, c_ctx, w_mod, b_mod, norm_w, w_in, conv_w,
           ret_decay_logit, ret_gn_w, dn_a_log, dn_dt_bias, dn_norm_w, w_ret_o, w_dn_o, w_out,
           w_ffn_in, w_ffn_out):
    n_ctx, l_ctx, _ = x_prompt.shape
    n_lat, l_lat, _ = x_sample.shape
    assert w_mod.shape[0] == 1, "single-layer kernel"

    cond8 = jnp.zeros((8, D_MODEL), F32).at[0].set(c_ctx).at[1:1 + n_lat].set(c)
    mod = _modulation(cond8, w_mod[0], b_mod)
    mod_ctx, mod_lat = mod[0:1], mod[1:1 + n_lat]

    wi = w_in[0]
    c_dq = 2 * QK_W + 2 * V_W
    c_dz = c_dq + DQKV_W
    c_gate = c_dz + V_W
    c_gr = c_gate + 2 * N_GATE
    w_main = jnp.concatenate([wi[:, :c_dq], wi[:, c_dz:c_gate], wi[:, c_dq:c_dz], wi[:, c_gr:]],
                             axis=1).astype(BF16)
    wts = {
        "norm_w": norm_w[0],
        "w_main": w_main,
        "w_gate_t": wi[:, c_gate:c_gr].T.astype(BF16),
        "logit_rep": jnp.broadcast_to(ret_decay_logit[0].T[:, :, None], (HEADS, 2, DK)),
        "ret_gn_w": ret_gn_w,
        "conv_w": conv_w[0],
        "a_log": dn_a_log.reshape(N_GATE, 1),
        "dt_bias": dn_dt_bias.reshape(N_GATE, 1),
        "dn_norm_w": dn_norm_w,
        "w_ret_o": w_ret_o[0].astype(BF16),
        "w_dn_o": w_dn_o[0].astype(BF16),
        "w_out": w_out[0].astype(BF16),
        "w_ffn_in": w_ffn_in[0].astype(BF16),
        "w_ffn_out": w_ffn_out[0].astype(BF16),
    }

    y_p, s_ret, s_dn = _one_path(x_prompt.reshape(n_ctx * l_ctx, D_MODEL), mod_ctx, l_ctx, n_ctx,
                                 wts, None, None, None, True)
    y_s, _, _ = _one_path(x_sample.reshape(n_lat * l_lat, D_MODEL), mod_lat, l_lat, n_lat,
                          wts, _rope_tables(l_lat), state_ret, state_dn, False)
    return (y_p.reshape(x_prompt.shape), y_s.reshape(x_sample.shape), s_ret, s_dn)
```

```python
import functools

import jax
import jax.numpy as jnp
from jax import lax
from jax.experimental import pallas as pl
from jax.experimental.pallas import tpu as pltpu

F32 = jnp.float32
BF16 = jnp.bfloat16

D_MODEL = 1024
HEADS = 4
DK = 128
DV = 256
QK_W = HEADS * DK
V_W = HEADS * DV
GDN_CHUNK = 64
TRI_BASE = 8
BLK = 256
SUB = 8
HALO = 16
CONV_WIDTH = 3
GRID_W = 64
ROPE_BASE = 10000.0
EPS = 1e-6
D_FF = 2816
FF_CHUNK = 1408
N_GATE = 2 * HEADS

COL_RQ, COL_RK, COL_RV, COL_RG = 0, 512, 1024, 2048
COL_DZ = 3072
COL_DQKV = 4096
COL_GR, COL_GD = 6144, 7168
PROJ_W = 8192
DQKV_W = 2 * QK_W + V_W

VMEM_LIMIT = 56 * 1024 * 1024


def _cparams(sem):
    return pltpu.CompilerParams(dimension_semantics=sem, vmem_limit_bytes=VMEM_LIMIT)


def _mm(a, b):
    return jnp.dot(a.astype(BF16), b.astype(BF16), preferred_element_type=F32)


def _mm_nt(a, b):
    return lax.dot_general(a.astype(BF16), b.astype(BF16), (((1,), (1,)), ((), ())),
                           preferred_element_type=F32)


def _mm_tn(a, b):
    return lax.dot_general(a.astype(BF16), b.astype(BF16), (((0,), (0,)), ((), ())),
                           preferred_element_type=F32)


def _sigmoid(x):
    return 1.0 / (1.0 + jnp.exp(-x))


def _silu(x):
    return x * _sigmoid(x)


def _softplus(x):
    return jnp.maximum(x, 0.0) + jnp.log(1.0 + jnp.exp(-jnp.abs(x)))


def _log_sigmoid(x):
    return -_softplus(-x)


def _rms(x, w):
    return x * lax.rsqrt(jnp.mean(x * x, axis=-1, keepdims=True) + EPS) * w


def _mod_kernel(c_ref, w_ref, b_ref, o_ref):
    o_ref[...] = _mm(_silu(c_ref[...]), w_ref[...]) + b_ref[...]


def _modulation(cond8, w_mod, b_mod):
    n = w_mod.shape[1]
    tn = 1536
    return pl.pallas_call(
        _mod_kernel,
        grid=(n // tn,),
        in_specs=[pl.BlockSpec((8, D_MODEL), lambda j: (0, 0)),
                  pl.BlockSpec((D_MODEL, tn), lambda j: (0, j)),
                  pl.BlockSpec((1, tn), lambda j: (0, j))],
        out_specs=pl.BlockSpec((8, tn), lambda j: (0, j)),
        out_shape=jax.ShapeDtypeStruct((8, n), F32),
        compiler_params=_cparams(("arbitrary",)),
        name="modulation",
    )(cond8, w_mod, b_mod)


def _inproj_kernel(x_ref, mod_ref, nw_ref, w_ref, wg_ref, o_ref, g_ref, h_sc, *, tm, rows_per_cond):
    i = pl.program_id(0)
    j = pl.program_id(1)

    @pl.when(j == 0)
    def _():
        ci = (i * tm) // rows_per_cond
        m = mod_ref[pl.ds(ci, 1), :]
        sh1 = m[:, 0:D_MODEL]
        sc1 = m[:, D_MODEL:2 * D_MODEL]
        h = _rms(x_ref[...], nw_ref[0:1, :]) * (1.0 + sc1) + sh1
        hb = h.astype(BF16)
        h_sc[...] = hb
        g_ref[...] = lax.dot_general(wg_ref[...], hb, (((1,), (1,)), ((), ())),
                                     preferred_element_type=F32)

    o_ref[...] = jnp.dot(h_sc[...], w_ref[...], preferred_element_type=F32).astype(o_ref.dtype)


def _inproj(x, mod, norm_w, w_main, w_gate_t, rows_per_cond):
    t = x.shape[0]
    tm, tn = 1024, 2048
    kern = functools.partial(_inproj_kernel, tm=tm, rows_per_cond=rows_per_cond)
    return pl.pallas_call(
        kern,
        grid=(t // tm, PROJ_W // tn),
        in_specs=[pl.BlockSpec((tm, D_MODEL), lambda i, j: (i, 0)),
                  pl.BlockSpec(mod.shape, lambda i, j: (0, 0)),
                  pl.BlockSpec(norm_w.shape, lambda i, j: (0, 0)),
                  pl.BlockSpec((D_MODEL, tn), lambda i, j: (0, j)),
                  pl.BlockSpec((2 * N_GATE, D_MODEL), lambda i, j: (0, 0))],
        out_specs=[pl.BlockSpec((tm, tn), lambda i, j: (i, j)),
                   pl.BlockSpec((2 * N_GATE, tm), lambda i, j: (0, i))],
        out_shape=[jax.ShapeDtypeStruct((t, PROJ_W), BF16),
                   jax.ShapeDtypeStruct((2 * N_GATE, t), F32)],
        scratch_shapes=[pltpu.VMEM((tm, D_MODEL), BF16)],
        compiler_params=_cparams(("arbitrary", "arbitrary")),
        name="inproj",
    )(x, mod, norm_w, w_main, w_gate_t)


def _pair_swap(x):
    lane = lax.broadcasted_iota(jnp.int32, x.shape, 1)
    nxt = pltpu.roll(x, x.shape[1] - 1, 1)
    prv = pltpu.roll(x, 1, 1)
    return jnp.where((lane & 1) == 0, nxt, prv)


def _ret_kernel(*refs, seq_len, rope, has_s0, emit_state):
    it = iter(refs)
    q_ref, k_ref, v_ref = next(it), next(it), next(it)
    cos_ref = next(it) if rope else None
    sin_ref = next(it) if rope else None
    logit_ref, gnw_ref = next(it), next(it)
    s0_ref = next(it) if has_s0 else None
    o_ref = next(it)
    st_ref = next(it) if emit_state else None
    q_sc, k_sc, sf_sc, sb_sc = next(it), next(it), next(it), next(it)

    nb = seq_len // BLK
    q = q_ref[...].astype(F32) * (DK ** -0.5)
    k = k_ref[...].astype(F32)
    if rope:
        cos = cos_ref[...]
        sin = sin_ref[...]
        q = q * cos + _pair_swap(q) * sin
        k = k * cos + _pair_swap(k) * sin
    q_sc[...] = q
    k_sc[...] = k

    lg = _log_sigmoid(logit_ref[...])
    lgf, lgb = lg[0:1, :], lg[1:2, :]
    lgf1, lgb1 = lgf[:, 0:1], lgb[:, 0:1]
    pos = lax.broadcasted_iota(jnp.int32, (BLK, DK), 0).astype(F32)
    kdec_f = jnp.exp(lgf * (BLK - 1.0 - pos))
    kdec_b = jnp.exp(lgb * pos)
    qdec_f = jnp.exp(lgf * (pos + 1.0))
    qdec_b = jnp.exp(lgb * (BLK - pos))
    cdec_f = jnp.exp(lgf1 * float(BLK))
    cdec_b = jnp.exp(lgb1 * float(BLK))
    ii = lax.broadcasted_iota(jnp.int32, (BLK, BLK), 0)
    jj = lax.broadcasted_iota(jnp.int32, (BLK, BLK), 1)
    diff = (ii - jj).astype(F32)
    dmat = (jnp.where(diff >= 0, jnp.exp(lgf1 * jnp.maximum(diff, 0.0)), 0.0)
            + jnp.where(diff <= 0, jnp.exp(lgb1 * jnp.maximum(-diff, 0.0)), 0.0))

    def contrib(c, dec):
        kc = k_sc[c * BLK:(c + 1) * BLK, :]
        vc = v_ref[c * BLK:(c + 1) * BLK, :]
        return _mm_tn(kc * dec, vc)

    s = s0_ref[0] if has_s0 else None
    for c in range(nb):
        if s is not None:
            sf_sc[c] = s
        if c < nb - 1 or emit_state:
            a = contrib(c, kdec_f)
            s = a if s is None else s * cdec_f + a
    if emit_state:
        st_ref[0] = s
    s = s0_ref[1] if has_s0 else None
    for c in range(nb - 1, -1, -1):
        if s is not None:
            sb_sc[c] = s
        if c > 0 or emit_state:
            a = contrib(c, kdec_b)
            s = a if s is None else s * cdec_b + a
    if emit_state:
        st_ref[1] = s

    gnw = gnw_ref[...]
    for c in range(nb):
        rows = slice(c * BLK, (c + 1) * BLK)
        qc = q_sc[rows, :]
        kc = k_sc[rows, :]
        vc = v_ref[rows, :]
        o = _mm(_mm_nt(qc, kc) * dmat, vc)
        if has_s0 or c > 0:
            o = o + _mm(qc * qdec_f, sf_sc[c])
        if has_s0 or c < nb - 1:
            o = o + _mm(qc * qdec_b, sb_sc[c])
        mu = jnp.mean(o, axis=-1, keepdims=True)
        oc = o - mu
        var = jnp.mean(oc * oc, axis=-1, keepdims=True)
        o_ref[rows, :] = (oc * lax.rsqrt(var + EPS) * gnw).astype(o_ref.dtype)


def _retention(proj, logit_rep, gn_w, n_seq, seq_len, rope_tabs=None, s0=None, emit_state=False):
    t = proj.shape[0]
    rope = rope_tabs is not None
    has_s0 = s0 is not None
    kern = functools.partial(_ret_kernel, seq_len=seq_len, rope=rope, has_s0=has_s0,
                             emit_state=emit_state)
    in_specs = [pl.BlockSpec((seq_len, DK), lambda b, h: (b, COL_RQ // DK + h)),
                pl.BlockSpec((seq_len, DK), lambda b, h: (b, COL_RK // DK + h)),
                pl.BlockSpec((seq_len, DV), lambda b, h: (b, COL_RV // DV + h))]
    args = [proj, proj, proj]
    if rope:
        in_specs += [pl.BlockSpec((seq_len, DK), lambda b, h: (0, 0))] * 2
        args += list(rope_tabs)
    in_specs += [pl.BlockSpec((None, 2, DK), lambda b, h: (h, 0, 0)),
                 pl.BlockSpec((1, DV), lambda b, h: (0, h))]
    args += [logit_rep, gn_w]
    if has_s0:
        in_specs.append(pl.BlockSpec((None, None, 2, None, DK, DV), lambda b, h: (b, 0, 0, h, 0, 0)))
        args.append(s0)
    out_specs = [pl.BlockSpec((seq_len, DV), lambda b, h: (b, h))]
    out_shape = [jax.ShapeDtypeStruct((t, V_W), BF16)]
    if emit_state:
        out_specs.append(pl.BlockSpec((None, None, 2, None, DK, DV), lambda b, h: (b, 0, 0, h, 0, 0)))
        out_shape.append(jax.ShapeDtypeStruct((n_seq, 1, 2, HEADS, DK, DV), F32))
    nb = seq_len // BLK
    res = pl.pallas_call(
        kern,
        grid=(n_seq, HEADS),
        in_specs=in_specs,
        out_specs=out_specs,
        out_shape=out_shape,
        scratch_shapes=[pltpu.VMEM((seq_len, DK), F32), pltpu.VMEM((seq_len, DK), F32),
                        pltpu.VMEM((nb, DK, DV), F32), pltpu.VMEM((nb, DK, DV), F32)],
        compiler_params=_cparams(("arbitrary", "arbitrary")),
        name="retention",
    )(*args)
    return res


N_MASK = 8
(MASK_EQ_F, MASK_EQ_B, MASK_OFFDIAG, MASK_EYE, MASK_BASE, MASK_PAIR0) = 0, 1, 2, 3, 4, 5


def _build_masks(mask_sc, eye_sc):
    ii = lax.broadcasted_iota(jnp.int32, (BLK, BLK), 0)
    jj = lax.broadcasted_iota(jnp.int32, (BLK, BLK), 1)

    def blk(v, size):
        return lax.shift_right_logical(v, size.bit_length() - 1)

    same = blk(ii, GDN_CHUNK) == blk(jj, GDN_CHUNK)
    mask_sc[MASK_EQ_F] = (same & (jj <= ii)).astype(F32)
    mask_sc[MASK_EQ_B] = (same & (jj >= ii)).astype(F32)
    mask_sc[MASK_OFFDIAG] = (ii != jj).astype(F32)
    mask_sc[MASK_EYE] = (ii == jj).astype(F32)
    mask_sc[MASK_BASE] = (blk(ii, TRI_BASE) == blk(jj, TRI_BASE)).astype(F32)
    size, idx = TRI_BASE, MASK_PAIR0
    while size < GDN_CHUNK:
        pair = (blk(ii, 2 * size) == blk(jj, 2 * size)) & (blk(ii, size) != blk(jj, size))
        mask_sc[idx] = pair.astype(F32)
        size, idx = 2 * size, idx + 1
    eye_sc[...] = (ii == jj).astype(BF16)


def _seg_cumsums(x):
    n = x.shape[1]
    pos = lax.broadcasted_iota(jnp.int32, x.shape, 1) & (GDN_CHUNK - 1)
    up, down = x, x
    s = 1
    while s < GDN_CHUNK:
        up = up + jnp.where(pos >= s, pltpu.roll(up, s, 1), 0.0)
        down = down + jnp.where(pos < GDN_CHUNK - s, pltpu.roll(down, n - s, 1), 0.0)
        s *= 2
    return up, down


def _rows_to_cols(rows, eye_sc):
    p1 = rows.astype(BF16).astype(F32)
    r1 = rows - p1
    p2 = r1.astype(BF16).astype(F32)
    p3 = r1 - p2
    pieces = jnp.concatenate([p1, p2, p3], axis=0).astype(BF16)
    c3 = lax.dot_general(eye_sc[...], pieces, (((1,), (1,)), ((), ())), preferred_element_type=F32)
    return c3[:, 0:SUB] + c3[:, SUB:2 * SUB] + c3[:, 2 * SUB:3 * SUB]


def _gdn_kernel(*refs, nb, has_s0, emit_state):
    it = iter(refs)
    x_refs = (next(it), next(it))
    hp_refs = (next(it), next(it))
    hn_refs = (next(it), next(it))
    g_refs = (next(it), next(it))
    cw_ref, alog_ref, dtb_ref = next(it), next(it), next(it)
    s0_ref = next(it) if has_s0 else None
    o_refs = (next(it), next(it))
    st_ref = next(it) if emit_state else None
    st_sc, mask_sc, eye_sc = next(it), next(it), next(it)

    b = pl.program_id(0)
    step = pl.program_id(1)

    @pl.when((b == 0) & (step == 0))
    def _():
        _build_masks(mask_sc, eye_sc)

    @pl.when(step == 0)
    def _():
        for d in range(2):
            for h in range(HEADS):
                st_sc[d, h] = s0_ref[d, h] if has_s0 else jnp.zeros((DK, DV), F32)

    blk_of = (step, nb - 1 - step)

    row8 = lax.broadcasted_iota(jnp.int32, (N_GATE, BLK), 0)
    fwd_rows = row8 < HEADS
    logit_b = jnp.where(fwd_rows, g_refs[0][0:N_GATE, :], g_refs[1][0:N_GATE, :])
    logit_a = jnp.where(fwd_rows, g_refs[0][N_GATE:2 * N_GATE, :], g_refs[1][N_GATE:2 * N_GATE, :])
    beta = _sigmoid(logit_b)
    la = -jnp.exp(alog_ref[...]) * _softplus(logit_a + dtb_ref[...])
    up, down = _seg_cumsums(la)
    g8 = jnp.where(fwd_rows, up, down)
    r8 = jnp.where(fwd_rows, down, up) - la
    eg8 = jnp.exp(g8)
    erb8 = jnp.exp(r8) * beta
    etot8 = jnp.exp(up + down - la)
    gcols = _rows_to_cols(g8, eye_sc)

    nck = BLK // GDN_CHUNK

    def l2n(x):
        return x * lax.rsqrt(jnp.sum(x * x, axis=-1, keepdims=True) + EPS)

    def chain(d, h):
        x_ref, hp_ref, hn_ref = x_refs[d], hp_refs[d], hn_refs[d]
        has_prev = blk_of[d] > 0
        has_next = blk_of[d] < nb - 1

        def conv(cols):
            cur = x_ref[:, cols].astype(F32)
            row = lax.broadcasted_iota(jnp.int32, cur.shape, 0)
            before = jnp.where(has_prev, hp_ref[HALO - 1:HALO, cols].astype(F32), 0.0)
            after = jnp.where(has_next, hn_ref[0:1, cols].astype(F32), 0.0)
            prev = jnp.where(row == 0, before, pltpu.roll(cur, 1, 0))
            nxt = jnp.where(row == BLK - 1, after, pltpu.roll(cur, BLK - 1, 0))
            cw = cw_ref[:, cols]
            return _silu(prev * cw[0:1, :] + cur * cw[1:2, :] + nxt * cw[2:3, :])

        c = d * HEADS + h
        q = l2n(conv(slice(h * DK, (h + 1) * DK))) * (DK ** -0.5)
        k = l2n(conv(slice(QK_W + h * DK, QK_W + (h + 1) * DK)))
        v16 = conv(slice(2 * QK_W + h * DV, 2 * QK_W + (h + 1) * DV)).astype(BF16)
        k16 = k.astype(BF16)
        kk = _mm_nt(k16, k16)
        qk = _mm_nt(q, k16)
        yield
        g_col = gcols[:, c:c + 1]
        g_row = g8[c:c + 1, :]
        m_eq = mask_sc[MASK_EQ_F if d == 0 else MASK_EQ_B]
        decay_beta = (jnp.exp(jnp.minimum(g_col - g_row, 0.0)) * m_eq) * beta[c:c + 1, :]
        lp = kk * decay_beta * mask_sc[MASK_OFFDIAG]
        sc16 = (qk * decay_beta).astype(BF16)
        qg16 = (q * jnp.exp(g_col)).astype(BF16)
        kt16 = (k.T * erb8[c:c + 1, :]).astype(BF16)

        x = -(lp * mask_sc[MASK_BASE])
        t = mask_sc[MASK_EYE] + x
        for _ in range(TRI_BASE.bit_length() - 2):
            x = _mm(x, x)
            yield
            t = t + _mm(t, x)
            yield
        size, idx = TRI_BASE, MASK_PAIR0
        while size < GDN_CHUNK:
            et = _mm(lp * mask_sc[idx], t)
            yield
            t = t - _mm(t, et)
            yield
            size, idx = 2 * size, idx + 1
        u = _mm(t, v16)
        w16 = _mm(t * eg8[c:c + 1, :], k16).astype(BF16)
        yield

        order = range(nck) if d == 0 else range(nck - 1, -1, -1)
        for cc in order:
            rs = slice(cc * GDN_CHUNK, (cc + 1) * GDN_CHUNK)
            s_prev = st_sc[d, h]
            wq = jnp.concatenate([w16[rs, :], qg16[rs, :]], axis=0)
            ws_qs = _mm(wq, s_prev)
            yield
            v_new = (u[rs, :] - ws_qs[0:GDN_CHUNK]).astype(BF16)
            o = ws_qs[GDN_CHUNK:] + _mm(sc16[rs, rs], v_new)
            decay_all = etot8[c:c + 1, cc * GDN_CHUNK:cc * GDN_CHUNK + 1]
            st_sc[d, h] = s_prev * decay_all + _mm(kt16[:, rs], v_new)
            o_refs[d][rs, h * DV:(h + 1) * DV] = o.astype(o_refs[d].dtype)
            yield

    chains = [chain(d, h) for d in range(2) for h in range(HEADS)]
    for _ in zip(*chains):
        pass


    if emit_state:
        @pl.when(step == nb - 1)
        def _():
            for d in range(2):
                for h in range(HEADS):
                    st_ref[d, h] = st_sc[d, h]


def _gdn(proj, gates, conv_w, a_log, dt_bias, n_seq, seq_len, s0=None, emit_state=False):
    t = proj.shape[0]
    nb = seq_len // BLK
    has_s0 = s0 is not None
    kern = functools.partial(_gdn_kernel, nb=nb, has_s0=has_s0, emit_state=emit_state)
    colb = COL_DQKV // DQKV_W
    halo_per_blk = BLK // HALO
    n_halo = t // HALO

    def fblk(b, s):
        return b * nb + s

    def bblk(b, s):
        return b * nb + nb - 1 - s

    def spec_x(blk):
        return pl.BlockSpec((BLK, DQKV_W), lambda b, s: (blk(b, s), colb))

    def spec_prev(blk):
        return pl.BlockSpec((HALO, DQKV_W), lambda b, s: (jnp.maximum(blk(b, s) * halo_per_blk - 1, 0), colb))

    def spec_next(blk):
        return pl.BlockSpec((HALO, DQKV_W),
                            lambda b, s: (jnp.minimum((blk(b, s) + 1) * halo_per_blk, n_halo - 1), colb))

    def spec_g(blk):
        return pl.BlockSpec((2 * N_GATE, BLK), lambda b, s: (0, blk(b, s)))

    const2 = lambda b, s: (0, 0)
    in_specs = [spec_x(fblk), spec_x(bblk), spec_prev(fblk), spec_prev(bblk),
                spec_next(fblk), spec_next(bblk), spec_g(fblk), spec_g(bblk),
                pl.BlockSpec((CONV_WIDTH, DQKV_W), const2),
                pl.BlockSpec((N_GATE, 1), const2),
                pl.BlockSpec((N_GATE, 1), const2)]
    args = [proj, proj, proj, proj, proj, proj, gates, gates, conv_w, a_log, dt_bias]
    state_spec = pl.BlockSpec((None, None, 2, HEADS, DK, DV), lambda b, s: (b, 0, 0, 0, 0, 0))
    if has_s0:
        in_specs.append(state_spec)
        args.append(s0)
    out_specs = [pl.BlockSpec((BLK, V_W), lambda b, s: (fblk(b, s), 0)),
                 pl.BlockSpec((BLK, V_W), lambda b, s: (bblk(b, s), 0))]
    out_shape = [jax.ShapeDtypeStruct((t, V_W), BF16), jax.ShapeDtypeStruct((t, V_W), BF16)]
    if emit_state:
        out_specs.append(state_spec)
        out_shape.append(jax.ShapeDtypeStruct((n_seq, 1, 2, HEADS, DK, DV), F32))
    scratch = [pltpu.VMEM((2, HEADS, DK, DV), F32),
               pltpu.VMEM((N_MASK, BLK, BLK), F32),
               pltpu.VMEM((BLK, BLK), BF16)]
    return pl.pallas_call(
        kern,
        grid=(n_seq, nb),
        in_specs=in_specs,
        out_specs=out_specs,
        out_shape=out_shape,
        scratch_shapes=scratch,
        compiler_params=_cparams(("arbitrary", "arbitrary")),
        name="gated_delta",
    )(*args)


def _merge_kernel(x_ref, or_ref, of_ref, ob_ref, rg_ref, dz_ref, gr_ref, gd_ref, mod_ref, nw_ref,
                  dnw_ref, wr_ref, wd_ref, wo_ref, o_ref, *, tm, rows_per_cond):
    i = pl.program_id(0)
    ci = (i * tm) // rows_per_cond
    m = mod_ref[pl.ds(ci, 1), :]
    g1 = m[:, 2 * D_MODEL:3 * D_MODEL]
    y_r = _mm(_silu(rg_ref[...].astype(F32)) * or_ref[...].astype(F32), wr_ref[...])
    dnw = dnw_ref[...]
    heads = []
    for h in range(HEADS):
        cols = slice(h * DV, (h + 1) * DV)
        od = _rms(of_ref[:, cols].astype(F32) + ob_ref[:, cols].astype(F32), dnw)
        heads.append((od * _silu(dz_ref[:, cols].astype(F32))).astype(BF16))
    y_d = jnp.dot(jnp.concatenate(heads, axis=1), wd_ref[...], preferred_element_type=F32)
    merged = _sigmoid(gr_ref[...].astype(F32)) * y_r + _sigmoid(gd_ref[...].astype(F32)) * y_d
    mo = _mm(merged, wo_ref[...])
    o_ref[...] = x_ref[...] + g1 * _rms(mo, nw_ref[1:2, :])


def _merge(x, o_r, o_f, o_b, proj, mod, norm_w, dn_norm_w, w_ret_o, w_dn_o, w_out, rows_per_cond):
    t = x.shape[0]
    tm = 512
    kern = functools.partial(_merge_kernel, tm=tm, rows_per_cond=rows_per_cond)
    row = lambda i: (i, 0)
    const = lambda i: (0, 0)
    wspec = pl.BlockSpec((D_MODEL, D_MODEL), const)
    return pl.pallas_call(
        kern,
        grid=(t // tm,),
        in_specs=[pl.BlockSpec((tm, D_MODEL), row),
                  pl.BlockSpec((tm, V_W), row),
                  pl.BlockSpec((tm, V_W), row),
                  pl.BlockSpec((tm, V_W), row),
                  pl.BlockSpec((tm, V_W), lambda i: (i, COL_RG // V_W)),
                  pl.BlockSpec((tm, V_W), lambda i: (i, COL_DZ // V_W)),
                  pl.BlockSpec((tm, D_MODEL), lambda i: (i, COL_GR // D_MODEL)),
                  pl.BlockSpec((tm, D_MODEL), lambda i: (i, COL_GD // D_MODEL)),
                  pl.BlockSpec(mod.shape, const),
                  pl.BlockSpec(norm_w.shape, const),
                  pl.BlockSpec(dn_norm_w.shape, const),
                  wspec, wspec, wspec],
        out_specs=pl.BlockSpec((tm, D_MODEL), row),
        out_shape=jax.ShapeDtypeStruct((t, D_MODEL), F32),
        compiler_params=_cparams(("arbitrary",)),
        name="merge_out",
    )(x, o_r, o_f, o_b, proj, proj, proj, proj, mod, norm_w, dn_norm_w, w_ret_o, w_dn_o, w_out)


def _ffn_kernel(x_ref, mod_ref, nw_ref, wi_ref, wo_ref, o_ref, *, tm, rows_per_cond):
    i = pl.program_id(0)
    ci = (i * tm) // rows_per_cond
    m = mod_ref[pl.ds(ci, 1), :]
    sh2 = m[:, 3 * D_MODEL:4 * D_MODEL]
    sc2 = m[:, 4 * D_MODEL:5 * D_MODEL]
    g2 = m[:, 5 * D_MODEL:6 * D_MODEL]
    x = x_ref[...]
    hb = (_rms(x, nw_ref[2:3, :]) * (1.0 + sc2) + sh2).astype(BF16)
    f = None
    for c in range(D_FF // FF_CHUNK):
        lo = c * FF_CHUNK
        gate = jnp.dot(hb, wi_ref[:, lo:lo + FF_CHUNK], preferred_element_type=F32)
        up = jnp.dot(hb, wi_ref[:, D_FF + lo:D_FF + lo + FF_CHUNK], preferred_element_type=F32)
        part = _mm(_silu(gate) * up, wo_ref[lo:lo + FF_CHUNK, :])
        f = part if f is None else f + part
    o_ref[...] = x + g2 * _rms(f, nw_ref[3:4, :])


def _ffn(x, mod, norm_w, w_in, w_out, rows_per_cond):
    t = x.shape[0]
    tm = 512
    kern = functools.partial(_ffn_kernel, tm=tm, rows_per_cond=rows_per_cond)
    row = lambda i: (i, 0)
    const = lambda i: (0, 0)
    resident = pl.Buffered(1)
    return pl.pallas_call(
        kern,
        grid=(t // tm,),
        in_specs=[pl.BlockSpec((tm, D_MODEL), row),
                  pl.BlockSpec(mod.shape, const),
                  pl.BlockSpec(norm_w.shape, const),
                  pl.BlockSpec((D_MODEL, 2 * D_FF), const, pipeline_mode=resident),
                  pl.BlockSpec((D_FF, D_MODEL), const, pipeline_mode=resident)],
        out_specs=pl.BlockSpec((tm, D_MODEL), row),
        out_shape=jax.ShapeDtypeStruct((t, D_MODEL), F32),
        compiler_params=_cparams(("arbitrary",)),
        name="swiglu",
    )(x, mod, norm_w, w_in, w_out)


def _rope_tables(seq_len):
    rows = seq_len // GRID_W
    row_idx = jnp.repeat(jnp.arange(rows, dtype=F32), GRID_W)
    col_idx = (jnp.arange(seq_len) % GRID_W).astype(F32)
    n_freq = DK // 4
    freqs = ROPE_BASE ** (-jnp.arange(n_freq, dtype=F32) / n_freq)
    ang = jnp.concatenate([row_idx[:, None] * freqs, col_idx[:, None] * freqs], axis=-1)
    cos = jnp.repeat(jnp.cos(ang), 2, axis=-1)
    sin = jnp.repeat(jnp.sin(ang), 2, axis=-1)
    sign = jnp.tile(jnp.array([-1.0, 1.0], F32), DK // 2)
    return cos, sin * sign


def _one_path(x, mod, seq_len, n_seq, wts, rope_tabs, s_ret0, s_dn0, emit_state):
    rows_per_cond = seq_len if mod.shape[0] > 1 else x.shape[0]
    proj, gates = _inproj(x, mod, wts["norm_w"], wts["w_main"], wts["w_gate_t"], rows_per_cond)
    ret = _retention(proj, wts["logit_rep"], wts["ret_gn_w"], n_seq, seq_len,
                     rope_tabs=rope_tabs, s0=s_ret0, emit_state=emit_state)
    gdn = _gdn(proj, gates, wts["conv_w"], wts["a_log"], wts["dt_bias"], n_seq, seq_len,
               s0=s_dn0, emit_state=emit_state)
    x1 = _merge(x, ret[0], gdn[0], gdn[1], proj, mod, wts["norm_w"], wts["dn_norm_w"],
                wts["w_ret_o"], wts["w_dn_o"], wts["w_out"], rows_per_cond)
    y = _ffn(x1, mod, wts["norm_w"], wts["w_ffn_in"], wts["w_ffn_out"], rows_per_cond)
    if emit_state:
        return y, ret[1], gdn[2]
    return y, None, None


def kernel(x_prompt, x_sample, c, state_ret, state_dn, c_ctx, w_mod, b_mod, norm_w, w_in, conv_w,
           ret_decay_logit, ret_gn_w, dn_a_log, dn_dt_bias, dn_norm_w, w_ret_o, w_dn_o, w_out,
           w_ffn_in, w_ffn_out):
    n_ctx, l_ctx, _ = x_prompt.shape
    n_lat, l_lat, _ = x_sample.shape
    assert w_mod.shape[0] == 1, "single-layer kernel"

    cond8 = jnp.zeros((8, D_MODEL), F32).at[0].set(c_ctx).at[1:1 + n_lat].set(c)
    mod = _modulation(cond8, w_mod[0], b_mod)
    mod_ctx, mod_lat = mod[0:1], mod[1:1 + n_lat]

    wi = w_in[0]
    c_dq = 2 * QK_W + 2 * V_W
    c_dz = c_dq + DQKV_W
    c_gate = c_dz + V_W
    c_gr = c_gate + 2 * N_GATE
    w_main = jnp.concatenate([wi[:, :c_dq], wi[:, c_dz:c_gate], wi[:, c_dq:c_dz], wi[:, c_gr:]],
                             axis=1).astype(BF16)
    wts = {
        "norm_w": norm_w[0],
        "w_main": w_main,
        "w_gate_t": wi[:, c_gate:c_gr].T.astype(BF16),
        "logit_rep": jnp.broadcast_to(ret_decay_logit[0].T[:, :, None], (HEADS, 2, DK)),
        "ret_gn_w": ret_gn_w,
        "conv_w": conv_w[0],
        "a_log": dn_a_log.reshape(N_GATE, 1),
        "dt_bias": dn_dt_bias.reshape(N_GATE, 1),
        "dn_norm_w": dn_norm_w,
        "w_ret_o": w_ret_o[0].astype(BF16),
        "w_dn_o": w_dn_o[0].astype(BF16),
        "w_out": w_out[0].astype(BF16),
        "w_ffn_in": w_ffn_in[0].astype(BF16),
        "w_ffn_out": w_ffn_out[0].astype(BF16),
    }

    y_p, s_ret, s_dn = _one_path(x_prompt.reshape(n_ctx * l_ctx, D_MODEL), mod_ctx, l_ctx, n_ctx,
                                 wts, None, None, None, True)
    y_s, _, _ = _one_path(x_sample.reshape(n_lat * l_lat, D_MODEL), mod_lat, l_lat, n_lat,
                          wts, _rope_tables(l_lat), state_ret, state_dn, False)
    return (y_p.reshape(x_prompt.shape), y_s.reshape(x_sample.shape), s_ret, s_dn)
```

```python
import functools

import jax
import jax.numpy as jnp
from jax import lax
from jax.experimental import pallas as pl
from jax.experimental.pallas import tpu as pltpu

F32 = jnp.float32
BF16 = jnp.bfloat16

D_MODEL = 1024
HEADS = 4
DK = 128
DV = 256
QK_W = HEADS * DK
V_W = HEADS * DV
GDN_CHUNK = 64
TRI_BASE = 8
BLK = 256
SUB = 8
HALO = 16
CONV_WIDTH = 3
GRID_W = 64
ROPE_BASE = 10000.0
EPS = 1e-6
D_FF = 2816
FF_CHUNK = 1408
N_GATE = 2 * HEADS

COL_RQ, COL_RK, COL_RV, COL_RG = 0, 512, 1024, 2048
COL_DQK, COL_DV, COL_DZ = 3072, 4096, 5120
COL_GATE = 6144
COL_GR, COL_GD = 6144, 7168
PROJ_W = 8192
PROJ_TN = 2048

VMEM_LIMIT = 56 * 1024 * 1024


def _cparams(sem):
    return pltpu.CompilerParams(dimension_semantics=sem, vmem_limit_bytes=VMEM_LIMIT)


def _mm(a, b):
    return jnp.dot(a.astype(BF16), b.astype(BF16), preferred_element_type=F32)


def _mm_nt(a, b):
    return lax.dot_general(a.astype(BF16), b.astype(BF16), (((1,), (1,)), ((), ())),
                           preferred_element_type=F32)


def _mm_tn(a, b):
    return lax.dot_general(a.astype(BF16), b.astype(BF16), (((0,), (0,)), ((), ())),
                           preferred_element_type=F32)


def _sigmoid(x):
    return 1.0 / (1.0 + jnp.exp(-x))


def _silu(x):
    return x * _sigmoid(x)


def _softplus(x):
    return jnp.maximum(x, 0.0) + jnp.log(1.0 + jnp.exp(-jnp.abs(x)))


def _log_sigmoid(x):
    return -_softplus(-x)


def _rms(x, w):
    return x * lax.rsqrt(jnp.mean(x * x, axis=-1, keepdims=True) + EPS) * w


def _mod_kernel(c_ref, w_ref, b_ref, o_ref):
    o_ref[...] = _mm(_silu(c_ref[...]), w_ref[...]) + b_ref[...]


def _modulation(cond8, w_mod, b_mod):
    n = w_mod.shape[1]
    tn = 1536
    return pl.pallas_call(
        _mod_kernel,
        grid=(n // tn,),
        in_specs=[pl.BlockSpec((8, D_MODEL), lambda j: (0, 0)),
                  pl.BlockSpec((D_MODEL, tn), lambda j: (0, j)),
                  pl.BlockSpec((1, tn), lambda j: (0, j))],
        out_specs=pl.BlockSpec((8, tn), lambda j: (0, j)),
        out_shape=jax.ShapeDtypeStruct((8, n), F32),
        compiler_params=_cparams(("arbitrary",)),
        name="modulation",
    )(cond8, w_mod, b_mod)


def _inproj_kernel(x_ref, mod_ref, nw_ref, wa_ref, wb_ref, wg_ref, o_ref, g_ref, h_sc, *, tm,
                   rows_per_cond, n_a):
    i = pl.program_id(0)
    j = pl.program_id(1)

    @pl.when(j == 0)
    def _():
        ci = (i * tm) // rows_per_cond
        m = mod_ref[pl.ds(ci, 1), :]
        sh1 = m[:, 0:D_MODEL]
        sc1 = m[:, D_MODEL:2 * D_MODEL]
        h = _rms(x_ref[...], nw_ref[0:1, :]) * (1.0 + sc1) + sh1
        hb = h.astype(BF16)
        h_sc[...] = hb
        g_ref[...] = lax.dot_general(wg_ref[...], hb, (((1,), (1,)), ((), ())),
                                     preferred_element_type=F32)

    @pl.when(j < n_a)
    def _():
        o_ref[...] = jnp.dot(h_sc[...], wa_ref[...], preferred_element_type=F32).astype(o_ref.dtype)

    @pl.when(j >= n_a)
    def _():
        o_ref[...] = jnp.dot(h_sc[...], wb_ref[...], preferred_element_type=F32).astype(o_ref.dtype)


def _inproj(x, mod, norm_w, w_a, w_b, w_gate_t, rows_per_cond):
    t = x.shape[0]
    tm, tn = 1024, PROJ_TN
    n_a = w_a.shape[1] // tn
    kern = functools.partial(_inproj_kernel, tm=tm, rows_per_cond=rows_per_cond, n_a=n_a)
    return pl.pallas_call(
        kern,
        grid=(t // tm, PROJ_W // tn),
        in_specs=[pl.BlockSpec((tm, D_MODEL), lambda i, j: (i, 0)),
                  pl.BlockSpec(mod.shape, lambda i, j: (0, 0)),
                  pl.BlockSpec(norm_w.shape, lambda i, j: (0, 0)),
                  pl.BlockSpec((D_MODEL, tn), lambda i, j: (0, jnp.minimum(j, n_a - 1))),
                  pl.BlockSpec((D_MODEL, tn), lambda i, j: (0, jnp.maximum(j - n_a, 0))),
                  pl.BlockSpec((2 * N_GATE, D_MODEL), lambda i, j: (0, 0))],
        out_specs=[pl.BlockSpec((tm, tn), lambda i, j: (i, j)),
                   pl.BlockSpec((2 * N_GATE, tm), lambda i, j: (0, i))],
        out_shape=[jax.ShapeDtypeStruct((t, PROJ_W), BF16),
                   jax.ShapeDtypeStruct((2 * N_GATE, t), F32)],
        scratch_shapes=[pltpu.VMEM((tm, D_MODEL), BF16)],
        compiler_params=_cparams(("arbitrary", "arbitrary")),
        name="inproj",
    )(x, mod, norm_w, w_a, w_b, w_gate_t)


RET_GROUP = 8


def _pair_swap(x):
    lane = lax.broadcasted_iota(jnp.int32, x.shape, 1)
    nxt = pltpu.roll(x, x.shape[1] - 1, 1)
    prv = pltpu.roll(x, 1, 1)
    return jnp.where((lane & 1) == 0, nxt, prv)


def _ret_kernel(*refs, seq_len, rope, has_s0, emit_state):
    it = iter(refs)
    q_ref, k_ref, v_ref = next(it), next(it), next(it)
    cos_ref = next(it) if rope else None
    sin_ref = next(it) if rope else None
    logit_ref, gnw_ref = next(it), next(it)
    s0_ref = next(it) if has_s0 else None
    o_ref = next(it)
    st_ref = next(it) if emit_state else None
    q_sc, k_sc, st_sc = next(it), next(it), next(it)

    nb = seq_len // BLK
    cross = has_s0 or nb > 1

    def rows(c):
        return slice(c * BLK, (c + 1) * BLK)

    def qk_cols(h):
        return slice(h * DK, (h + 1) * DK)

    def v_cols(h):
        return slice(h * DV, (h + 1) * DV)

    for h in range(HEADS):
        q = q_ref[:, qk_cols(h)].astype(F32) * (DK ** -0.5)
        k = k_ref[:, qk_cols(h)].astype(F32)
        if rope:
            cos = cos_ref[...]
            sin = sin_ref[...]
            q = q * cos + _pair_swap(q) * sin
            k = k * cos + _pair_swap(k) * sin
        q_sc[:, qk_cols(h)] = q
        k_sc[:, qk_cols(h)] = k

    pos = lax.broadcasted_iota(jnp.int32, (BLK, DK), 0).astype(F32)

    def log_gammas(h):
        lg = _log_sigmoid(logit_ref[h])
        return lg[0:1, :], lg[1:2, :]

    for h in range(HEADS):
        lgf, lgb = log_gammas(h)
        kdec_f = jnp.exp(lgf * (BLK - 1.0 - pos))
        kdec_b = jnp.exp(lgb * pos)
        cdec_f = jnp.exp(lgf[:, 0:1] * float(BLK))
        cdec_b = jnp.exp(lgb[:, 0:1] * float(BLK))
        af = {c: _mm_tn(k_sc[rows(c), qk_cols(h)] * kdec_f, v_ref[rows(c), v_cols(h)])
              for c in range(nb) if c < nb - 1 or emit_state}
        ab = {c: _mm_tn(k_sc[rows(c), qk_cols(h)] * kdec_b, v_ref[rows(c), v_cols(h)])
              for c in range(nb) if c > 0 or emit_state}
        s = s0_ref[0, h] if has_s0 else jnp.zeros((DK, DV), F32)
        for c in range(nb):
            if cross:
                st_sc[h, c, 0:DK, :] = s.astype(BF16)
            if c in af:
                s = s * cdec_f + af[c]
        if emit_state:
            st_ref[0, h] = s
        s = s0_ref[1, h] if has_s0 else jnp.zeros((DK, DV), F32)
        for c in range(nb - 1, -1, -1):
            if cross:
                st_sc[h, c, DK:2 * DK, :] = s.astype(BF16)
            if c in ab:
                s = s * cdec_b + ab[c]
        if emit_state:
            st_ref[1, h] = s

    ii = lax.broadcasted_iota(jnp.int32, (BLK, BLK), 0)
    jj = lax.broadcasted_iota(jnp.int32, (BLK, BLK), 1)
    diff = (ii - jj).astype(F32)
    pos2 = lax.broadcasted_iota(jnp.int32, (BLK, 2 * DK), 0).astype(F32)
    first_half = lax.broadcasted_iota(jnp.int32, (BLK, 2 * DK), 1) < DK

    def head_consts(h):
        lgf, lgb = log_gammas(h)
        lgf1, lgb1 = lgf[:, 0:1], lgb[:, 0:1]
        dmat = (jnp.where(diff >= 0, jnp.exp(lgf1 * jnp.maximum(diff, 0.0)), 0.0)
                + jnp.where(diff <= 0, jnp.exp(lgb1 * jnp.maximum(-diff, 0.0)), 0.0))
        qdec = jnp.where(first_half, jnp.exp(lgf1 * (pos2 + 1.0)), jnp.exp(lgb1 * (BLK - pos2)))
        return dmat, qdec

    def unit(h, c, consts):
        dmat, qdec = consts
        qc = q_sc[rows(c), qk_cols(h)]
        s = _mm_nt(qc, k_sc[rows(c), qk_cols(h)])
        yield
        o = _mm(s * dmat, v_ref[rows(c), v_cols(h)])
        if cross:
            o_cross = _mm(jnp.concatenate([qc, qc], axis=1) * qdec, st_sc[h, c])
        yield
        if cross:
            o = o + o_cross
        mu = jnp.mean(o, axis=-1, keepdims=True)
        oc = o - mu
        var = jnp.mean(oc * oc, axis=-1, keepdims=True)
        o_ref[rows(c), v_cols(h)] = (oc * lax.rsqrt(var + EPS) * gnw_ref[:, v_cols(h)]).astype(o_ref.dtype)
        yield

    units = [(h, c) for h in range(HEADS) for c in range(nb)]
    for g0 in range(0, len(units), RET_GROUP):
        consts = {}
        gens = []
        for h, c in units[g0:g0 + RET_GROUP]:
            if h not in consts:
                consts[h] = head_consts(h)
            gens.append(unit(h, c, consts[h]))
        for _ in zip(*gens):
            pass


def _retention(proj, logit_rep, gn_w, n_seq, seq_len, rope_tabs=None, s0=None, emit_state=False):
    t = proj.shape[0]
    rope = rope_tabs is not None
    has_s0 = s0 is not None
    kern = functools.partial(_ret_kernel, seq_len=seq_len, rope=rope, has_s0=has_s0,
                             emit_state=emit_state)
    in_specs = [pl.BlockSpec((seq_len, QK_W), lambda b: (b, COL_RQ // QK_W)),
                pl.BlockSpec((seq_len, QK_W), lambda b: (b, COL_RK // QK_W)),
                pl.BlockSpec((seq_len, V_W), lambda b: (b, COL_RV // V_W))]
    args = [proj, proj, proj]
    if rope:
        in_specs += [pl.BlockSpec((seq_len, DK), lambda b: (0, 0))] * 2
        args += list(rope_tabs)
    in_specs += [pl.BlockSpec((HEADS, 2, DK), lambda b: (0, 0, 0)),
                 pl.BlockSpec((1, V_W), lambda b: (0, 0))]
    args += [logit_rep, gn_w]
    state_spec = pl.BlockSpec((None, None, 2, HEADS, DK, DV), lambda b: (b, 0, 0, 0, 0, 0))
    if has_s0:
        in_specs.append(state_spec)
        args.append(s0)
    out_specs = [pl.BlockSpec((seq_len, V_W), lambda b: (b, 0))]
    out_shape = [jax.ShapeDtypeStruct((t, V_W), BF16)]
    if emit_state:
        out_specs.append(state_spec)
        out_shape.append(jax.ShapeDtypeStruct((n_seq, 1, 2, HEADS, DK, DV), F32))
    nb = seq_len // BLK
    return pl.pallas_call(
        kern,
        grid=(n_seq,),
        in_specs=in_specs,
        out_specs=out_specs,
        out_shape=out_shape,
        scratch_shapes=[pltpu.VMEM((seq_len, QK_W), F32), pltpu.VMEM((seq_len, QK_W), F32),
                        pltpu.VMEM((HEADS, nb, 2 * DK, DV), BF16)],
        compiler_params=_cparams(("arbitrary",)),
        name="retention",
    )(*args)


N_MASK = 8
(MASK_EQ_F, MASK_EQ_B, MASK_OFFDIAG, MASK_EYE, MASK_BASE, MASK_PAIR0) = 0, 1, 2, 3, 4, 5


def _build_masks(mask_sc, eye_sc):
    ii = lax.broadcasted_iota(jnp.int32, (BLK, BLK), 0)
    jj = lax.broadcasted_iota(jnp.int32, (BLK, BLK), 1)

    def blk(v, size):
        return lax.shift_right_logical(v, size.bit_length() - 1)

    same = blk(ii, GDN_CHUNK) == blk(jj, GDN_CHUNK)
    mask_sc[MASK_EQ_F] = (same & (jj <= ii)).astype(F32)
    mask_sc[MASK_EQ_B] = (same & (jj >= ii)).astype(F32)
    mask_sc[MASK_OFFDIAG] = (ii != jj).astype(F32)
    mask_sc[MASK_EYE] = (ii == jj).astype(F32)
    mask_sc[MASK_BASE] = (blk(ii, TRI_BASE) == blk(jj, TRI_BASE)).astype(F32)
    size, idx = TRI_BASE, MASK_PAIR0
    while size < GDN_CHUNK:
        pair = (blk(ii, 2 * size) == blk(jj, 2 * size)) & (blk(ii, size) != blk(jj, size))
        mask_sc[idx] = pair.astype(F32)
        size, idx = 2 * size, idx + 1
    eye_sc[...] = (ii == jj).astype(BF16)


def _seg_cumsums(x):
    n = x.shape[1]
    pos = lax.broadcasted_iota(jnp.int32, x.shape, 1) & (GDN_CHUNK - 1)
    up, down = x, x
    s = 1
    while s < GDN_CHUNK:
        up = up + jnp.where(pos >= s, pltpu.roll(up, s, 1), 0.0)
        down = down + jnp.where(pos < GDN_CHUNK - s, pltpu.roll(down, n - s, 1), 0.0)
        s *= 2
    return up, down


def _rows_to_cols(rows, eye_sc):
    p1 = rows.astype(BF16).astype(F32)
    r1 = rows - p1
    p2 = r1.astype(BF16).astype(F32)
    p3 = r1 - p2
    pieces = jnp.concatenate([p1, p2, p3], axis=0).astype(BF16)
    c3 = lax.dot_general(eye_sc[...], pieces, (((1,), (1,)), ((), ())), preferred_element_type=F32)
    return c3[:, 0:SUB] + c3[:, SUB:2 * SUB] + c3[:, 2 * SUB:3 * SUB]


def _gdn_kernel(*refs, nb, has_s0, emit_state):
    it = iter(refs)
    xqk_refs, xv_refs = (next(it), next(it)), (next(it), next(it))
    pqk_refs, pv_refs = (next(it), next(it)), (next(it), next(it))
    nqk_refs, nv_refs = (next(it), next(it)), (next(it), next(it))
    g_refs = (next(it), next(it))
    cw_ref, alog_ref, dtb_ref = next(it), next(it), next(it)
    s0_ref = next(it) if has_s0 else None
    o_refs = (next(it), next(it))
    st_ref = next(it) if emit_state else None
    st_sc, mask_sc, eye_sc = next(it), next(it), next(it)

    b = pl.program_id(0)
    step = pl.program_id(1)

    @pl.when((b == 0) & (step == 0))
    def _():
        _build_masks(mask_sc, eye_sc)

    @pl.when(step == 0)
    def _():
        for d in range(2):
            for h in range(HEADS):
                st_sc[d, h] = s0_ref[d, h] if has_s0 else jnp.zeros((DK, DV), F32)

    blk_of = (step, nb - 1 - step)

    row8 = lax.broadcasted_iota(jnp.int32, (N_GATE, BLK), 0)
    fwd_rows = row8 < HEADS
    logit_b = jnp.where(fwd_rows, g_refs[0][0:N_GATE, :], g_refs[1][0:N_GATE, :])
    logit_a = jnp.where(fwd_rows, g_refs[0][N_GATE:2 * N_GATE, :], g_refs[1][N_GATE:2 * N_GATE, :])
    beta = _sigmoid(logit_b)
    la = -jnp.exp(alog_ref[...]) * _softplus(logit_a + dtb_ref[...])
    up, down = _seg_cumsums(la)
    g8 = jnp.where(fwd_rows, up, down)
    r8 = jnp.where(fwd_rows, down, up) - la
    eg8 = jnp.exp(g8)
    erb8 = jnp.exp(r8) * beta
    etot8 = jnp.exp(up + down - la)
    gcols = _rows_to_cols(g8, eye_sc)

    nck = BLK // GDN_CHUNK

    def l2n(x):
        return x * lax.rsqrt(jnp.sum(x * x, axis=-1, keepdims=True) + EPS)

    def chain(d, h):
        has_prev = blk_of[d] > 0
        has_next = blk_of[d] < nb - 1

        def conv(x_ref, hp_ref, hn_ref, lo, width, w_lo):
            cols = slice(lo, lo + width)
            cur = x_ref[:, cols].astype(F32)
            row = lax.broadcasted_iota(jnp.int32, cur.shape, 0)
            before = jnp.where(has_prev, hp_ref[HALO - 1:HALO, cols].astype(F32), 0.0)
            after = jnp.where(has_next, hn_ref[0:1, cols].astype(F32), 0.0)
            prev = jnp.where(row == 0, before, pltpu.roll(cur, 1, 0))
            nxt = jnp.where(row == BLK - 1, after, pltpu.roll(cur, BLK - 1, 0))
            cw = cw_ref[:, w_lo:w_lo + width]
            return _silu(prev * cw[0:1, :] + cur * cw[1:2, :] + nxt * cw[2:3, :])

        c = d * HEADS + h
        qk_in = (xqk_refs[d], pqk_refs[d], nqk_refs[d])
        v_in = (xv_refs[d], pv_refs[d], nv_refs[d])
        q = l2n(conv(*qk_in, h * DK, DK, h * DK)) * (DK ** -0.5)
        k = l2n(conv(*qk_in, QK_W + h * DK, DK, QK_W + h * DK))
        v16 = conv(*v_in, h * DV, DV, 2 * QK_W + h * DV).astype(BF16)
        k16 = k.astype(BF16)
        kk = _mm_nt(k16, k16)
        qk = _mm_nt(q, k16)
        yield
        g_col = gcols[:, c:c + 1]
        g_row = g8[c:c + 1, :]
        m_eq = mask_sc[MASK_EQ_F if d == 0 else MASK_EQ_B]
        decay_beta = (jnp.exp(jnp.minimum(g_col - g_row, 0.0)) * m_eq) * beta[c:c + 1, :]
        lp = kk * decay_beta * mask_sc[MASK_OFFDIAG]
        sc16 = (qk * decay_beta).astype(BF16)
        qg16 = (q * jnp.exp(g_col)).astype(BF16)
        kt16 = (k.T * erb8[c:c + 1, :]).astype(BF16)

        x = -(lp * mask_sc[MASK_BASE])
        t = mask_sc[MASK_EYE] + x
        for _ in range(TRI_BASE.bit_length() - 2):
            x = _mm(x, x)
            yield
            t = t + _mm(t, x)
            yield
        size, idx = TRI_BASE, MASK_PAIR0
        while size < GDN_CHUNK:
            et = _mm(lp * mask_sc[idx], t)
            yield
            t = t - _mm(t, et)
            yield
            size, idx = 2 * size, idx + 1
        u = _mm(t, v16)
        w16 = _mm(t * eg8[c:c + 1, :], k16).astype(BF16)
        yield

        order = range(nck) if d == 0 else range(nck - 1, -1, -1)
        for cc in order:
            rs = slice(cc * GDN_CHUNK, (cc + 1) * GDN_CHUNK)
            s_prev = st_sc[d, h]
            wq = jnp.concatenate([w16[rs, :], qg16[rs, :]], axis=0)
            ws_qs = _mm(wq, s_prev)
            yield
            v_new = (u[rs, :] - ws_qs[0:GDN_CHUNK]).astype(BF16)
            o = ws_qs[GDN_CHUNK:] + _mm(sc16[rs, rs], v_new)
            decay_all = etot8[c:c + 1, cc * GDN_CHUNK:cc * GDN_CHUNK + 1]
            st_sc[d, h] = s_prev * decay_all + _mm(kt16[:, rs], v_new)
            o_refs[d][rs, h * DV:(h + 1) * DV] = o.astype(o_refs[d].dtype)
            yield

    chains = [chain(d, h) for d in range(2) for h in range(HEADS)]
    for _ in zip(*chains):
        pass

    if emit_state:
        @pl.when(step == nb - 1)
        def _():
            for d in range(2):
                for h in range(HEADS):
                    st_ref[d, h] = st_sc[d, h]


def _gdn(proj, gates, conv_w, a_log, dt_bias, n_seq, seq_len, s0=None, emit_state=False):
    t = proj.shape[0]
    nb = seq_len // BLK
    has_s0 = s0 is not None
    kern = functools.partial(_gdn_kernel, nb=nb, has_s0=has_s0, emit_state=emit_state)
    halo_per_blk = BLK // HALO
    n_halo = t // HALO

    def fblk(b, s):
        return b * nb + s

    def bblk(b, s):
        return b * nb + nb - 1 - s

    def spec_x(blk, col):
        return pl.BlockSpec((BLK, V_W), lambda b, s: (blk(b, s), col // V_W))

    def spec_prev(blk, col):
        return pl.BlockSpec((HALO, V_W),
                            lambda b, s: (jnp.maximum(blk(b, s) * halo_per_blk - 1, 0), col // V_W))

    def spec_next(blk, col):
        return pl.BlockSpec((HALO, V_W),
                            lambda b, s: (jnp.minimum((blk(b, s) + 1) * halo_per_blk, n_halo - 1),
                                          col // V_W))

    def spec_g(blk):
        return pl.BlockSpec((2 * N_GATE, BLK), lambda b, s: (0, blk(b, s)))

    const2 = lambda b, s: (0, 0)
    in_specs = []
    for spec in (spec_x, spec_prev, spec_next):
        for col in (COL_DQK, COL_DV):
            in_specs += [spec(fblk, col), spec(bblk, col)]
    in_specs += [spec_g(fblk), spec_g(bblk),
                 pl.BlockSpec((CONV_WIDTH, 2 * QK_W + V_W), const2),
                 pl.BlockSpec((N_GATE, 1), const2),
                 pl.BlockSpec((N_GATE, 1), const2)]
    args = [proj] * 12 + [gates, gates, conv_w, a_log, dt_bias]
    state_spec = pl.BlockSpec((None, None, 2, HEADS, DK, DV), lambda b, s: (b, 0, 0, 0, 0, 0))
    if has_s0:
        in_specs.append(state_spec)
        args.append(s0)
    out_specs = [pl.BlockSpec((BLK, V_W), lambda b, s: (fblk(b, s), 0)),
                 pl.BlockSpec((BLK, V_W), lambda b, s: (bblk(b, s), 0))]
    out_shape = [jax.ShapeDtypeStruct((t, V_W), BF16), jax.ShapeDtypeStruct((t, V_W), BF16)]
    if emit_state:
        out_specs.append(state_spec)
        out_shape.append(jax.ShapeDtypeStruct((n_seq, 1, 2, HEADS, DK, DV), F32))
    scratch = [pltpu.VMEM((2, HEADS, DK, DV), F32),
               pltpu.VMEM((N_MASK, BLK, BLK), F32),
               pltpu.VMEM((BLK, BLK), BF16)]
    return pl.pallas_call(
        kern,
        grid=(n_seq, nb),
        in_specs=in_specs,
        out_specs=out_specs,
        out_shape=out_shape,
        scratch_shapes=scratch,
        compiler_params=_cparams(("arbitrary", "arbitrary")),
        name="gated_delta",
    )(*args)


def _merge_kernel(x_ref, or_ref, of_ref, ob_ref, rg_ref, dz_ref, gr_ref, gd_ref, mod_ref, nw_ref,
                  dnw_ref, wr_ref, wd_ref, wo_ref, o_ref, *, tm, rows_per_cond):
    i = pl.program_id(0)
    ci = (i * tm) // rows_per_cond
    m = mod_ref[pl.ds(ci, 1), :]
    g1 = m[:, 2 * D_MODEL:3 * D_MODEL]
    y_r = _mm(_silu(rg_ref[...].astype(F32)) * or_ref[...].astype(F32), wr_ref[...])
    dnw = dnw_ref[...]
    heads = []
    for h in range(HEADS):
        cols = slice(h * DV, (h + 1) * DV)
        od = _rms(of_ref[:, cols].astype(F32) + ob_ref[:, cols].astype(F32), dnw)
        heads.append((od * _silu(dz_ref[:, cols].astype(F32))).astype(BF16))
    y_d = jnp.dot(jnp.concatenate(heads, axis=1), wd_ref[...], preferred_element_type=F32)
    merged = _sigmoid(gr_ref[...].astype(F32)) * y_r + _sigmoid(gd_ref[...].astype(F32)) * y_d
    mo = _mm(merged, wo_ref[...])
    o_ref[...] = x_ref[...] + g1 * _rms(mo, nw_ref[1:2, :])


def _merge(x, o_r, o_f, o_b, proj, mod, norm_w, dn_norm_w, w_ret_o, w_dn_o, w_out, rows_per_cond):
    t = x.shape[0]
    tm = 512
    kern = functools.partial(_merge_kernel, tm=tm, rows_per_cond=rows_per_cond)
    row = lambda i: (i, 0)
    const = lambda i: (0, 0)
    wspec = pl.BlockSpec((D_MODEL, D_MODEL), const)
    return pl.pallas_call(
        kern,
        grid=(t // tm,),
        in_specs=[pl.BlockSpec((tm, D_MODEL), row),
                  pl.BlockSpec((tm, V_W), row),
                  pl.BlockSpec((tm, V_W), row),
                  pl.BlockSpec((tm, V_W), row),
                  pl.BlockSpec((tm, V_W), lambda i: (i, COL_RG // V_W)),
                  pl.BlockSpec((tm, V_W), lambda i: (i, COL_DZ // V_W)),
                  pl.BlockSpec((tm, D_MODEL), lambda i: (i, COL_GR // D_MODEL)),
                  pl.BlockSpec((tm, D_MODEL), lambda i: (i, COL_GD // D_MODEL)),
                  pl.BlockSpec(mod.shape, const),
                  pl.BlockSpec(norm_w.shape, const),
                  pl.BlockSpec(dn_norm_w.shape, const),
                  wspec, wspec, wspec],
        out_specs=pl.BlockSpec((tm, D_MODEL), row),
        out_shape=jax.ShapeDtypeStruct((t, D_MODEL), F32),
        compiler_params=_cparams(("arbitrary",)),
        name="merge_out",
    )(x, o_r, o_f, o_b, proj, proj, proj, proj, mod, norm_w, dn_norm_w, w_ret_o, w_dn_o, w_out)


def _ffn_kernel(x_ref, mod_ref, nw_ref, wi_ref, wo_ref, o_ref, *, tm, rows_per_cond):
    i = pl.program_id(0)
    ci = (i * tm) // rows_per_cond
    m = mod_ref[pl.ds(ci, 1), :]
    sh2 = m[:, 3 * D_MODEL:4 * D_MODEL]
    sc2 = m[:, 4 * D_MODEL:5 * D_MODEL]
    g2 = m[:, 5 * D_MODEL:6 * D_MODEL]
    x = x_ref[...]
    hb = (_rms(x, nw_ref[2:3, :]) * (1.0 + sc2) + sh2).astype(BF16)
    f = None
    for c in range(D_FF // FF_CHUNK):
        lo = c * FF_CHUNK
        gate = jnp.dot(hb, wi_ref[:, lo:lo + FF_CHUNK], preferred_element_type=F32)
        up = jnp.dot(hb, wi_ref[:, D_FF + lo:D_FF + lo + FF_CHUNK], preferred_element_type=F32)
        part = _mm(_silu(gate) * up, wo_ref[lo:lo + FF_CHUNK, :])
        f = part if f is None else f + part
    o_ref[...] = x + g2 * _rms(f, nw_ref[3:4, :])


def _ffn(x, mod, norm_w, w_in, w_out, rows_per_cond):
    t = x.shape[0]
    tm = 512
    kern = functools.partial(_ffn_kernel, tm=tm, rows_per_cond=rows_per_cond)
    row = lambda i: (i, 0)
    const = lambda i: (0, 0)
    resident = pl.Buffered(1)
    return pl.pallas_call(
        kern,
        grid=(t // tm,),
        in_specs=[pl.BlockSpec((tm, D_MODEL), row),
                  pl.BlockSpec(mod.shape, const),
                  pl.BlockSpec(norm_w.shape, const),
                  pl.BlockSpec((D_MODEL, 2 * D_FF), const, pipeline_mode=resident),
                  pl.BlockSpec((D_FF, D_MODEL), const, pipeline_mode=resident)],
        out_specs=pl.BlockSpec((tm, D_MODEL), row),
        out_shape=jax.ShapeDtypeStruct((t, D_MODEL), F32),
        compiler_params=_cparams(("arbitrary",)),
        name="swiglu",
    )(x, mod, norm_w, w_in, w_out)


def _rope_tables(seq_len):
    rows = seq_len // GRID_W
    row_idx = jnp.repeat(jnp.arange(rows, dtype=F32), GRID_W)
    col_idx = (jnp.arange(seq_len) % GRID_W).astype(F32)
    n_freq = DK // 4
    freqs = ROPE_BASE ** (-jnp.arange(n_freq, dtype=F32) / n_freq)
    ang = jnp.concatenate([row_idx[:, None] * freqs, col_idx[:, None] * freqs], axis=-1)
    cos = jnp.repeat(jnp.cos(ang), 2, axis=-1)
    sin = jnp.repeat(jnp.sin(ang), 2, axis=-1)
    sign = jnp.tile(jnp.array([-1.0, 1.0], F32), DK // 2)
    return cos, sin * sign


def _one_path(x, mod, seq_len, n_seq, wts, rope_tabs, s_ret0, s_dn0, emit_state):
    rows_per_cond = seq_len if mod.shape[0] > 1 else x.shape[0]
    proj, gates = _inproj(x, mod, wts["norm_w"], wts["w_a"], wts["w_b"], wts["w_gate_t"],
                          rows_per_cond)
    ret = _retention(proj, wts["logit_rep"], wts["ret_gn_w"], n_seq, seq_len,
                     rope_tabs=rope_tabs, s0=s_ret0, emit_state=emit_state)
    gdn = _gdn(proj, gates, wts["conv_w"], wts["a_log"], wts["dt_bias"], n_seq, seq_len,
               s0=s_dn0, emit_state=emit_state)
    x1 = _merge(x, ret[0], gdn[0], gdn[1], proj, mod, wts["norm_w"], wts["dn_norm_w"],
                wts["w_ret_o"], wts["w_dn_o"], wts["w_out"], rows_per_cond)
    y = _ffn(x1, mod, wts["norm_w"], wts["w_ffn_in"], wts["w_ffn_out"], rows_per_cond)
    if emit_state:
        return y, ret[1], gdn[2]
    return y, None, None


def kernel(x_prompt, x_sample, c, state_ret, state_dn, c_ctx, w_mod, b_mod, norm_w, w_in, conv_w,
           ret_decay_logit, ret_gn_w, dn_a_log, dn_dt_bias, dn_norm_w, w_ret_o, w_dn_o, w_out,
           w_ffn_in, w_ffn_out):
    n_ctx, l_ctx, _ = x_prompt.shape
    n_lat, l_lat, _ = x_sample.shape
    assert w_mod.shape[0] == 1, "single-layer kernel"

    cond8 = jnp.zeros((8, D_MODEL), F32).at[0].set(c_ctx).at[1:1 + n_lat].set(c)
    mod = _modulation(cond8, w_mod[0], b_mod)
    mod_ctx, mod_lat = mod[0:1], mod[1:1 + n_lat]

    wi = w_in[0]
    c_gr = COL_GATE + 2 * N_GATE
    wts = {
        "norm_w": norm_w[0],
        "w_a": wi[:, :COL_GATE].astype(BF16),
        "w_b": wi[:, c_gr:].astype(BF16),
        "w_gate_t": wi[:, COL_GATE:c_gr].T.astype(BF16),
        "logit_rep": jnp.broadcast_to(ret_decay_logit[0].T[:, :, None], (HEADS, 2, DK)),
        "ret_gn_w": ret_gn_w,
        "conv_w": conv_w[0],
        "a_log": dn_a_log.reshape(N_GATE, 1),
        "dt_bias": dn_dt_bias.reshape(N_GATE, 1),
        "dn_norm_w": dn_norm_w,
        "w_ret_o": w_ret_o[0].astype(BF16),
        "w_dn_o": w_dn_o[0].astype(BF16),
        "w_out": w_out[0].astype(BF16),
        "w_ffn_in": w_ffn_in[0].astype(BF16),
        "w_ffn_out": w_ffn_out[0].astype(BF16),
    }

    y_p, s_ret, s_dn = _one_path(x_prompt.reshape(n_ctx * l_ctx, D_MODEL), mod_ctx, l_ctx, n_ctx,
                                 wts, None, None, None, True)
    y_s, _, _ = _one_path(x_sample.reshape(n_lat * l_lat, D_MODEL), mod_lat, l_lat, n_lat,
                          wts, _rope_tables(l_lat), state_ret, state_dn, False)
    return (y_p.reshape(x_prompt.shape), y_s.reshape(x_sample.shape), s_ret, s_dn)
```

```python
import functools

import jax
import jax.numpy as jnp
from jax import lax
from jax.experimental import pallas as pl
from jax.experimental.pallas import tpu as pltpu

F32 = jnp.float32
BF16 = jnp.bfloat16

D_MODEL = 1024
HEADS = 4
DK = 128
DV = 256
QK_W = HEADS * DK
V_W = HEADS * DV
GDN_CHUNK = 64
TRI_BASE = 8
BLK = 256
SUB = 8
HALO = 16
CONV_WIDTH = 3
GRID_W = 64
ROPE_BASE = 10000.0
EPS = 1e-6
D_FF = 2816
FF_CHUNK = 1408
N_GATE = 2 * HEADS

COL_RQ, COL_RK, COL_RV, COL_RG = 0, 512, 1024, 2048
COL_DQK, COL_DV, COL_DZ = 3072, 4096, 5120
COL_GATE = 6144
COL_GR, COL_GD = 6144, 7168
PROJ_W = 8192
PROJ_TN = 2048

VMEM_LIMIT = 56 * 1024 * 1024


def _cparams(sem):
    return pltpu.CompilerParams(dimension_semantics=sem, vmem_limit_bytes=VMEM_LIMIT)


def _mm(a, b):
    return jnp.dot(a.astype(BF16), b.astype(BF16), preferred_element_type=F32)


def _mm_nt(a, b):
    return lax.dot_general(a.astype(BF16), b.astype(BF16), (((1,), (1,)), ((), ())),
                           preferred_element_type=F32)


def _mm_tn(a, b):
    return lax.dot_general(a.astype(BF16), b.astype(BF16), (((0,), (0,)), ((), ())),
                           preferred_element_type=F32)


def _sigmoid(x):
    return 1.0 / (1.0 + jnp.exp(-x))


def _silu(x):
    return x * _sigmoid(x)


def _softplus(x):
    return jnp.maximum(x, 0.0) + jnp.log(1.0 + jnp.exp(-jnp.abs(x)))


def _log_sigmoid(x):
    return -_softplus(-x)


def _rms(x, w):
    return x * lax.rsqrt(jnp.mean(x * x, axis=-1, keepdims=True) + EPS) * w


def _mod_kernel(c_ref, w_ref, b_ref, o_ref):
    o_ref[...] = _mm(_silu(c_ref[...]), w_ref[...]) + b_ref[...]


def _modulation(cond8, w_mod, b_mod):
    n = w_mod.shape[1]
    tn = 1536
    return pl.pallas_call(
        _mod_kernel,
        grid=(n // tn,),
        in_specs=[pl.BlockSpec((8, D_MODEL), lambda j: (0, 0)),
                  pl.BlockSpec((D_MODEL, tn), lambda j: (0, j)),
                  pl.BlockSpec((1, tn), lambda j: (0, j))],
        out_specs=pl.BlockSpec((8, tn), lambda j: (0, j)),
        out_shape=jax.ShapeDtypeStruct((8, n), F32),
        compiler_params=_cparams(("arbitrary",)),
        name="modulation",
    )(cond8, w_mod, b_mod)


def _inproj_kernel(x_ref, mod_ref, nw_ref, wa_ref, wb_ref, wg_ref, o_ref, g_ref, h_sc, *, tm,
                   rows_per_cond, n_a):
    i = pl.program_id(0)
    j = pl.program_id(1)

    @pl.when(j == 0)
    def _():
        ci = (i * tm) // rows_per_cond
        m = mod_ref[pl.ds(ci, 1), :]
        sh1 = m[:, 0:D_MODEL]
        sc1 = m[:, D_MODEL:2 * D_MODEL]
        h = _rms(x_ref[...], nw_ref[0:1, :]) * (1.0 + sc1) + sh1
        hb = h.astype(BF16)
        h_sc[...] = hb
        g_ref[...] = lax.dot_general(wg_ref[...].astype(BF16), hb, (((1,), (1,)), ((), ())),
                                     preferred_element_type=F32)

    @pl.when(j < n_a)
    def _():
        o_ref[...] = jnp.dot(h_sc[...], wa_ref[...], preferred_element_type=F32).astype(o_ref.dtype)

    @pl.when(j >= n_a)
    def _():
        o_ref[...] = jnp.dot(h_sc[...], wb_ref[...], preferred_element_type=F32).astype(o_ref.dtype)


def _inproj(x, mod, norm_w, w_a, w_b, w_gate_t, rows_per_cond):
    t = x.shape[0]
    tm, tn = 1024, PROJ_TN
    n_a = w_a.shape[1] // tn
    kern = functools.partial(_inproj_kernel, tm=tm, rows_per_cond=rows_per_cond, n_a=n_a)
    return pl.pallas_call(
        kern,
        grid=(t // tm, PROJ_W // tn),
        in_specs=[pl.BlockSpec((tm, D_MODEL), lambda i, j: (i, 0)),
                  pl.BlockSpec(mod.shape, lambda i, j: (0, 0)),
                  pl.BlockSpec(norm_w.shape, lambda i, j: (0, 0)),
                  pl.BlockSpec((D_MODEL, tn), lambda i, j: (0, jnp.minimum(j, n_a - 1))),
                  pl.BlockSpec((D_MODEL, tn), lambda i, j: (0, jnp.maximum(j - n_a, 0))),
                  pl.BlockSpec((2 * N_GATE, D_MODEL), lambda i, j: (0, 0))],
        out_specs=[pl.BlockSpec((tm, tn), lambda i, j: (i, j)),
                   pl.BlockSpec((2 * N_GATE, tm), lambda i, j: (0, i))],
        out_shape=[jax.ShapeDtypeStruct((t, PROJ_W), BF16),
                   jax.ShapeDtypeStruct((2 * N_GATE, t), F32)],
        scratch_shapes=[pltpu.VMEM((tm, D_MODEL), BF16)],
        compiler_params=_cparams(("arbitrary", "arbitrary")),
        name="inproj",
    )(x, mod, norm_w, w_a, w_b, w_gate_t)


RET_GROUP = 8


def _pair_swap(x):
    lane = lax.broadcasted_iota(jnp.int32, x.shape, 1)
    nxt = pltpu.roll(x, x.shape[1] - 1, 1)
    prv = pltpu.roll(x, 1, 1)
    return jnp.where((lane & 1) == 0, nxt, prv)


def _ret_kernel(*refs, seq_len, rope, has_s0, emit_state):
    it = iter(refs)
    q_ref, k_ref, v_ref = next(it), next(it), next(it)
    cos_ref = next(it) if rope else None
    sin_ref = next(it) if rope else None
    logit_ref, gnw_ref = next(it), next(it)
    s0_ref = next(it) if has_s0 else None
    o_ref = next(it)
    st_ref = next(it) if emit_state else None
    q_sc, k_sc, st_sc = next(it), next(it), next(it)

    nb = seq_len // BLK
    cross = has_s0 or nb > 1

    def rows(c):
        return slice(c * BLK, (c + 1) * BLK)

    def qk_cols(h):
        return slice(h * DK, (h + 1) * DK)

    def v_cols(h):
        return slice(h * DV, (h + 1) * DV)

    for h in range(HEADS):
        q = q_ref[:, qk_cols(h)].astype(F32) * (DK ** -0.5)
        k = k_ref[:, qk_cols(h)].astype(F32)
        if rope:
            cos = cos_ref[...]
            sin = sin_ref[...]
            q = q * cos + _pair_swap(q) * sin
            k = k * cos + _pair_swap(k) * sin
        q_sc[:, qk_cols(h)] = q
        k_sc[:, qk_cols(h)] = k

    pos = lax.broadcasted_iota(jnp.int32, (BLK, DK), 0).astype(F32)

    def log_gammas(h):
        lg = _log_sigmoid(logit_ref[h])
        return lg[0:1, :], lg[1:2, :]

    for h in range(HEADS):
        lgf, lgb = log_gammas(h)
        kdec_f = jnp.exp(lgf * (BLK - 1.0 - pos))
        kdec_b = jnp.exp(lgb * pos)
        cdec_f = jnp.exp(lgf[:, 0:1] * float(BLK))
        cdec_b = jnp.exp(lgb[:, 0:1] * float(BLK))
        af = {c: _mm_tn(k_sc[rows(c), qk_cols(h)] * kdec_f, v_ref[rows(c), v_cols(h)])
              for c in range(nb) if c < nb - 1 or emit_state}
        ab = {c: _mm_tn(k_sc[rows(c), qk_cols(h)] * kdec_b, v_ref[rows(c), v_cols(h)])
              for c in range(nb) if c > 0 or emit_state}
        s = s0_ref[0, h] if has_s0 else jnp.zeros((DK, DV), F32)
        for c in range(nb):
            if cross:
                st_sc[h, c, 0:DK, :] = s.astype(BF16)
            if c in af:
                s = s * cdec_f + af[c]
        if emit_state:
            st_ref[0, h] = s
        s = s0_ref[1, h] if has_s0 else jnp.zeros((DK, DV), F32)
        for c in range(nb - 1, -1, -1):
            if cross:
                st_sc[h, c, DK:2 * DK, :] = s.astype(BF16)
            if c in ab:
                s = s * cdec_b + ab[c]
        if emit_state:
            st_ref[1, h] = s

    ii = lax.broadcasted_iota(jnp.int32, (BLK, BLK), 0)
    jj = lax.broadcasted_iota(jnp.int32, (BLK, BLK), 1)
    diff = (ii - jj).astype(F32)
    pos2 = lax.broadcasted_iota(jnp.int32, (BLK, 2 * DK), 0).astype(F32)
    first_half = lax.broadcasted_iota(jnp.int32, (BLK, 2 * DK), 1) < DK

    def head_consts(h):
        lgf, lgb = log_gammas(h)
        lgf1, lgb1 = lgf[:, 0:1], lgb[:, 0:1]
        dmat = (jnp.where(diff >= 0, jnp.exp(lgf1 * jnp.maximum(diff, 0.0)), 0.0)
                + jnp.where(diff <= 0, jnp.exp(lgb1 * jnp.maximum(-diff, 0.0)), 0.0))
        qdec = jnp.where(first_half, jnp.exp(lgf1 * (pos2 + 1.0)), jnp.exp(lgb1 * (BLK - pos2)))
        return dmat, qdec

    def unit(h, c, consts):
        dmat, qdec = consts
        qc = q_sc[rows(c), qk_cols(h)]
        s = _mm_nt(qc, k_sc[rows(c), qk_cols(h)])
        yield
        o = _mm(s * dmat, v_ref[rows(c), v_cols(h)])
        if cross:
            o_cross = _mm(jnp.concatenate([qc, qc], axis=1) * qdec, st_sc[h, c])
        yield
        if cross:
            o = o + o_cross
        mu = jnp.mean(o, axis=-1, keepdims=True)
        oc = o - mu
        var = jnp.mean(oc * oc, axis=-1, keepdims=True)
        o_ref[rows(c), v_cols(h)] = (oc * lax.rsqrt(var + EPS) * gnw_ref[:, v_cols(h)]).astype(o_ref.dtype)
        yield

    units = [(h, c) for h in range(HEADS) for c in range(nb)]
    for g0 in range(0, len(units), RET_GROUP):
        consts = {}
        gens = []
        for h, c in units[g0:g0 + RET_GROUP]:
            if h not in consts:
                consts[h] = head_consts(h)
            gens.append(unit(h, c, consts[h]))
        for _ in zip(*gens):
            pass


def _retention(proj, logit_rep, gn_w, n_seq, seq_len, rope_tabs=None, s0=None, emit_state=False):
    t = proj.shape[0]
    rope = rope_tabs is not None
    has_s0 = s0 is not None
    kern = functools.partial(_ret_kernel, seq_len=seq_len, rope=rope, has_s0=has_s0,
                             emit_state=emit_state)
    in_specs = [pl.BlockSpec((seq_len, QK_W), lambda b: (b, COL_RQ // QK_W)),
                pl.BlockSpec((seq_len, QK_W), lambda b: (b, COL_RK // QK_W)),
                pl.BlockSpec((seq_len, V_W), lambda b: (b, COL_RV // V_W))]
    args = [proj, proj, proj]
    if rope:
        in_specs += [pl.BlockSpec((seq_len, DK), lambda b: (0, 0))] * 2
        args += list(rope_tabs)
    in_specs += [pl.BlockSpec((HEADS, 2, DK), lambda b: (0, 0, 0)),
                 pl.BlockSpec((1, V_W), lambda b: (0, 0))]
    args += [logit_rep, gn_w]
    state_spec = pl.BlockSpec((None, None, 2, HEADS, DK, DV), lambda b: (b, 0, 0, 0, 0, 0))
    if has_s0:
        in_specs.append(state_spec)
        args.append(s0)
    out_specs = [pl.BlockSpec((seq_len, V_W), lambda b: (b, 0))]
    out_shape = [jax.ShapeDtypeStruct((t, V_W), BF16)]
    if emit_state:
        out_specs.append(state_spec)
        out_shape.append(jax.ShapeDtypeStruct((n_seq, 1, 2, HEADS, DK, DV), F32))
    nb = seq_len // BLK
    return pl.pallas_call(
        kern,
        grid=(n_seq,),
        in_specs=in_specs,
        out_specs=out_specs,
        out_shape=out_shape,
        scratch_shapes=[pltpu.VMEM((seq_len, QK_W), F32), pltpu.VMEM((seq_len, QK_W), F32),
                        pltpu.VMEM((HEADS, nb, 2 * DK, DV), BF16)],
        compiler_params=_cparams(("arbitrary",)),
        name="retention",
    )(*args)


N_MASK = 8
(MASK_EQ_F, MASK_EQ_B, MASK_OFFDIAG, MASK_EYE, MASK_BASE, MASK_PAIR0) = 0, 1, 2, 3, 4, 5


def _build_masks(mask_sc, eye_sc):
    ii = lax.broadcasted_iota(jnp.int32, (BLK, BLK), 0)
    jj = lax.broadcasted_iota(jnp.int32, (BLK, BLK), 1)

    def blk(v, size):
        return lax.shift_right_logical(v, size.bit_length() - 1)

    same = blk(ii, GDN_CHUNK) == blk(jj, GDN_CHUNK)
    mask_sc[MASK_EQ_F] = (same & (jj <= ii)).astype(F32)
    mask_sc[MASK_EQ_B] = (same & (jj >= ii)).astype(F32)
    mask_sc[MASK_OFFDIAG] = (ii != jj).astype(F32)
    mask_sc[MASK_EYE] = (ii == jj).astype(F32)
    mask_sc[MASK_BASE] = (blk(ii, TRI_BASE) == blk(jj, TRI_BASE)).astype(F32)
    size, idx = TRI_BASE, MASK_PAIR0
    while size < GDN_CHUNK:
        pair = (blk(ii, 2 * size) == blk(jj, 2 * size)) & (blk(ii, size) != blk(jj, size))
        mask_sc[idx] = pair.astype(F32)
        size, idx = 2 * size, idx + 1
    eye_sc[...] = (ii == jj).astype(BF16)


def _seg_cumsums(x):
    n = x.shape[1]
    pos = lax.broadcasted_iota(jnp.int32, x.shape, 1) & (GDN_CHUNK - 1)
    up, down = x, x
    s = 1
    while s < GDN_CHUNK:
        up = up + jnp.where(pos >= s, pltpu.roll(up, s, 1), 0.0)
        down = down + jnp.where(pos < GDN_CHUNK - s, pltpu.roll(down, n - s, 1), 0.0)
        s *= 2
    return up, down


def _rows_to_cols(rows, eye_sc):
    p1 = rows.astype(BF16).astype(F32)
    r1 = rows - p1
    p2 = r1.astype(BF16).astype(F32)
    p3 = r1 - p2
    pieces = jnp.concatenate([p1, p2, p3], axis=0).astype(BF16)
    c3 = lax.dot_general(eye_sc[...], pieces, (((1,), (1,)), ((), ())), preferred_element_type=F32)
    return c3[:, 0:SUB] + c3[:, SUB:2 * SUB] + c3[:, 2 * SUB:3 * SUB]


def _gdn_kernel(*refs, nb, has_s0, emit_state):
    it = iter(refs)
    xs_refs = (next(it), next(it))
    g_refs = (next(it), next(it))
    alog_ref, dtb_ref = next(it), next(it)
    s0_ref = next(it) if has_s0 else None
    o_refs = (next(it), next(it))
    st_ref = next(it) if emit_state else None
    st_sc, mask_sc, eye_sc = next(it), next(it), next(it)

    b = pl.program_id(0)
    step = pl.program_id(1)

    @pl.when((b == 0) & (step == 0))
    def _():
        _build_masks(mask_sc, eye_sc)

    @pl.when(step == 0)
    def _():
        for d in range(2):
            for h in range(HEADS):
                st_sc[d, h] = s0_ref[d, h] if has_s0 else jnp.zeros((DK, DV), F32)

    blk_of = (step, nb - 1 - step)

    row8 = lax.broadcasted_iota(jnp.int32, (N_GATE, BLK), 0)
    fwd_rows = row8 < HEADS
    logit_b = jnp.where(fwd_rows, g_refs[0][0:N_GATE, :], g_refs[1][0:N_GATE, :])
    logit_a = jnp.where(fwd_rows, g_refs[0][N_GATE:2 * N_GATE, :], g_refs[1][N_GATE:2 * N_GATE, :])
    beta = _sigmoid(logit_b)
    la = -jnp.exp(alog_ref[...]) * _softplus(logit_a + dtb_ref[...])
    up, down = _seg_cumsums(la)
    g8 = jnp.where(fwd_rows, up, down)
    r8 = jnp.where(fwd_rows, down, up) - la
    eg8 = jnp.exp(g8)
    erb8 = jnp.exp(r8) * beta
    etot8 = jnp.exp(up + down - la)
    gcols = _rows_to_cols(g8, eye_sc)

    nck = BLK // GDN_CHUNK

    def chain(d, h):
        c = d * HEADS + h
        xs_ref = xs_refs[d]
        q16 = xs_ref[:, h * DK:(h + 1) * DK]
        k16 = xs_ref[:, QK_W + h * DK:QK_W + (h + 1) * DK]
        v16 = xs_ref[:, 2 * QK_W + h * DV:2 * QK_W + (h + 1) * DV]
        q = q16.astype(F32)
        k = k16.astype(F32)
        kk = _mm_nt(k16, k16)
        qk = _mm_nt(q16, k16)
        yield
        g_col = gcols[:, c:c + 1]
        g_row = g8[c:c + 1, :]
        m_eq = mask_sc[MASK_EQ_F if d == 0 else MASK_EQ_B]
        decay_beta = (jnp.exp(jnp.minimum(g_col - g_row, 0.0)) * m_eq) * beta[c:c + 1, :]
        lp = kk * decay_beta * mask_sc[MASK_OFFDIAG]
        sc16 = (qk * decay_beta).astype(BF16)
        qg16 = (q * jnp.exp(g_col)).astype(BF16)
        kt16 = (k.T * erb8[c:c + 1, :]).astype(BF16)

        x = -(lp * mask_sc[MASK_BASE])
        t = mask_sc[MASK_EYE] + x
        for _ in range(TRI_BASE.bit_length() - 2):
            x = _mm(x, x)
            yield
            t = t + _mm(t, x)
            yield
        size, idx = TRI_BASE, MASK_PAIR0
        while size < GDN_CHUNK:
            groups = [slice(r * size, (r + 1) * size) for r in range(BLK // size)]
            late = [(r % 2 == 1) == (d == 0) for r in range(BLK // size)]
            e_late = jnp.concatenate([lp[g, :] * mask_sc[idx, g, :] for g, on in zip(groups, late) if on],
                                     axis=0)
            et = _mm(e_late, t)
            yield
            t_late = jnp.concatenate([t[g, :] for g, on in zip(groups, late) if on], axis=0)
            et_rows = iter(et[j * size:(j + 1) * size, :] for j in range(BLK // size // 2))
            et_full = jnp.concatenate([next(et_rows) if on else jnp.zeros((size, BLK), F32)
                                       for on in late], axis=0)
            t_late = t_late - _mm(t_late, et_full)
            yield
            new_rows = iter(t_late[j * size:(j + 1) * size, :] for j in range(BLK // size // 2))
            t = jnp.concatenate([next(new_rows) if on else t[g, :] for g, on in zip(groups, late)],
                                axis=0)
            size, idx = 2 * size, idx + 1
        u = _mm(t, v16)
        w16 = _mm(t * eg8[c:c + 1, :], k16).astype(BF16)
        yield

        order = range(nck) if d == 0 else range(nck - 1, -1, -1)
        for cc in order:
            rs = slice(cc * GDN_CHUNK, (cc + 1) * GDN_CHUNK)
            s_prev = st_sc[d, h]
            wq = jnp.concatenate([w16[rs, :], qg16[rs, :]], axis=0)
            ws_qs = _mm(wq, s_prev)
            yield
            v_new = (u[rs, :] - ws_qs[0:GDN_CHUNK]).astype(BF16)
            o = ws_qs[GDN_CHUNK:] + _mm(sc16[rs, rs], v_new)
            decay_all = etot8[c:c + 1, cc * GDN_CHUNK:cc * GDN_CHUNK + 1]
            st_sc[d, h] = s_prev * decay_all + _mm(kt16[:, rs], v_new)
            o_refs[d][rs, h * DV:(h + 1) * DV] = o.astype(o_refs[d].dtype)
            yield

    chains = [chain(d, h) for d in range(2) for h in range(HEADS)]
    for _ in zip(*chains):
        pass

    if emit_state:
        @pl.when(step == nb - 1)
        def _():
            for d in range(2):
                for h in range(HEADS):
                    st_ref[d, h] = st_sc[d, h]


def _prep_kernel(xqk_ref, xv_ref, pqk_ref, pv_ref, nqk_ref, nv_ref, cw_ref, o_ref, *, seq_len, rb):
    t0 = pl.program_id(0) * rb

    def conv(x_ref, hp_ref, hn_ref, lo, width, w_lo):
        cols = slice(lo, lo + width)
        cur = x_ref[:, cols].astype(F32)
        row = lax.broadcasted_iota(jnp.int32, cur.shape, 0)
        before = hp_ref[HALO - 1:HALO, cols].astype(F32)
        after = hn_ref[0:1, cols].astype(F32)
        if seq_len >= rb:
            before = jnp.where((t0 & (seq_len - 1)) == 0, 0.0, before)
            after = jnp.where(((t0 + rb) & (seq_len - 1)) == 0, 0.0, after)
        prev = jnp.where(row == 0, before, pltpu.roll(cur, 1, 0))
        nxt = jnp.where(row == rb - 1, after, pltpu.roll(cur, rb - 1, 0))
        if seq_len < rb:
            pos = (row + t0) & (seq_len - 1)
            prev = jnp.where(pos == 0, 0.0, prev)
            nxt = jnp.where(pos == seq_len - 1, 0.0, nxt)
        cw = cw_ref[:, w_lo:w_lo + width]
        return _silu(prev * cw[0:1, :] + cur * cw[1:2, :] + nxt * cw[2:3, :])

    def l2n(x):
        return x * lax.rsqrt(jnp.sum(x * x, axis=-1, keepdims=True) + EPS)

    for h in range(HEADS):
        lo = h * DK
        q = l2n(conv(xqk_ref, pqk_ref, nqk_ref, lo, DK, lo)) * (DK ** -0.5)
        o_ref[:, lo:lo + DK] = q.astype(o_ref.dtype)
        lo = QK_W + h * DK
        o_ref[:, lo:lo + DK] = l2n(conv(xqk_ref, pqk_ref, nqk_ref, lo, DK, lo)).astype(o_ref.dtype)
    for h in range(HEADS):
        lo = h * DV
        v = conv(xv_ref, pv_ref, nv_ref, lo, DV, 2 * QK_W + lo)
        o_ref[:, 2 * QK_W + lo:2 * QK_W + lo + DV] = v.astype(o_ref.dtype)


def _gdn_prep(proj, conv_w, seq_len):
    t = proj.shape[0]
    rb = 1024
    assert seq_len & (seq_len - 1) == 0 and (rb % seq_len == 0 or seq_len % rb == 0)
    kern = functools.partial(_prep_kernel, seq_len=seq_len, rb=rb)
    halo_per_blk = rb // HALO
    n_halo = t // HALO
    width = 2 * QK_W + V_W

    def spec_x(col):
        return pl.BlockSpec((rb, V_W), lambda i: (i, col // V_W))

    def spec_prev(col):
        return pl.BlockSpec((HALO, V_W), lambda i: (jnp.maximum(i * halo_per_blk - 1, 0), col // V_W))

    def spec_next(col):
        return pl.BlockSpec((HALO, V_W),
                            lambda i: (jnp.minimum((i + 1) * halo_per_blk, n_halo - 1), col // V_W))

    in_specs = [spec(col) for spec in (spec_x, spec_prev, spec_next) for col in (COL_DQK, COL_DV)]
    in_specs.append(pl.BlockSpec((CONV_WIDTH, width), lambda i: (0, 0)))
    return pl.pallas_call(
        kern,
        grid=(t // rb,),
        in_specs=in_specs,
        out_specs=pl.BlockSpec((rb, width), lambda i: (i, 0)),
        out_shape=jax.ShapeDtypeStruct((t, width), BF16),
        compiler_params=_cparams(("arbitrary",)),
        name="gdn_prep",
    )(*([proj] * 6), conv_w)


def _gdn(qkv, gates, a_log, dt_bias, n_seq, seq_len, s0=None, emit_state=False):
    t = qkv.shape[0]
    nb = seq_len // BLK
    has_s0 = s0 is not None
    kern = functools.partial(_gdn_kernel, nb=nb, has_s0=has_s0, emit_state=emit_state)

    def fblk(b, s):
        return b * nb + s

    def bblk(b, s):
        return b * nb + nb - 1 - s

    def spec_x(blk):
        return pl.BlockSpec((BLK, qkv.shape[1]), lambda b, s: (blk(b, s), 0))

    def spec_g(blk):
        return pl.BlockSpec((2 * N_GATE, BLK), lambda b, s: (0, blk(b, s)))

    const2 = lambda b, s: (0, 0)
    in_specs = [spec_x(fblk), spec_x(bblk), spec_g(fblk), spec_g(bblk),
                pl.BlockSpec((N_GATE, 1), const2),
                pl.BlockSpec((N_GATE, 1), const2)]
    args = [qkv, qkv, gates, gates, a_log, dt_bias]
    state_spec = pl.BlockSpec((None, None, 2, HEADS, DK, DV), lambda b, s: (b, 0, 0, 0, 0, 0))
    if has_s0:
        in_specs.append(state_spec)
        args.append(s0)
    out_specs = [pl.BlockSpec((BLK, V_W), lambda b, s: (fblk(b, s), 0)),
                 pl.BlockSpec((BLK, V_W), lambda b, s: (bblk(b, s), 0))]
    out_shape = [jax.ShapeDtypeStruct((t, V_W), BF16), jax.ShapeDtypeStruct((t, V_W), BF16)]
    if emit_state:
        out_specs.append(state_spec)
        out_shape.append(jax.ShapeDtypeStruct((n_seq, 1, 2, HEADS, DK, DV), F32))
    scratch = [pltpu.VMEM((2, HEADS, DK, DV), F32),
               pltpu.VMEM((N_MASK, BLK, BLK), F32),
               pltpu.VMEM((BLK, BLK), BF16)]
    return pl.pallas_call(
        kern,
        grid=(n_seq, nb),
        in_specs=in_specs,
        out_specs=out_specs,
        out_shape=out_shape,
        scratch_shapes=scratch,
        compiler_params=_cparams(("arbitrary", "arbitrary")),
        name="gated_delta",
    )(*args)


def _merge_kernel(x_ref, or_ref, of_ref, ob_ref, rg_ref, dz_ref, gr_ref, gd_ref, mod_ref, nw_ref,
                  dnw_ref, wr_ref, wd_ref, wo_ref, o_ref, *, tm, rows_per_cond):
    i = pl.program_id(0)
    ci = (i * tm) // rows_per_cond
    m = mod_ref[pl.ds(ci, 1), :]
    g1 = m[:, 2 * D_MODEL:3 * D_MODEL]
    y_r = _mm(_silu(rg_ref[...].astype(F32)) * or_ref[...].astype(F32), wr_ref[...])
    dnw = dnw_ref[...]
    heads = []
    for h in range(HEADS):
        cols = slice(h * DV, (h + 1) * DV)
        od = _rms(of_ref[:, cols].astype(F32) + ob_ref[:, cols].astype(F32), dnw)
        heads.append((od * _silu(dz_ref[:, cols].astype(F32))).astype(BF16))
    y_d = jnp.dot(jnp.concatenate(heads, axis=1), wd_ref[...], preferred_element_type=F32)
    merged = _sigmoid(gr_ref[...].astype(F32)) * y_r + _sigmoid(gd_ref[...].astype(F32)) * y_d
    mo = _mm(merged, wo_ref[...])
    o_ref[...] = x_ref[...] + g1 * _rms(mo, nw_ref[1:2, :])


def _merge(x, o_r, o_f, o_b, proj, mod, norm_w, dn_norm_w, w_ret_o, w_dn_o, w_out, rows_per_cond):
    t = x.shape[0]
    tm = 512
    kern = functools.partial(_merge_kernel, tm=tm, rows_per_cond=rows_per_cond)
    row = lambda i: (i, 0)
    const = lambda i: (0, 0)
    wspec = pl.BlockSpec((D_MODEL, D_MODEL), const)
    return pl.pallas_call(
        kern,
        grid=(t // tm,),
        in_specs=[pl.BlockSpec((tm, D_MODEL), row),
                  pl.BlockSpec((tm, V_W), row),
                  pl.BlockSpec((tm, V_W), row),
                  pl.BlockSpec((tm, V_W), row),
                  pl.BlockSpec((tm, V_W), lambda i: (i, COL_RG // V_W)),
                  pl.BlockSpec((tm, V_W), lambda i: (i, COL_DZ // V_W)),
                  pl.BlockSpec((tm, D_MODEL), lambda i: (i, COL_GR // D_MODEL)),
                  pl.BlockSpec((tm, D_MODEL), lambda i: (i, COL_GD // D_MODEL)),
                  pl.BlockSpec(mod.shape, const),
                  pl.BlockSpec(norm_w.shape, const),
                  pl.BlockSpec(dn_norm_w.shape, const),
                  wspec, wspec, wspec],
        out_specs=pl.BlockSpec((tm, D_MODEL), row),
        out_shape=jax.ShapeDtypeStruct((t, D_MODEL), F32),
        compiler_params=_cparams(("arbitrary",)),
        name="merge_out",
    )(x, o_r, o_f, o_b, proj, proj, proj, proj, mod, norm_w, dn_norm_w, w_ret_o, w_dn_o, w_out)


def _ffn_kernel(x_ref, mod_ref, nw_ref, wi_ref, wo_ref, o_ref, *, tm, rows_per_cond):
    i = pl.program_id(0)
    ci = (i * tm) // rows_per_cond
    m = mod_ref[pl.ds(ci, 1), :]
    sh2 = m[:, 3 * D_MODEL:4 * D_MODEL]
    sc2 = m[:, 4 * D_MODEL:5 * D_MODEL]
    g2 = m[:, 5 * D_MODEL:6 * D_MODEL]
    x = x_ref[...]
    hb = (_rms(x, nw_ref[2:3, :]) * (1.0 + sc2) + sh2).astype(BF16)
    f = None
    for c in range(D_FF // FF_CHUNK):
        lo = c * FF_CHUNK
        gate = jnp.dot(hb, wi_ref[:, lo:lo + FF_CHUNK], preferred_element_type=F32)
        up = jnp.dot(hb, wi_ref[:, D_FF + lo:D_FF + lo + FF_CHUNK], preferred_element_type=F32)
        part = _mm(_silu(gate) * up, wo_ref[lo:lo + FF_CHUNK, :])
        f = part if f is None else f + part
    o_ref[...] = x + g2 * _rms(f, nw_ref[3:4, :])


def _ffn(x, mod, norm_w, w_in, w_out, rows_per_cond):
    t = x.shape[0]
    tm = 512
    kern = functools.partial(_ffn_kernel, tm=tm, rows_per_cond=rows_per_cond)
    row = lambda i: (i, 0)
    const = lambda i: (0, 0)
    resident = pl.Buffered(1)
    return pl.pallas_call(
        kern,
        grid=(t // tm,),
        in_specs=[pl.BlockSpec((tm, D_MODEL), row),
                  pl.BlockSpec(mod.shape, const),
                  pl.BlockSpec(norm_w.shape, const),
                  pl.BlockSpec((D_MODEL, 2 * D_FF), const, pipeline_mode=resident),
                  pl.BlockSpec((D_FF, D_MODEL), const, pipeline_mode=resident)],
        out_specs=pl.BlockSpec((tm, D_MODEL), row),
        out_shape=jax.ShapeDtypeStruct((t, D_MODEL), F32),
        compiler_params=_cparams(("arbitrary",)),
        name="swiglu",
    )(x, mod, norm_w, w_in, w_out)


def _rope_tables(seq_len):
    rows = seq_len // GRID_W
    row_idx = jnp.repeat(jnp.arange(rows, dtype=F32), GRID_W)
    col_idx = (jnp.arange(seq_len) % GRID_W).astype(F32)
    n_freq = DK // 4
    freqs = ROPE_BASE ** (-jnp.arange(n_freq, dtype=F32) / n_freq)
    ang = jnp.concatenate([row_idx[:, None] * freqs, col_idx[:, None] * freqs], axis=-1)
    cos = jnp.repeat(jnp.cos(ang), 2, axis=-1)
    sin = jnp.repeat(jnp.sin(ang), 2, axis=-1)
    sign = jnp.tile(jnp.array([-1.0, 1.0], F32), DK // 2)
    return cos, sin * sign


def _one_path(x, mod, seq_len, n_seq, wts, rope_tabs, s_ret0, s_dn0, emit_state):
    rows_per_cond = seq_len if mod.shape[0] > 1 else x.shape[0]
    proj, gates = _inproj(x, mod, wts["norm_w"], wts["w_a"], wts["w_b"], wts["w_gate_t"],
                          rows_per_cond)
    ret = _retention(proj, wts["logit_rep"], wts["ret_gn_w"], n_seq, seq_len,
                     rope_tabs=rope_tabs, s0=s_ret0, emit_state=emit_state)
    qkv = _gdn_prep(proj, wts["conv_w"], seq_len)
    gdn = _gdn(qkv, gates, wts["a_log"], wts["dt_bias"], n_seq, seq_len,
               s0=s_dn0, emit_state=emit_state)
    x1 = _merge(x, ret[0], gdn[0], gdn[1], proj, mod, wts["norm_w"], wts["dn_norm_w"],
                wts["w_ret_o"], wts["w_dn_o"], wts["w_out"], rows_per_cond)
    y = _ffn(x1, mod, wts["norm_w"], wts["w_ffn_in"], wts["w_ffn_out"], rows_per_cond)
    if emit_state:
        return y, ret[1], gdn[2]
    return y, None, None


def kernel(x_prompt, x_sample, c, state_ret, state_dn, c_ctx, w_mod, b_mod, norm_w, w_in, conv_w,
           ret_decay_logit, ret_gn_w, dn_a_log, dn_dt_bias, dn_norm_w, w_ret_o, w_dn_o, w_out,
           w_ffn_in, w_ffn_out):
    n_ctx, l_ctx, _ = x_prompt.shape
    n_lat, l_lat, _ = x_sample.shape
    assert w_mod.shape[0] == 1, "single-layer kernel"

    cond8 = jnp.zeros((8, D_MODEL), F32).at[0].set(c_ctx).at[1:1 + n_lat].set(c)
    mod = _modulation(cond8, w_mod[0], b_mod)
    mod_ctx, mod_lat = mod[0:1], mod[1:1 + n_lat]

    wi = w_in[0]
    c_gr = COL_GATE + 2 * N_GATE
    wts = {
        "norm_w": norm_w[0],
        "w_a": wi[:, :COL_GATE].astype(BF16),
        "w_b": wi[:, c_gr:].astype(BF16),
        "w_gate_t": wi[:, COL_GATE:c_gr].T,
        "logit_rep": jnp.broadcast_to(ret_decay_logit[0].T[:, :, None], (HEADS, 2, DK)),
        "ret_gn_w": ret_gn_w,
        "conv_w": conv_w[0],
        "a_log": dn_a_log.reshape(N_GATE, 1),
        "dt_bias": dn_dt_bias.reshape(N_GATE, 1),
        "dn_norm_w": dn_norm_w,
        "w_ret_o": w_ret_o[0].astype(BF16),
        "w_dn_o": w_dn_o[0].astype(BF16),
        "w_out": w_out[0].astype(BF16),
        "w_ffn_in": w_ffn_in[0].astype(BF16),
        "w_ffn_out": w_ffn_out[0].astype(BF16),
    }

    y_p, s_ret, s_dn = _one_path(x_prompt.reshape(n_ctx * l_ctx, D_MODEL), mod_ctx, l_ctx, n_ctx,
                                 wts, None, None, None, True)
    y_s, _, _ = _one_path(x_sample.reshape(n_lat * l_lat, D_MODEL), mod_lat, l_lat, n_lat,
                          wts, _rope_tables(l_lat), state_ret, state_dn, False)
    return (y_p.reshape(x_prompt.shape), y_s.reshape(x_sample.shape), s_ret, s_dn)
```

```python
import functools

import jax
import jax.numpy as jnp
from jax import lax
from jax.experimental import pallas as pl
from jax.experimental.pallas import tpu as pltpu

F32 = jnp.float32
BF16 = jnp.bfloat16

D_MODEL = 1024
HEADS = 4
DK = 128
DV = 256
QK_W = HEADS * DK
V_W = HEADS * DV
GDN_CHUNK = 64
TRI_BASE = 8
BLK = 256
SUB = 8
LANES = 128
HALO = 16
CONV_WIDTH = 3
GRID_W = 64
ROPE_BASE = 10000.0
EPS = 1e-6
D_FF = 2816
MXU_DIM = 256
FF_CHUNKS = (6 * MXU_DIM, 5 * MXU_DIM)
assert sum(FF_CHUNKS) == D_FF
N_GATE = 2 * HEADS

COL_RQ, COL_RK, COL_RV, COL_RG = 0, 512, 1024, 2048
COL_DQK, COL_DV, COL_DZ = 3072, 4096, 5120
COL_GATE = 6144
COL_GR, COL_GD = 6144, 7168
PROJ_W = 8192
PROJ_TN = 2048

VMEM_LIMIT = 56 * 1024 * 1024


def _cparams(sem):
    return pltpu.CompilerParams(dimension_semantics=sem, vmem_limit_bytes=VMEM_LIMIT)


def _mm(a, b):
    return jnp.dot(a.astype(BF16), b.astype(BF16), preferred_element_type=F32)


def _mm_nt(a, b):
    return lax.dot_general(a.astype(BF16), b.astype(BF16), (((1,), (1,)), ((), ())),
                           preferred_element_type=F32)


def _mm_tn(a, b):
    return lax.dot_general(a.astype(BF16), b.astype(BF16), (((0,), (0,)), ((), ())),
                           preferred_element_type=F32)


def _sigmoid(x):
    return 1.0 / (1.0 + jnp.exp(-x))


def _silu(x):
    return x * _sigmoid(x)


def _softplus(x):
    return jnp.maximum(x, 0.0) + jnp.log(1.0 + jnp.exp(-jnp.abs(x)))


def _log_sigmoid(x):
    return -_softplus(-x)


def _rms(x, w):
    return x * lax.rsqrt(jnp.mean(x * x, axis=-1, keepdims=True) + EPS) * w


def _mod_kernel(c_ref, w_ref, b_ref, o_ref):
    o_ref[...] = _mm(_silu(c_ref[...]), w_ref[...]) + b_ref[...]


def _modulation(cond8, w_mod, b_mod):
    n = w_mod.shape[1]
    tn = 1536
    return pl.pallas_call(
        _mod_kernel,
        grid=(n // tn,),
        in_specs=[pl.BlockSpec((8, D_MODEL), lambda j: (0, 0)),
                  pl.BlockSpec((D_MODEL, tn), lambda j: (0, j)),
                  pl.BlockSpec((1, tn), lambda j: (0, j))],
        out_specs=pl.BlockSpec((8, tn), lambda j: (0, j)),
        out_shape=jax.ShapeDtypeStruct((8, n), F32),
        compiler_params=_cparams(("arbitrary",)),
        name="modulation",
    )(cond8, w_mod, b_mod)


def _inproj_kernel(x_ref, mod_ref, nw_ref, wa_ref, wb_ref, wg_ref, o_ref, g_ref, h_sc, *, tm,
                   rows_per_cond, n_a):
    i = pl.program_id(0)
    j = pl.program_id(1)

    @pl.when(j == 0)
    def _():
        ci = (i * tm) // rows_per_cond
        m = mod_ref[pl.ds(ci, 1), :]
        sh1 = m[:, 0:D_MODEL]
        sc1 = m[:, D_MODEL:2 * D_MODEL]
        h = _rms(x_ref[...], nw_ref[0:1, :]) * (1.0 + sc1) + sh1
        hb = h.astype(BF16)
        h_sc[...] = hb
        g_ref[...] = lax.dot_general(wg_ref[...].astype(BF16), hb, (((1,), (1,)), ((), ())),
                                     preferred_element_type=F32)

    @pl.when(j < n_a)
    def _():
        o_ref[...] = jnp.dot(h_sc[...], wa_ref[...].astype(BF16),
                             preferred_element_type=F32).astype(o_ref.dtype)

    @pl.when(j >= n_a)
    def _():
        o_ref[...] = jnp.dot(h_sc[...], wb_ref[...], preferred_element_type=F32).astype(o_ref.dtype)


def _tail_kernel(a_ref, b_ref, o_ref, *, shift):
    n = a_ref.shape[1]
    body = pltpu.roll(a_ref[...], n - shift, 1)
    tail = pltpu.roll(b_ref[...], LANES - shift, 1)
    lane = lax.broadcasted_iota(jnp.int32, tail.shape, 1)
    o_ref[:, 0:n - LANES] = body[:, 0:n - LANES].astype(o_ref.dtype)
    o_ref[:, n - LANES:n] = jnp.where(lane >= LANES - shift, tail, body[:, n - LANES:n]).astype(o_ref.dtype)


def _weight_tail(w_in, start, shift, width):
    rows = w_in.shape[1]
    tr = 256
    assert start % width == 0 and (start + width) % LANES == 0 and 0 < shift < LANES
    return pl.pallas_call(
        functools.partial(_tail_kernel, shift=shift),
        grid=(rows // tr,),
        in_specs=[pl.BlockSpec((None, tr, width), lambda i: (0, i, start // width)),
                  pl.BlockSpec((None, tr, LANES), lambda i: (0, i, (start + width) // LANES))],
        out_specs=pl.BlockSpec((tr, width), lambda i: (i, 0)),
        out_shape=jax.ShapeDtypeStruct((rows, width), BF16),
        compiler_params=_cparams(("arbitrary",)),
        name="weight_tail",
    )(w_in, w_in)


def _inproj(x, mod, norm_w, w_in, w_b, w_gate_t, rows_per_cond):
    t = x.shape[0]
    tm, tn = 1024, PROJ_TN
    n_a = COL_GATE // tn
    kern = functools.partial(_inproj_kernel, tm=tm, rows_per_cond=rows_per_cond, n_a=n_a)
    return pl.pallas_call(
        kern,
        grid=(t // tm, PROJ_W // tn),
        in_specs=[pl.BlockSpec((tm, D_MODEL), lambda i, j: (i, 0)),
                  pl.BlockSpec(mod.shape, lambda i, j: (0, 0)),
                  pl.BlockSpec(norm_w.shape, lambda i, j: (0, 0)),
                  pl.BlockSpec((None, D_MODEL, tn), lambda i, j: (0, 0, jnp.minimum(j, n_a - 1))),
                  pl.BlockSpec((D_MODEL, tn), lambda i, j: (0, jnp.maximum(j - n_a, 0))),
                  pl.BlockSpec((2 * N_GATE, D_MODEL), lambda i, j: (0, 0))],
        out_specs=[pl.BlockSpec((tm, tn), lambda i, j: (i, j)),
                   pl.BlockSpec((2 * N_GATE, tm), lambda i, j: (0, i))],
        out_shape=[jax.ShapeDtypeStruct((t, PROJ_W), BF16),
                   jax.ShapeDtypeStruct((2 * N_GATE, t), F32)],
        scratch_shapes=[pltpu.VMEM((tm, D_MODEL), BF16)],
        compiler_params=_cparams(("arbitrary", "arbitrary")),
        name="inproj",
    )(x, mod, norm_w, w_in, w_b, w_gate_t)


RET_GROUP = 8


def _pair_swap(x):
    lane = lax.broadcasted_iota(jnp.int32, x.shape, 1)
    nxt = pltpu.roll(x, x.shape[1] - 1, 1)
    prv = pltpu.roll(x, 1, 1)
    return jnp.where((lane & 1) == 0, nxt, prv)


def _ret_kernel(*refs, seq_len, rope, has_s0, emit_state):
    it = iter(refs)
    q_ref, k_ref, v_ref = next(it), next(it), next(it)
    cos_ref = next(it) if rope else None
    sin_ref = next(it) if rope else None
    logit_ref, gnw_ref = next(it), next(it)
    s0_ref = next(it) if has_s0 else None
    o_ref = next(it)
    st_ref = next(it) if emit_state else None
    q_sc, k_sc, st_sc = next(it), next(it), next(it)

    nb = seq_len // BLK
    cross = has_s0 or nb > 1

    def rows(c):
        return slice(c * BLK, (c + 1) * BLK)

    def qk_cols(h):
        return slice(h * DK, (h + 1) * DK)

    def v_cols(h):
        return slice(h * DV, (h + 1) * DV)

    for h in range(HEADS):
        q = q_ref[:, qk_cols(h)].astype(F32) * (DK ** -0.5)
        k = k_ref[:, qk_cols(h)].astype(F32)
        if rope:
            cos = cos_ref[...]
            sin = sin_ref[...]
            q = q * cos + _pair_swap(q) * sin
            k = k * cos + _pair_swap(k) * sin
        q_sc[:, qk_cols(h)] = q
        k_sc[:, qk_cols(h)] = k

    pos = lax.broadcasted_iota(jnp.int32, (BLK, DK), 0).astype(F32)

    def log_gammas(h):
        lg = _log_sigmoid(logit_ref[h])
        return lg[0:1, :], lg[1:2, :]

    for h in range(HEADS):
        lgf, lgb = log_gammas(h)
        kdec_f = jnp.exp(lgf * (BLK - 1.0 - pos))
        kdec_b = jnp.exp(lgb * pos)
        cdec_f = jnp.exp(lgf[:, 0:1] * float(BLK))
        cdec_b = jnp.exp(lgb[:, 0:1] * float(BLK))
        af = {c: _mm_tn(k_sc[rows(c), qk_cols(h)] * kdec_f, v_ref[rows(c), v_cols(h)])
              for c in range(nb) if c < nb - 1 or emit_state}
        ab = {c: _mm_tn(k_sc[rows(c), qk_cols(h)] * kdec_b, v_ref[rows(c), v_cols(h)])
              for c in range(nb) if c > 0 or emit_state}
        s = s0_ref[0, h] if has_s0 else jnp.zeros((DK, DV), F32)
        for c in range(nb):
            if cross:
                st_sc[h, c, 0:DK, :] = s.astype(BF16)
            if c in af:
                s = s * cdec_f + af[c]
        if emit_state:
            st_ref[0, h] = s
        s = s0_ref[1, h] if has_s0 else jnp.zeros((DK, DV), F32)
        for c in range(nb - 1, -1, -1):
            if cross:
                st_sc[h, c, DK:2 * DK, :] = s.astype(BF16)
            if c in ab:
                s = s * cdec_b + ab[c]
        if emit_state:
            st_ref[1, h] = s

    ii = lax.broadcasted_iota(jnp.int32, (BLK, BLK), 0)
    jj = lax.broadcasted_iota(jnp.int32, (BLK, BLK), 1)
    diff = (ii - jj).astype(F32)
    pos2 = lax.broadcasted_iota(jnp.int32, (BLK, 2 * DK), 0).astype(F32)
    first_half = lax.broadcasted_iota(jnp.int32, (BLK, 2 * DK), 1) < DK

    def head_consts(h):
        lgf, lgb = log_gammas(h)
        lgf1, lgb1 = lgf[:, 0:1], lgb[:, 0:1]
        dmat = (jnp.where(diff >= 0, jnp.exp(lgf1 * jnp.maximum(diff, 0.0)), 0.0)
                + jnp.where(diff <= 0, jnp.exp(lgb1 * jnp.maximum(-diff, 0.0)), 0.0))
        qdec = jnp.where(first_half, jnp.exp(lgf1 * (pos2 + 1.0)), jnp.exp(lgb1 * (BLK - pos2)))
        return dmat, qdec

    def unit(h, c, consts):
        dmat, qdec = consts
        qc = q_sc[rows(c), qk_cols(h)]
        s = _mm_nt(qc, k_sc[rows(c), qk_cols(h)])
        yield
        o = _mm(s * dmat, v_ref[rows(c), v_cols(h)])
        if cross:
            o_cross = _mm(jnp.concatenate([qc, qc], axis=1) * qdec, st_sc[h, c])
        yield
        if cross:
            o = o + o_cross
        mu = jnp.mean(o, axis=-1, keepdims=True)
        oc = o - mu
        var = jnp.mean(oc * oc, axis=-1, keepdims=True)
        o_ref[rows(c), v_cols(h)] = (oc * lax.rsqrt(var + EPS) * gnw_ref[:, v_cols(h)]).astype(o_ref.dtype)
        yield

    units = [(h, c) for h in range(HEADS) for c in range(nb)]
    for g0 in range(0, len(units), RET_GROUP):
        consts = {}
        gens = []
        for h, c in units[g0:g0 + RET_GROUP]:
            if h not in consts:
                consts[h] = head_consts(h)
            gens.append(unit(h, c, consts[h]))
        for _ in zip(*gens):
            pass


def _retention(proj, logit_rep, gn_w, n_seq, seq_len, rope_tabs=None, s0=None, emit_state=False):
    t = proj.shape[0]
    rope = rope_tabs is not None
    has_s0 = s0 is not None
    kern = functools.partial(_ret_kernel, seq_len=seq_len, rope=rope, has_s0=has_s0,
                             emit_state=emit_state)
    in_specs = [pl.BlockSpec((seq_len, QK_W), lambda b: (b, COL_RQ // QK_W)),
                pl.BlockSpec((seq_len, QK_W), lambda b: (b, COL_RK // QK_W)),
                pl.BlockSpec((seq_len, V_W), lambda b: (b, COL_RV // V_W))]
    args = [proj, proj, proj]
    if rope:
        in_specs += [pl.BlockSpec((seq_len, DK), lambda b: (0, 0))] * 2
        args += list(rope_tabs)
    in_specs += [pl.BlockSpec((HEADS, 2, DK), lambda b: (0, 0, 0)),
                 pl.BlockSpec((1, V_W), lambda b: (0, 0))]
    args += [logit_rep, gn_w]
    state_spec = pl.BlockSpec((None, None, 2, HEADS, DK, DV), lambda b: (b, 0, 0, 0, 0, 0))
    if has_s0:
        in_specs.append(state_spec)
        args.append(s0)
    out_specs = [pl.BlockSpec((seq_len, V_W), lambda b: (b, 0))]
    out_shape = [jax.ShapeDtypeStruct((t, V_W), BF16)]
    if emit_state:
        out_specs.append(state_spec)
        out_shape.append(jax.ShapeDtypeStruct((n_seq, 1, 2, HEADS, DK, DV), F32))
    nb = seq_len // BLK
    return pl.pallas_call(
        kern,
        grid=(n_seq,),
        in_specs=in_specs,
        out_specs=out_specs,
        out_shape=out_shape,
        scratch_shapes=[pltpu.VMEM((seq_len, QK_W), F32), pltpu.VMEM((seq_len, QK_W), F32),
                        pltpu.VMEM((HEADS, nb, 2 * DK, DV), BF16)],
        compiler_params=_cparams(("arbitrary",)),
        name="retention",
    )(*args)


N_MASK = 8
(MASK_EQ_F, MASK_EQ_B, MASK_OFFDIAG, MASK_EYE, MASK_BASE, MASK_PAIR0) = 0, 1, 2, 3, 4, 5


def _build_masks(mask_sc, eye_sc):
    ii = lax.broadcasted_iota(jnp.int32, (BLK, BLK), 0)
    jj = lax.broadcasted_iota(jnp.int32, (BLK, BLK), 1)

    def blk(v, size):
        return lax.shift_right_logical(v, size.bit_length() - 1)

    same = blk(ii, GDN_CHUNK) == blk(jj, GDN_CHUNK)
    mask_sc[MASK_EQ_F] = (same & (jj <= ii)).astype(F32)
    mask_sc[MASK_EQ_B] = (same & (jj >= ii)).astype(F32)
    mask_sc[MASK_OFFDIAG] = (ii != jj).astype(F32)
    mask_sc[MASK_EYE] = (ii == jj).astype(F32)
    mask_sc[MASK_BASE] = (blk(ii, TRI_BASE) == blk(jj, TRI_BASE)).astype(F32)
    size, idx = TRI_BASE, MASK_PAIR0
    while size < GDN_CHUNK:
        pair = (blk(ii, 2 * size) == blk(jj, 2 * size)) & (blk(ii, size) != blk(jj, size))
        mask_sc[idx] = pair.astype(F32)
        size, idx = 2 * size, idx + 1
    eye_sc[...] = (ii == jj).astype(BF16)


def _seg_cumsums(x):
    n = x.shape[1]
    pos = lax.broadcasted_iota(jnp.int32, x.shape, 1) & (GDN_CHUNK - 1)
    up, down = x, x
    s = 1
    while s < GDN_CHUNK:
        up = up + jnp.where(pos >= s, pltpu.roll(up, s, 1), 0.0)
        down = down + jnp.where(pos < GDN_CHUNK - s, pltpu.roll(down, n - s, 1), 0.0)
        s *= 2
    return up, down


def _rows_to_cols(rows, eye_sc):
    p1 = rows.astype(BF16).astype(F32)
    r1 = rows - p1
    p2 = r1.astype(BF16).astype(F32)
    p3 = r1 - p2
    pieces = jnp.concatenate([p1, p2, p3], axis=0).astype(BF16)
    c3 = lax.dot_general(eye_sc[...], pieces, (((1,), (1,)), ((), ())), preferred_element_type=F32)
    return c3[:, 0:SUB] + c3[:, SUB:2 * SUB] + c3[:, 2 * SUB:3 * SUB]


def _gdn_kernel(*refs, nb, has_s0, emit_state):
    it = iter(refs)
    xs_refs = (next(it), next(it))
    g_refs = (next(it), next(it))
    alog_ref, dtb_ref = next(it), next(it)
    s0_ref = next(it) if has_s0 else None
    o_refs = (next(it), next(it))
    st_ref = next(it) if emit_state else None
    st_sc, mask_sc, eye_sc = next(it), next(it), next(it)

    b = pl.program_id(0)
    step = pl.program_id(1)

    @pl.when((b == 0) & (step == 0))
    def _():
        _build_masks(mask_sc, eye_sc)

    @pl.when(step == 0)
    def _():
        for d in range(2):
            for h in range(HEADS):
                st_sc[d, h] = s0_ref[d, h] if has_s0 else jnp.zeros((DK, DV), F32)

    blk_of = (step, nb - 1 - step)

    row8 = lax.broadcasted_iota(jnp.int32, (N_GATE, BLK), 0)
    fwd_rows = row8 < HEADS
    logit_b = jnp.where(fwd_rows, g_refs[0][0:N_GATE, :], g_refs[1][0:N_GATE, :])
    logit_a = jnp.where(fwd_rows, g_refs[0][N_GATE:2 * N_GATE, :], g_refs[1][N_GATE:2 * N_GATE, :])
    beta = _sigmoid(logit_b)
    la = -jnp.exp(alog_ref[...]) * _softplus(logit_a + dtb_ref[...])
    up, down = _seg_cumsums(la)
    g8 = jnp.where(fwd_rows, up, down)
    r8 = jnp.where(fwd_rows, down, up) - la
    eg8 = jnp.exp(g8)
    erb8 = jnp.exp(r8) * beta
    etot8 = jnp.exp(up + down - la)
    gcols = _rows_to_cols(g8, eye_sc)

    nck = BLK // GDN_CHUNK

    def chain(d, h):
        c = d * HEADS + h
        xs_ref = xs_refs[d]
        q16 = xs_ref[:, h * DK:(h + 1) * DK]
        k16 = xs_ref[:, QK_W + h * DK:QK_W + (h + 1) * DK]
        v16 = xs_ref[:, 2 * QK_W + h * DV:2 * QK_W + (h + 1) * DV]
        q = q16.astype(F32)
        k = k16.astype(F32)
        kk = _mm_nt(k16, k16)
        qk = _mm_nt(q16, k16)
        yield
        g_col = gcols[:, c:c + 1]
        g_row = g8[c:c + 1, :]
        m_eq = mask_sc[MASK_EQ_F if d == 0 else MASK_EQ_B]
        decay_beta = (jnp.exp(jnp.minimum(g_col - g_row, 0.0)) * m_eq) * beta[c:c + 1, :]
        lp = kk * decay_beta * mask_sc[MASK_OFFDIAG]
        sc16 = (qk * decay_beta).astype(BF16)
        qg16 = (q * jnp.exp(g_col)).astype(BF16)
        kt16 = (k.T * erb8[c:c + 1, :]).astype(BF16)

        x = -(lp * mask_sc[MASK_BASE])
        t = mask_sc[MASK_EYE] + x
        for _ in range(TRI_BASE.bit_length() - 2):
            x = _mm(x, x)
            yield
            t = t + _mm(t, x)
            yield
        size, idx = TRI_BASE, MASK_PAIR0
        while size < GDN_CHUNK:
            groups = [slice(r * size, (r + 1) * size) for r in range(BLK // size)]
            late = [(r % 2 == 1) == (d == 0) for r in range(BLK // size)]
            e_late = jnp.concatenate([lp[g, :] * mask_sc[idx, g, :] for g, on in zip(groups, late) if on],
                                     axis=0)
            et = _mm(e_late, t)
            yield
            t_late = jnp.concatenate([t[g, :] for g, on in zip(groups, late) if on], axis=0)
            et_rows = iter(et[j * size:(j + 1) * size, :] for j in range(BLK // size // 2))
            et_full = jnp.concatenate([next(et_rows) if on else jnp.zeros((size, BLK), F32)
                                       for on in late], axis=0)
            t_late = t_late - _mm(t_late, et_full)
            yield
            new_rows = iter(t_late[j * size:(j + 1) * size, :] for j in range(BLK // size // 2))
            t = jnp.concatenate([next(new_rows) if on else t[g, :] for g, on in zip(groups, late)],
                                axis=0)
            size, idx = 2 * size, idx + 1
        u = _mm(t, v16)
        w16 = _mm(t * eg8[c:c + 1, :], k16).astype(BF16)
        yield

        order = range(nck) if d == 0 else range(nck - 1, -1, -1)
        for cc in order:
            rs = slice(cc * GDN_CHUNK, (cc + 1) * GDN_CHUNK)
            s_prev = st_sc[d, h]
            wq = jnp.concatenate([w16[rs, :], qg16[rs, :]], axis=0)
            ws_qs = _mm(wq, s_prev)
            yield
            v_new = (u[rs, :] - ws_qs[0:GDN_CHUNK]).astype(BF16)
            o = ws_qs[GDN_CHUNK:] + _mm(sc16[rs, rs], v_new)
            decay_all = etot8[c:c + 1, cc * GDN_CHUNK:cc * GDN_CHUNK + 1]
            st_sc[d, h] = s_prev * decay_all + _mm(kt16[:, rs], v_new)
            o_refs[d][rs, h * DV:(h + 1) * DV] = o.astype(o_refs[d].dtype)
            yield

    chains = [chain(d, h) for d in range(2) for h in range(HEADS)]
    for _ in zip(*chains):
        pass

    if emit_state:
        @pl.when(step == nb - 1)
        def _():
            for d in range(2):
                for h in range(HEADS):
                    st_ref[d, h] = st_sc[d, h]


def _prep_kernel(xqk_ref, xv_ref, pqk_ref, pv_ref, nqk_ref, nv_ref, cw_ref, o_ref, *, seq_len, rb):
    t0 = pl.program_id(0) * rb

    def conv(x_ref, hp_ref, hn_ref, lo, width, w_lo):
        cols = slice(lo, lo + width)
        cur = x_ref[:, cols].astype(F32)
        row = lax.broadcasted_iota(jnp.int32, cur.shape, 0)
        before = hp_ref[HALO - 1:HALO, cols].astype(F32)
        after = hn_ref[0:1, cols].astype(F32)
        if seq_len >= rb:
            before = jnp.where((t0 & (seq_len - 1)) == 0, 0.0, before)
            after = jnp.where(((t0 + rb) & (seq_len - 1)) == 0, 0.0, after)
        prev = jnp.where(row == 0, before, pltpu.roll(cur, 1, 0))
        nxt = jnp.where(row == rb - 1, after, pltpu.roll(cur, rb - 1, 0))
        if seq_len < rb:
            pos = (row + t0) & (seq_len - 1)
            prev = jnp.where(pos == 0, 0.0, prev)
            nxt = jnp.where(pos == seq_len - 1, 0.0, nxt)
        cw = cw_ref[:, w_lo:w_lo + width]
        return _silu(prev * cw[0:1, :] + cur * cw[1:2, :] + nxt * cw[2:3, :])

    def l2n(x):
        return x * lax.rsqrt(jnp.sum(x * x, axis=-1, keepdims=True) + EPS)

    for h in range(HEADS):
        lo = h * DK
        q = l2n(conv(xqk_ref, pqk_ref, nqk_ref, lo, DK, lo)) * (DK ** -0.5)
        o_ref[:, lo:lo + DK] = q.astype(o_ref.dtype)
        lo = QK_W + h * DK
        o_ref[:, lo:lo + DK] = l2n(conv(xqk_ref, pqk_ref, nqk_ref, lo, DK, lo)).astype(o_ref.dtype)
    for h in range(HEADS):
        lo = h * DV
        v = conv(xv_ref, pv_ref, nv_ref, lo, DV, 2 * QK_W + lo)
        o_ref[:, 2 * QK_W + lo:2 * QK_W + lo + DV] = v.astype(o_ref.dtype)


def _gdn_prep(proj, conv_w, seq_len):
    t = proj.shape[0]
    rb = 1024
    assert seq_len & (seq_len - 1) == 0 and (rb % seq_len == 0 or seq_len % rb == 0)
    kern = functools.partial(_prep_kernel, seq_len=seq_len, rb=rb)
    halo_per_blk = rb // HALO
    n_halo = t // HALO
    width = 2 * QK_W + V_W

    def spec_x(col):
        return pl.BlockSpec((rb, V_W), lambda i: (i, col // V_W))

    def spec_prev(col):
        return pl.BlockSpec((HALO, V_W), lambda i: (jnp.maximum(i * halo_per_blk - 1, 0), col // V_W))

    def spec_next(col):
        return pl.BlockSpec((HALO, V_W),
                            lambda i: (jnp.minimum((i + 1) * halo_per_blk, n_halo - 1), col // V_W))

    in_specs = [spec(col) for spec in (spec_x, spec_prev, spec_next) for col in (COL_DQK, COL_DV)]
    in_specs.append(pl.BlockSpec((CONV_WIDTH, width), lambda i: (0, 0)))
    return pl.pallas_call(
        kern,
        grid=(t // rb,),
        in_specs=in_specs,
        out_specs=pl.BlockSpec((rb, width), lambda i: (i, 0)),
        out_shape=jax.ShapeDtypeStruct((t, width), BF16),
        compiler_params=_cparams(("arbitrary",)),
        name="gdn_prep",
    )(*([proj] * 6), conv_w)


def _gdn(qkv, gates, a_log, dt_bias, n_seq, seq_len, s0=None, emit_state=False):
    t = qkv.shape[0]
    nb = seq_len // BLK
    has_s0 = s0 is not None
    kern = functools.partial(_gdn_kernel, nb=nb, has_s0=has_s0, emit_state=emit_state)

    def fblk(b, s):
        return b * nb + s

    def bblk(b, s):
        return b * nb + nb - 1 - s

    def spec_x(blk):
        return pl.BlockSpec((BLK, qkv.shape[1]), lambda b, s: (blk(b, s), 0))

    def spec_g(blk):
        return pl.BlockSpec((2 * N_GATE, BLK), lambda b, s: (0, blk(b, s)))

    const2 = lambda b, s: (0, 0)
    in_specs = [spec_x(fblk), spec_x(bblk), spec_g(fblk), spec_g(bblk),
                pl.BlockSpec((N_GATE, 1), const2),
                pl.BlockSpec((N_GATE, 1), const2)]
    args = [qkv, qkv, gates, gates, a_log, dt_bias]
    state_spec = pl.BlockSpec((None, None, 2, HEADS, DK, DV), lambda b, s: (b, 0, 0, 0, 0, 0))
    if has_s0:
        in_specs.append(state_spec)
        args.append(s0)
    out_specs = [pl.BlockSpec((BLK, V_W), lambda b, s: (fblk(b, s), 0)),
                 pl.BlockSpec((BLK, V_W), lambda b, s: (bblk(b, s), 0))]
    out_shape = [jax.ShapeDtypeStruct((t, V_W), BF16), jax.ShapeDtypeStruct((t, V_W), BF16)]
    if emit_state:
        out_specs.append(state_spec)
        out_shape.append(jax.ShapeDtypeStruct((n_seq, 1, 2, HEADS, DK, DV), F32))
    scratch = [pltpu.VMEM((2, HEADS, DK, DV), F32),
               pltpu.VMEM((N_MASK, BLK, BLK), F32),
               pltpu.VMEM((BLK, BLK), BF16)]
    return pl.pallas_call(
        kern,
        grid=(n_seq, nb),
        in_specs=in_specs,
        out_specs=out_specs,
        out_shape=out_shape,
        scratch_shapes=scratch,
        compiler_params=_cparams(("arbitrary", "arbitrary")),
        name="gated_delta",
    )(*args)


def _merge_kernel(x_ref, or_ref, of_ref, ob_ref, rg_ref, dz_ref, gr_ref, gd_ref, mod_ref, nw_ref,
                  dnw_ref, wr_ref, wd_ref, wo_ref, o_ref, *, tm, rows_per_cond):
    i = pl.program_id(0)
    ci = (i * tm) // rows_per_cond
    m = mod_ref[pl.ds(ci, 1), :]
    g1 = m[:, 2 * D_MODEL:3 * D_MODEL]
    y_r = _mm(_silu(rg_ref[...].astype(F32)) * or_ref[...].astype(F32), wr_ref[...])
    dnw = dnw_ref[...]
    heads = []
    for h in range(HEADS):
        cols = slice(h * DV, (h + 1) * DV)
        od = _rms(of_ref[:, cols].astype(F32) + ob_ref[:, cols].astype(F32), dnw)
        heads.append((od * _silu(dz_ref[:, cols].astype(F32))).astype(BF16))
    y_d = jnp.dot(jnp.concatenate(heads, axis=1), wd_ref[...], preferred_element_type=F32)
    merged = _sigmoid(gr_ref[...].astype(F32)) * y_r + _sigmoid(gd_ref[...].astype(F32)) * y_d
    mo = _mm(merged, wo_ref[...])
    o_ref[...] = x_ref[...] + g1 * _rms(mo, nw_ref[1:2, :])


def _merge(x, o_r, o_f, o_b, proj, mod, norm_w, dn_norm_w, w_ret_o, w_dn_o, w_out, rows_per_cond):
    t = x.shape[0]
    tm = 512
    kern = functools.partial(_merge_kernel, tm=tm, rows_per_cond=rows_per_cond)
    row = lambda i: (i, 0)
    const = lambda i: (0, 0)
    wspec = pl.BlockSpec((D_MODEL, D_MODEL), const)
    return pl.pallas_call(
        kern,
        grid=(t // tm,),
        in_specs=[pl.BlockSpec((tm, D_MODEL), row),
                  pl.BlockSpec((tm, V_W), row),
                  pl.BlockSpec((tm, V_W), row),
                  pl.BlockSpec((tm, V_W), row),
                  pl.BlockSpec((tm, V_W), lambda i: (i, COL_RG // V_W)),
                  pl.BlockSpec((tm, V_W), lambda i: (i, COL_DZ // V_W)),
                  pl.BlockSpec((tm, D_MODEL), lambda i: (i, COL_GR // D_MODEL)),
                  pl.BlockSpec((tm, D_MODEL), lambda i: (i, COL_GD // D_MODEL)),
                  pl.BlockSpec(mod.shape, const),
                  pl.BlockSpec(norm_w.shape, const),
                  pl.BlockSpec(dn_norm_w.shape, const),
                  wspec, wspec, wspec],
        out_specs=pl.BlockSpec((tm, D_MODEL), row),
        out_shape=jax.ShapeDtypeStruct((t, D_MODEL), F32),
        compiler_params=_cparams(("arbitrary",)),
        name="merge_out",
    )(x, o_r, o_f, o_b, proj, proj, proj, proj, mod, norm_w, dn_norm_w, w_ret_o, w_dn_o, w_out)


def _ffn_kernel(x_ref, mod_ref, nw_ref, wi_ref, wo_ref, o_ref, *, tm, rows_per_cond):
    i = pl.program_id(0)
    ci = (i * tm) // rows_per_cond
    m = mod_ref[pl.ds(ci, 1), :]
    sh2 = m[:, 3 * D_MODEL:4 * D_MODEL]
    sc2 = m[:, 4 * D_MODEL:5 * D_MODEL]
    g2 = m[:, 5 * D_MODEL:6 * D_MODEL]
    x = x_ref[...]
    hb = (_rms(x, nw_ref[2:3, :]) * (1.0 + sc2) + sh2).astype(BF16)
    f = None
    lo = 0
    for width in FF_CHUNKS:
        gate = jnp.dot(hb, wi_ref[:, lo:lo + width], preferred_element_type=F32)
        up = jnp.dot(hb, wi_ref[:, D_FF + lo:D_FF + lo + width], preferred_element_type=F32)
        part = _mm(_silu(gate) * up, wo_ref[lo:lo + width, :])
        lo += width
        f = part if f is None else f + part
    o_ref[...] = x + g2 * _rms(f, nw_ref[3:4, :])


def _ffn(x, mod, norm_w, w_in, w_out, rows_per_cond):
    t = x.shape[0]
    tm = 512
    kern = functools.partial(_ffn_kernel, tm=tm, rows_per_cond=rows_per_cond)
    row = lambda i: (i, 0)
    const = lambda i: (0, 0)
    resident = pl.Buffered(1)
    return pl.pallas_call(
        kern,
        grid=(t // tm,),
        in_specs=[pl.BlockSpec((tm, D_MODEL), row),
                  pl.BlockSpec(mod.shape, const),
                  pl.BlockSpec(norm_w.shape, const),
                  pl.BlockSpec((D_MODEL, 2 * D_FF), const, pipeline_mode=resident),
                  pl.BlockSpec((D_FF, D_MODEL), const, pipeline_mode=resident)],
        out_specs=pl.BlockSpec((tm, D_MODEL), row),
        out_shape=jax.ShapeDtypeStruct((t, D_MODEL), F32),
        compiler_params=_cparams(("arbitrary",)),
        name="swiglu",
    )(x, mod, norm_w, w_in, w_out)


def _rope_tables(seq_len):
    rows = seq_len // GRID_W
    row_idx = jnp.repeat(jnp.arange(rows, dtype=F32), GRID_W)
    col_idx = (jnp.arange(seq_len) % GRID_W).astype(F32)
    n_freq = DK // 4
    freqs = ROPE_BASE ** (-jnp.arange(n_freq, dtype=F32) / n_freq)
    ang = jnp.concatenate([row_idx[:, None] * freqs, col_idx[:, None] * freqs], axis=-1)
    cos = jnp.repeat(jnp.cos(ang), 2, axis=-1)
    sin = jnp.repeat(jnp.sin(ang), 2, axis=-1)
    sign = jnp.tile(jnp.array([-1.0, 1.0], F32), DK // 2)
    return cos, sin * sign


def _one_path(x, mod, seq_len, n_seq, wts, rope_tabs, s_ret0, s_dn0, emit_state):
    rows_per_cond = seq_len if mod.shape[0] > 1 else x.shape[0]
    proj, gates = _inproj(x, mod, wts["norm_w"], wts["w_in"], wts["w_b"], wts["w_gate_t"],
                          rows_per_cond)
    ret = _retention(proj, wts["logit_rep"], wts["ret_gn_w"], n_seq, seq_len,
                     rope_tabs=rope_tabs, s0=s_ret0, emit_state=emit_state)
    qkv = _gdn_prep(proj, wts["conv_w"], seq_len)
    gdn = _gdn(qkv, gates, wts["a_log"], wts["dt_bias"], n_seq, seq_len,
               s0=s_dn0, emit_state=emit_state)
    x1 = _merge(x, ret[0], gdn[0], gdn[1], proj, mod, wts["norm_w"], wts["dn_norm_w"],
                wts["w_ret_o"], wts["w_dn_o"], wts["w_out"], rows_per_cond)
    y = _ffn(x1, mod, wts["norm_w"], wts["w_ffn_in"], wts["w_ffn_out"], rows_per_cond)
    if emit_state:
        return y, ret[1], gdn[2]
    return y, None, None


def kernel(x_prompt, x_sample, c, state_ret, state_dn, c_ctx, w_mod, b_mod, norm_w, w_in, conv_w,
           ret_decay_logit, ret_gn_w, dn_a_log, dn_dt_bias, dn_norm_w, w_ret_o, w_dn_o, w_out,
           w_ffn_in, w_ffn_out):
    n_ctx, l_ctx, _ = x_prompt.shape
    n_lat, l_lat, _ = x_sample.shape
    assert w_mod.shape[0] == 1, "single-layer kernel"

    cond8 = jnp.zeros((8, D_MODEL), F32).at[0].set(c_ctx).at[1:1 + n_lat].set(c)
    mod = _modulation(cond8, w_mod[0], b_mod)
    mod_ctx, mod_lat = mod[0:1], mod[1:1 + n_lat]

    wi = w_in[0]
    c_gr = COL_GATE + 2 * N_GATE
    wts = {
        "norm_w": norm_w[0],
        "w_in": w_in,
        "w_b": _weight_tail(w_in, COL_GATE, 2 * N_GATE, PROJ_W - COL_GATE),
        "w_gate_t": wi[:, COL_GATE:c_gr].T,
        "logit_rep": jnp.broadcast_to(ret_decay_logit[0].T[:, :, None], (HEADS, 2, DK)),
        "ret_gn_w": ret_gn_w,
        "conv_w": conv_w[0],
        "a_log": dn_a_log.reshape(N_GATE, 1),
        "dt_bias": dn_dt_bias.reshape(N_GATE, 1),
        "dn_norm_w": dn_norm_w,
        "w_ret_o": w_ret_o[0].astype(BF16),
        "w_dn_o": w_dn_o[0].astype(BF16),
        "w_out": w_out[0].astype(BF16),
        "w_ffn_in": w_ffn_in[0].astype(BF16),
        "w_ffn_out": w_ffn_out[0].astype(BF16),
    }

    y_p, s_ret, s_dn = _one_path(x_prompt.reshape(n_ctx * l_ctx, D_MODEL), mod_ctx, l_ctx, n_ctx,
                                 wts, None, None, None, True)
    y_s, _, _ = _one_path(x_sample.reshape(n_lat * l_lat, D_MODEL), mod_lat, l_lat, n_lat,
                          wts, _rope_tables(l_lat), state_ret, state_dn, False)
    return (y_p.reshape(x_prompt.shape), y_s.reshape(x_sample.shape), s_ret, s_dn)
```

```python
import functools

import jax
import jax.numpy as jnp
from jax import lax
from jax.experimental import pallas as pl
from jax.experimental.pallas import tpu as pltpu

F32 = jnp.float32
BF16 = jnp.bfloat16

D_MODEL = 1024
HEADS = 4
DK = 128
DV = 256
QK_W = HEADS * DK
V_W = HEADS * DV
GDN_CHUNK = 64
TRI_BASE = 8
BLK = 256
SUB = 8
LANES = 128
HALO = 16
CONV_WIDTH = 3
GRID_W = 64
ROPE_BASE = 10000.0
EPS = 1e-6
D_FF = 2816
MXU_DIM = 256
FF_CHUNKS = (6 * MXU_DIM, 5 * MXU_DIM)
assert sum(FF_CHUNKS) == D_FF
N_GATE = 2 * HEADS

COL_RQ, COL_RK, COL_RV, COL_RG = 0, 512, 1024, 2048
COL_DQK, COL_DV, COL_DZ = 3072, 4096, 5120
COL_GATE = 6144
COL_GR, COL_GD = 6144, 7168
PROJ_W = 8192
PROJ_TN = 2048

VMEM_LIMIT = 56 * 1024 * 1024


def _cparams(sem):
    return pltpu.CompilerParams(dimension_semantics=sem, vmem_limit_bytes=VMEM_LIMIT)


def _mm(a, b):
    return jnp.dot(a.astype(BF16), b.astype(BF16), preferred_element_type=F32)


def _mm_nt(a, b):
    return lax.dot_general(a.astype(BF16), b.astype(BF16), (((1,), (1,)), ((), ())),
                           preferred_element_type=F32)


def _mm_tn(a, b):
    return lax.dot_general(a.astype(BF16), b.astype(BF16), (((0,), (0,)), ((), ())),
                           preferred_element_type=F32)


def _sigmoid(x):
    return 1.0 / (1.0 + jnp.exp(-x))


def _silu(x):
    return x * _sigmoid(x)


def _softplus(x):
    return jnp.maximum(x, 0.0) + jnp.log(1.0 + jnp.exp(-jnp.abs(x)))


def _log_sigmoid(x):
    return -_softplus(-x)


def _rms(x, w):
    return x * lax.rsqrt(jnp.mean(x * x, axis=-1, keepdims=True) + EPS) * w


def _mod_kernel(c_ref, w_ref, b_ref, o_ref):
    o_ref[...] = _mm(_silu(c_ref[...]), w_ref[...]) + b_ref[...]


def _modulation(cond8, w_mod, b_mod):
    n = w_mod.shape[1]
    tn = 1536
    return pl.pallas_call(
        _mod_kernel,
        grid=(n // tn,),
        in_specs=[pl.BlockSpec((8, D_MODEL), lambda j: (0, 0)),
                  pl.BlockSpec((D_MODEL, tn), lambda j: (0, j)),
                  pl.BlockSpec((1, tn), lambda j: (0, j))],
        out_specs=pl.BlockSpec((8, tn), lambda j: (0, j)),
        out_shape=jax.ShapeDtypeStruct((8, n), F32),
        compiler_params=_cparams(("arbitrary",)),
        name="modulation",
    )(cond8, w_mod, b_mod)


def _inproj_kernel(x_ref, mod_ref, nw_ref, wa_ref, wb_ref, wg_ref, o_ref, g_ref, h_sc, *, tm,
                   rows_per_cond, n_a):
    i = pl.program_id(0)
    j = pl.program_id(1)

    @pl.when(j == 0)
    def _():
        ci = (i * tm) // rows_per_cond
        m = mod_ref[pl.ds(ci, 1), :]
        sh1 = m[:, 0:D_MODEL]
        sc1 = m[:, D_MODEL:2 * D_MODEL]
        h = _rms(x_ref[...], nw_ref[0:1, :]) * (1.0 + sc1) + sh1
        hb = h.astype(BF16)
        h_sc[...] = hb
        g_ref[...] = _mm_nt(wg_ref[...], hb)

    @pl.when(j < n_a)
    def _():
        o_ref[...] = _mm_nt(h_sc[...], wa_ref[...]).astype(o_ref.dtype)

    @pl.when(j >= n_a)
    def _():
        o_ref[...] = _mm_nt(h_sc[...], wb_ref[...]).astype(o_ref.dtype)


def _inproj(x, mod, norm_w, w_a, w_b, w_gate, rows_per_cond):
    t = x.shape[0]
    tm, tn = 1024, PROJ_TN
    n_a = w_a.shape[0] // tn
    kern = functools.partial(_inproj_kernel, tm=tm, rows_per_cond=rows_per_cond, n_a=n_a)
    return pl.pallas_call(
        kern,
        grid=(t // tm, PROJ_W // tn),
        in_specs=[pl.BlockSpec((tm, D_MODEL), lambda i, j: (i, 0)),
                  pl.BlockSpec(mod.shape, lambda i, j: (0, 0)),
                  pl.BlockSpec(norm_w.shape, lambda i, j: (0, 0)),
                  pl.BlockSpec((tn, D_MODEL), lambda i, j: (jnp.minimum(j, n_a - 1), 0)),
                  pl.BlockSpec((tn, D_MODEL), lambda i, j: (jnp.maximum(j - n_a, 0), 0)),
                  pl.BlockSpec((2 * N_GATE, D_MODEL), lambda i, j: (0, 0))],
        out_specs=[pl.BlockSpec((tm, tn), lambda i, j: (i, j)),
                   pl.BlockSpec((2 * N_GATE, tm), lambda i, j: (0, i))],
        out_shape=[jax.ShapeDtypeStruct((t, PROJ_W), BF16),
                   jax.ShapeDtypeStruct((2 * N_GATE, t), F32)],
        scratch_shapes=[pltpu.VMEM((tm, D_MODEL), BF16)],
        compiler_params=_cparams(("arbitrary", "arbitrary")),
        name="inproj",
    )(x, mod, norm_w, w_a, w_b, w_gate)


RET_GROUP = 8


def _pair_swap(x):
    lane = lax.broadcasted_iota(jnp.int32, x.shape, 1)
    nxt = pltpu.roll(x, x.shape[1] - 1, 1)
    prv = pltpu.roll(x, 1, 1)
    return jnp.where((lane & 1) == 0, nxt, prv)


def _ret_kernel(*refs, seq_len, rope, has_s0, emit_state):
    it = iter(refs)
    q_ref, k_ref, v_ref = next(it), next(it), next(it)
    cos_ref = next(it) if rope else None
    sin_ref = next(it) if rope else None
    logit_ref, gnw_ref = next(it), next(it)
    s0_ref = next(it) if has_s0 else None
    o_ref = next(it)
    st_ref = next(it) if emit_state else None
    q_sc, k_sc, st_sc = next(it), next(it), next(it)

    nb = seq_len // BLK
    cross = has_s0 or nb > 1

    def rows(c):
        return slice(c * BLK, (c + 1) * BLK)

    def qk_cols(h):
        return slice(h * DK, (h + 1) * DK)

    def v_cols(h):
        return slice(h * DV, (h + 1) * DV)

    for h in range(HEADS):
        q = q_ref[:, qk_cols(h)].astype(F32) * (DK ** -0.5)
        k = k_ref[:, qk_cols(h)].astype(F32)
        if rope:
            cos = cos_ref[...]
            sin = sin_ref[...]
            q = q * cos + _pair_swap(q) * sin
            k = k * cos + _pair_swap(k) * sin
        q_sc[:, qk_cols(h)] = q
        k_sc[:, qk_cols(h)] = k

    pos = lax.broadcasted_iota(jnp.int32, (BLK, DK), 0).astype(F32)

    def log_gammas(h):
        lg = _log_sigmoid(logit_ref[h])
        return lg[0:1, :], lg[1:2, :]

    for h in range(HEADS):
        lgf, lgb = log_gammas(h)
        kdec_f = jnp.exp(lgf * (BLK - 1.0 - pos))
        kdec_b = jnp.exp(lgb * pos)
        cdec_f = jnp.exp(lgf[:, 0:1] * float(BLK))
        cdec_b = jnp.exp(lgb[:, 0:1] * float(BLK))
        af = {c: _mm_tn(k_sc[rows(c), qk_cols(h)] * kdec_f, v_ref[rows(c), v_cols(h)])
              for c in range(nb) if c < nb - 1 or emit_state}
        ab = {c: _mm_tn(k_sc[rows(c), qk_cols(h)] * kdec_b, v_ref[rows(c), v_cols(h)])
              for c in range(nb) if c > 0 or emit_state}
        s = s0_ref[0, h] if has_s0 else jnp.zeros((DK, DV), F32)
        for c in range(nb):
            if cross:
                st_sc[h, c, 0:DK, :] = s.astype(BF16)
            if c in af:
                s = s * cdec_f + af[c]
        if emit_state:
            st_ref[0, h] = s
        s = s0_ref[1, h] if has_s0 else jnp.zeros((DK, DV), F32)
        for c in range(nb - 1, -1, -1):
            if cross:
                st_sc[h, c, DK:2 * DK, :] = s.astype(BF16)
            if c in ab:
                s = s * cdec_b + ab[c]
        if emit_state:
            st_ref[1, h] = s

    ii = lax.broadcasted_iota(jnp.int32, (BLK, BLK), 0)
    jj = lax.broadcasted_iota(jnp.int32, (BLK, BLK), 1)
    diff = (ii - jj).astype(F32)
    pos2 = lax.broadcasted_iota(jnp.int32, (BLK, 2 * DK), 0).astype(F32)
    first_half = lax.broadcasted_iota(jnp.int32, (BLK, 2 * DK), 1) < DK

    def head_consts(h):
        lgf, lgb = log_gammas(h)
        lgf1, lgb1 = lgf[:, 0:1], lgb[:, 0:1]
        dmat = (jnp.where(diff >= 0, jnp.exp(lgf1 * jnp.maximum(diff, 0.0)), 0.0)
                + jnp.where(diff <= 0, jnp.exp(lgb1 * jnp.maximum(-diff, 0.0)), 0.0))
        qdec = jnp.where(first_half, jnp.exp(lgf1 * (pos2 + 1.0)), jnp.exp(lgb1 * (BLK - pos2)))
        return dmat, qdec

    def unit(h, c, consts):
        dmat, qdec = consts
        qc = q_sc[rows(c), qk_cols(h)]
        s = _mm_nt(qc, k_sc[rows(c), qk_cols(h)])
        yield
        o = _mm(s * dmat, v_ref[rows(c), v_cols(h)])
        if cross:
            o_cross = _mm(jnp.concatenate([qc, qc], axis=1) * qdec, st_sc[h, c])
        yield
        if cross:
            o = o + o_cross
        mu = jnp.mean(o, axis=-1, keepdims=True)
        oc = o - mu
        var = jnp.mean(oc * oc, axis=-1, keepdims=True)
        o_ref[rows(c), v_cols(h)] = (oc * lax.rsqrt(var + EPS) * gnw_ref[:, v_cols(h)]).astype(o_ref.dtype)
        yield

    units = [(h, c) for h in range(HEADS) for c in range(nb)]
    for g0 in range(0, len(units), RET_GROUP):
        consts = {}
        gens = []
        for h, c in units[g0:g0 + RET_GROUP]:
            if h not in consts:
                consts[h] = head_consts(h)
            gens.append(unit(h, c, consts[h]))
        for _ in zip(*gens):
            pass


def _retention(proj, logit_rep, gn_w, n_seq, seq_len, rope_tabs=None, s0=None, emit_state=False):
    t = proj.shape[0]
    rope = rope_tabs is not None
    has_s0 = s0 is not None
    kern = functools.partial(_ret_kernel, seq_len=seq_len, rope=rope, has_s0=has_s0,
                             emit_state=emit_state)
    in_specs = [pl.BlockSpec((seq_len, QK_W), lambda b: (b, COL_RQ // QK_W)),
                pl.BlockSpec((seq_len, QK_W), lambda b: (b, COL_RK // QK_W)),
                pl.BlockSpec((seq_len, V_W), lambda b: (b, COL_RV // V_W))]
    args = [proj, proj, proj]
    if rope:
        in_specs += [pl.BlockSpec((seq_len, DK), lambda b: (0, 0))] * 2
        args += list(rope_tabs)
    in_specs += [pl.BlockSpec((HEADS, 2, DK), lambda b: (0, 0, 0)),
                 pl.BlockSpec((1, V_W), lambda b: (0, 0))]
    args += [logit_rep, gn_w]
    state_spec = pl.BlockSpec((None, None, 2, HEADS, DK, DV), lambda b: (b, 0, 0, 0, 0, 0))
    if has_s0:
        in_specs.append(state_spec)
        args.append(s0)
    out_specs = [pl.BlockSpec((seq_len, V_W), lambda b: (b, 0))]
    out_shape = [jax.ShapeDtypeStruct((t, V_W), BF16)]
    if emit_state:
        out_specs.append(state_spec)
        out_shape.append(jax.ShapeDtypeStruct((n_seq, 1, 2, HEADS, DK, DV), F32))
    nb = seq_len // BLK
    return pl.pallas_call(
        kern,
        grid=(n_seq,),
        in_specs=in_specs,
        out_specs=out_specs,
        out_shape=out_shape,
        scratch_shapes=[pltpu.VMEM((seq_len, QK_W), F32), pltpu.VMEM((seq_len, QK_W), F32),
                        pltpu.VMEM((HEADS, nb, 2 * DK, DV), BF16)],
        compiler_params=_cparams(("arbitrary",)),
        name="retention",
    )(*args)


N_MASK = 8
(MASK_EQ_F, MASK_EQ_B, MASK_OFFDIAG, MASK_EYE, MASK_BASE, MASK_PAIR0) = 0, 1, 2, 3, 4, 5


def _build_masks(mask_sc, eye_sc):
    ii = lax.broadcasted_iota(jnp.int32, (BLK, BLK), 0)
    jj = lax.broadcasted_iota(jnp.int32, (BLK, BLK), 1)

    def blk(v, size):
        return lax.shift_right_logical(v, size.bit_length() - 1)

    same = blk(ii, GDN_CHUNK) == blk(jj, GDN_CHUNK)
    mask_sc[MASK_EQ_F] = (same & (jj <= ii)).astype(F32)
    mask_sc[MASK_EQ_B] = (same & (jj >= ii)).astype(F32)
    mask_sc[MASK_OFFDIAG] = (ii != jj).astype(F32)
    mask_sc[MASK_EYE] = (ii == jj).astype(F32)
    mask_sc[MASK_BASE] = (blk(ii, TRI_BASE) == blk(jj, TRI_BASE)).astype(F32)
    size, idx = TRI_BASE, MASK_PAIR0
    while size < GDN_CHUNK:
        pair = (blk(ii, 2 * size) == blk(jj, 2 * size)) & (blk(ii, size) != blk(jj, size))
        mask_sc[idx] = pair.astype(F32)
        size, idx = 2 * size, idx + 1
    eye_sc[...] = (ii == jj).astype(BF16)


def _seg_cumsums(x):
    n = x.shape[1]
    pos = lax.broadcasted_iota(jnp.int32, x.shape, 1) & (GDN_CHUNK - 1)
    up, down = x, x
    s = 1
    while s < GDN_CHUNK:
        up = up + jnp.where(pos >= s, pltpu.roll(up, s, 1), 0.0)
        down = down + jnp.where(pos < GDN_CHUNK - s, pltpu.roll(down, n - s, 1), 0.0)
        s *= 2
    return up, down


def _rows_to_cols(rows, eye_sc):
    p1 = rows.astype(BF16).astype(F32)
    r1 = rows - p1
    p2 = r1.astype(BF16).astype(F32)
    p3 = r1 - p2
    pieces = jnp.concatenate([p1, p2, p3], axis=0).astype(BF16)
    c3 = lax.dot_general(eye_sc[...], pieces, (((1,), (1,)), ((), ())), preferred_element_type=F32)
    return c3[:, 0:SUB] + c3[:, SUB:2 * SUB] + c3[:, 2 * SUB:3 * SUB]


def _gdn_kernel(*refs, nb, has_s0, emit_state):
    it = iter(refs)
    xs_refs = (next(it), next(it))
    g_refs = (next(it), next(it))
    alog_ref, dtb_ref = next(it), next(it)
    s0_ref = next(it) if has_s0 else None
    o_refs = (next(it), next(it))
    st_ref = next(it) if emit_state else None
    st_sc, mask_sc, eye_sc = next(it), next(it), next(it)

    b = pl.program_id(0)
    step = pl.program_id(1)

    @pl.when((b == 0) & (step == 0))
    def _():
        _build_masks(mask_sc, eye_sc)

    @pl.when(step == 0)
    def _():
        for d in range(2):
            for h in range(HEADS):
                st_sc[d, h] = s0_ref[d, h] if has_s0 else jnp.zeros((DK, DV), F32)

    blk_of = (step, nb - 1 - step)

    row8 = lax.broadcasted_iota(jnp.int32, (N_GATE, BLK), 0)
    fwd_rows = row8 < HEADS
    logit_b = jnp.where(fwd_rows, g_refs[0][0:N_GATE, :], g_refs[1][0:N_GATE, :])
    logit_a = jnp.where(fwd_rows, g_refs[0][N_GATE:2 * N_GATE, :], g_refs[1][N_GATE:2 * N_GATE, :])
    beta = _sigmoid(logit_b)
    la = -jnp.exp(alog_ref[...]) * _softplus(logit_a + dtb_ref[...])
    up, down = _seg_cumsums(la)
    g8 = jnp.where(fwd_rows, up, down)
    r8 = jnp.where(fwd_rows, down, up) - la
    eg8 = jnp.exp(g8)
    erb8 = jnp.exp(r8) * beta
    etot8 = jnp.exp(up + down - la)
    gcols = _rows_to_cols(g8, eye_sc)

    nck = BLK // GDN_CHUNK

    def chain(d, h):
        c = d * HEADS + h
        xs_ref = xs_refs[d]
        q16 = xs_ref[:, h * DK:(h + 1) * DK]
        k16 = xs_ref[:, QK_W + h * DK:QK_W + (h + 1) * DK]
        v16 = xs_ref[:, 2 * QK_W + h * DV:2 * QK_W + (h + 1) * DV]
        q = q16.astype(F32)
        k = k16.astype(F32)
        kk = _mm_nt(k16, k16)
        qk = _mm_nt(q16, k16)
        yield
        g_col = gcols[:, c:c + 1]
        g_row = g8[c:c + 1, :]
        m_eq = mask_sc[MASK_EQ_F if d == 0 else MASK_EQ_B]
        decay_beta = (jnp.exp(jnp.minimum(g_col - g_row, 0.0)) * m_eq) * beta[c:c + 1, :]
        lp = kk * decay_beta * mask_sc[MASK_OFFDIAG]
        sc16 = (qk * decay_beta).astype(BF16)
        qg16 = (q * jnp.exp(g_col)).astype(BF16)
        kt16 = (k.T * erb8[c:c + 1, :]).astype(BF16)

        x = -(lp * mask_sc[MASK_BASE])
        t = mask_sc[MASK_EYE] + x
        for _ in range(TRI_BASE.bit_length() - 2):
            x = _mm(x, x)
            yield
            t = t + _mm(t, x)
            yield
        size, idx = TRI_BASE, MASK_PAIR0
        while size < GDN_CHUNK:
            groups = [slice(r * size, (r + 1) * size) for r in range(BLK // size)]
            late = [(r % 2 == 1) == (d == 0) for r in range(BLK // size)]
            e_late = jnp.concatenate([lp[g, :] * mask_sc[idx, g, :] for g, on in zip(groups, late) if on],
                                     axis=0)
            et = _mm(e_late, t)
            yield
            t_late = jnp.concatenate([t[g, :] for g, on in zip(groups, late) if on], axis=0)
            et_rows = iter(et[j * size:(j + 1) * size, :] for j in range(BLK // size // 2))
            et_full = jnp.concatenate([next(et_rows) if on else jnp.zeros((size, BLK), F32)
                                       for on in late], axis=0)
            t_late = t_late - _mm(t_late, et_full)
            yield
            new_rows = iter(t_late[j * size:(j + 1) * size, :] for j in range(BLK // size // 2))
            t = jnp.concatenate([next(new_rows) if on else t[g, :] for g, on in zip(groups, late)],
                                axis=0)
            size, idx = 2 * size, idx + 1
        u = _mm(t, v16)
        w16 = _mm(t * eg8[c:c + 1, :], k16).astype(BF16)
        yield

        order = range(nck) if d == 0 else range(nck - 1, -1, -1)
        for cc in order:
            rs = slice(cc * GDN_CHUNK, (cc + 1) * GDN_CHUNK)
            s_prev = st_sc[d, h]
            wq = jnp.concatenate([w16[rs, :], qg16[rs, :]], axis=0)
            ws_qs = _mm(wq, s_prev)
            yield
            v_new = (u[rs, :] - ws_qs[0:GDN_CHUNK]).astype(BF16)
            o = ws_qs[GDN_CHUNK:] + _mm(sc16[rs, rs], v_new)
            decay_all = etot8[c:c + 1, cc * GDN_CHUNK:cc * GDN_CHUNK + 1]
            st_sc[d, h] = s_prev * decay_all + _mm(kt16[:, rs], v_new)
            o_refs[d][rs, h * DV:(h + 1) * DV] = o.astype(o_refs[d].dtype)
            yield

    chains = [chain(d, h) for d in range(2) for h in range(HEADS)]
    for _ in zip(*chains):
        pass

    if emit_state:
        @pl.when(step == nb - 1)
        def _():
            for d in range(2):
                for h in range(HEADS):
                    st_ref[d, h] = st_sc[d, h]


def _prep_kernel(xqk_ref, xv_ref, pqk_ref, pv_ref, nqk_ref, nv_ref, cw_ref, o_ref, *, seq_len, rb):
    t0 = pl.program_id(0) * rb

    def conv(x_ref, hp_ref, hn_ref, lo, width, w_lo):
        cols = slice(lo, lo + width)
        cur = x_ref[:, cols].astype(F32)
        row = lax.broadcasted_iota(jnp.int32, cur.shape, 0)
        before = hp_ref[HALO - 1:HALO, cols].astype(F32)
        after = hn_ref[0:1, cols].astype(F32)
        if seq_len >= rb:
            before = jnp.where((t0 & (seq_len - 1)) == 0, 0.0, before)
            after = jnp.where(((t0 + rb) & (seq_len - 1)) == 0, 0.0, after)
        prev = jnp.where(row == 0, before, pltpu.roll(cur, 1, 0))
        nxt = jnp.where(row == rb - 1, after, pltpu.roll(cur, rb - 1, 0))
        if seq_len < rb:
            pos = (row + t0) & (seq_len - 1)
            prev = jnp.where(pos == 0, 0.0, prev)
            nxt = jnp.where(pos == seq_len - 1, 0.0, nxt)
        cw = cw_ref[:, w_lo:w_lo + width]
        return _silu(prev * cw[0:1, :] + cur * cw[1:2, :] + nxt * cw[2:3, :])

    def l2n(x):
        return x * lax.rsqrt(jnp.sum(x * x, axis=-1, keepdims=True) + EPS)

    for h in range(HEADS):
        lo = h * DK
        q = l2n(conv(xqk_ref, pqk_ref, nqk_ref, lo, DK, lo)) * (DK ** -0.5)
        o_ref[:, lo:lo + DK] = q.astype(o_ref.dtype)
        lo = QK_W + h * DK
        o_ref[:, lo:lo + DK] = l2n(conv(xqk_ref, pqk_ref, nqk_ref, lo, DK, lo)).astype(o_ref.dtype)
    for h in range(HEADS):
        lo = h * DV
        v = conv(xv_ref, pv_ref, nv_ref, lo, DV, 2 * QK_W + lo)
        o_ref[:, 2 * QK_W + lo:2 * QK_W + lo + DV] = v.astype(o_ref.dtype)


def _gdn_prep(proj, conv_w, seq_len):
    t = proj.shape[0]
    rb = 1024
    assert seq_len & (seq_len - 1) == 0 and (rb % seq_len == 0 or seq_len % rb == 0)
    kern = functools.partial(_prep_kernel, seq_len=seq_len, rb=rb)
    halo_per_blk = rb // HALO
    n_halo = t // HALO
    width = 2 * QK_W + V_W

    def spec_x(col):
        return pl.BlockSpec((rb, V_W), lambda i: (i, col // V_W))

    def spec_prev(col):
        return pl.BlockSpec((HALO, V_W), lambda i: (jnp.maximum(i * halo_per_blk - 1, 0), col // V_W))

    def spec_next(col):
        return pl.BlockSpec((HALO, V_W),
                            lambda i: (jnp.minimum((i + 1) * halo_per_blk, n_halo - 1), col // V_W))

    in_specs = [spec(col) for spec in (spec_x, spec_prev, spec_next) for col in (COL_DQK, COL_DV)]
    in_specs.append(pl.BlockSpec((CONV_WIDTH, width), lambda i: (0, 0)))
    return pl.pallas_call(
        kern,
        grid=(t // rb,),
        in_specs=in_specs,
        out_specs=pl.BlockSpec((rb, width), lambda i: (i, 0)),
        out_shape=jax.ShapeDtypeStruct((t, width), BF16),
        compiler_params=_cparams(("arbitrary",)),
        name="gdn_prep",
    )(*([proj] * 6), conv_w)


def _gdn(qkv, gates, a_log, dt_bias, n_seq, seq_len, s0=None, emit_state=False):
    t = qkv.shape[0]
    nb = seq_len // BLK
    has_s0 = s0 is not None
    kern = functools.partial(_gdn_kernel, nb=nb, has_s0=has_s0, emit_state=emit_state)

    def fblk(b, s):
        return b * nb + s

    def bblk(b, s):
        return b * nb + nb - 1 - s

    def spec_x(blk):
        return pl.BlockSpec((BLK, qkv.shape[1]), lambda b, s: (blk(b, s), 0))

    def spec_g(blk):
        return pl.BlockSpec((2 * N_GATE, BLK), lambda b, s: (0, blk(b, s)))

    const2 = lambda b, s: (0, 0)
    in_specs = [spec_x(fblk), spec_x(bblk), spec_g(fblk), spec_g(bblk),
                pl.BlockSpec((N_GATE, 1), const2),
                pl.BlockSpec((N_GATE, 1), const2)]
    args = [qkv, qkv, gates, gates, a_log, dt_bias]
    state_spec = pl.BlockSpec((None, None, 2, HEADS, DK, DV), lambda b, s: (b, 0, 0, 0, 0, 0))
    if has_s0:
        in_specs.append(state_spec)
        args.append(s0)
    out_specs = [pl.BlockSpec((BLK, V_W), lambda b, s: (fblk(b, s), 0)),
                 pl.BlockSpec((BLK, V_W), lambda b, s: (bblk(b, s), 0))]
    out_shape = [jax.ShapeDtypeStruct((t, V_W), BF16), jax.ShapeDtypeStruct((t, V_W), BF16)]
    if emit_state:
        out_specs.append(state_spec)
        out_shape.append(jax.ShapeDtypeStruct((n_seq, 1, 2, HEADS, DK, DV), F32))
    scratch = [pltpu.VMEM((2, HEADS, DK, DV), F32),
               pltpu.VMEM((N_MASK, BLK, BLK), F32),
               pltpu.VMEM((BLK, BLK), BF16)]
    return pl.pallas_call(
        kern,
        grid=(n_seq, nb),
        in_specs=in_specs,
        out_specs=out_specs,
        out_shape=out_shape,
        scratch_shapes=scratch,
        compiler_params=_cparams(("arbitrary", "arbitrary")),
        name="gated_delta",
    )(*args)


def _merge_kernel(x_ref, or_ref, of_ref, ob_ref, rg_ref, dz_ref, gr_ref, gd_ref, mod_ref, nw_ref,
                  dnw_ref, wr_ref, wd_ref, wo_ref, o_ref, *, tm, rows_per_cond):
    i = pl.program_id(0)
    ci = (i * tm) // rows_per_cond
    m = mod_ref[pl.ds(ci, 1), :]
    g1 = m[:, 2 * D_MODEL:3 * D_MODEL]
    y_r = _mm(_silu(rg_ref[...].astype(F32)) * or_ref[...].astype(F32), wr_ref[...])
    dnw = dnw_ref[...]
    heads = []
    for h in range(HEADS):
        cols = slice(h * DV, (h + 1) * DV)
        od = _rms(of_ref[:, cols].astype(F32) + ob_ref[:, cols].astype(F32), dnw)
        heads.append((od * _silu(dz_ref[:, cols].astype(F32))).astype(BF16))
    y_d = jnp.dot(jnp.concatenate(heads, axis=1), wd_ref[...], preferred_element_type=F32)
    merged = _sigmoid(gr_ref[...].astype(F32)) * y_r + _sigmoid(gd_ref[...].astype(F32)) * y_d
    mo = _mm(merged, wo_ref[...])
    o_ref[...] = x_ref[...] + g1 * _rms(mo, nw_ref[1:2, :])


def _merge(x, o_r, o_f, o_b, proj, mod, norm_w, dn_norm_w, w_ret_o, w_dn_o, w_out, rows_per_cond):
    t = x.shape[0]
    tm = 512
    kern = functools.partial(_merge_kernel, tm=tm, rows_per_cond=rows_per_cond)
    row = lambda i: (i, 0)
    const = lambda i: (0, 0)
    wspec = pl.BlockSpec((D_MODEL, D_MODEL), const)
    return pl.pallas_call(
        kern,
        grid=(t // tm,),
        in_specs=[pl.BlockSpec((tm, D_MODEL), row),
                  pl.BlockSpec((tm, V_W), row),
                  pl.BlockSpec((tm, V_W), row),
                  pl.BlockSpec((tm, V_W), row),
                  pl.BlockSpec((tm, V_W), lambda i: (i, COL_RG // V_W)),
                  pl.BlockSpec((tm, V_W), lambda i: (i, COL_DZ // V_W)),
                  pl.BlockSpec((tm, D_MODEL), lambda i: (i, COL_GR // D_MODEL)),
                  pl.BlockSpec((tm, D_MODEL), lambda i: (i, COL_GD // D_MODEL)),
                  pl.BlockSpec(mod.shape, const),
                  pl.BlockSpec(norm_w.shape, const),
                  pl.BlockSpec(dn_norm_w.shape, const),
                  wspec, wspec, wspec],
        out_specs=pl.BlockSpec((tm, D_MODEL), row),
        out_shape=jax.ShapeDtypeStruct((t, D_MODEL), F32),
        compiler_params=_cparams(("arbitrary",)),
        name="merge_out",
    )(x, o_r, o_f, o_b, proj, proj, proj, proj, mod, norm_w, dn_norm_w, w_ret_o, w_dn_o, w_out)


def _ffn_kernel(x_ref, mod_ref, nw_ref, wi_ref, wo_ref, o_ref, *, tm, rows_per_cond):
    i = pl.program_id(0)
    ci = (i * tm) // rows_per_cond
    m = mod_ref[pl.ds(ci, 1), :]
    sh2 = m[:, 3 * D_MODEL:4 * D_MODEL]
    sc2 = m[:, 4 * D_MODEL:5 * D_MODEL]
    g2 = m[:, 5 * D_MODEL:6 * D_MODEL]
    x = x_ref[...]
    hb = (_rms(x, nw_ref[2:3, :]) * (1.0 + sc2) + sh2).astype(BF16)
    f = None
    lo = 0
    for width in FF_CHUNKS:
        gate = jnp.dot(hb, wi_ref[:, lo:lo + width], preferred_element_type=F32)
        up = jnp.dot(hb, wi_ref[:, D_FF + lo:D_FF + lo + width], preferred_element_type=F32)
        part = _mm(_silu(gate) * up, wo_ref[lo:lo + width, :])
        lo += width
        f = part if f is None else f + part
    o_ref[...] = x + g2 * _rms(f, nw_ref[3:4, :])


def _ffn(x, mod, norm_w, w_in, w_out, rows_per_cond):
    t = x.shape[0]
    tm = 512
    kern = functools.partial(_ffn_kernel, tm=tm, rows_per_cond=rows_per_cond)
    row = lambda i: (i, 0)
    const = lambda i: (0, 0)
    resident = pl.Buffered(1)
    return pl.pallas_call(
        kern,
        grid=(t // tm,),
        in_specs=[pl.BlockSpec((tm, D_MODEL), row),
                  pl.BlockSpec(mod.shape, const),
                  pl.BlockSpec(norm_w.shape, const),
                  pl.BlockSpec((D_MODEL, 2 * D_FF), const, pipeline_mode=resident),
                  pl.BlockSpec((D_FF, D_MODEL), const, pipeline_mode=resident)],
        out_specs=pl.BlockSpec((tm, D_MODEL), row),
        out_shape=jax.ShapeDtypeStruct((t, D_MODEL), F32),
        compiler_params=_cparams(("arbitrary",)),
        name="swiglu",
    )(x, mod, norm_w, w_in, w_out)


def _rope_tables(seq_len):
    rows = seq_len // GRID_W
    row_idx = jnp.repeat(jnp.arange(rows, dtype=F32), GRID_W)
    col_idx = (jnp.arange(seq_len) % GRID_W).astype(F32)
    n_freq = DK // 4
    freqs = ROPE_BASE ** (-jnp.arange(n_freq, dtype=F32) / n_freq)
    ang = jnp.concatenate([row_idx[:, None] * freqs, col_idx[:, None] * freqs], axis=-1)
    cos = jnp.repeat(jnp.cos(ang), 2, axis=-1)
    sin = jnp.repeat(jnp.sin(ang), 2, axis=-1)
    sign = jnp.tile(jnp.array([-1.0, 1.0], F32), DK // 2)
    return cos, sin * sign


def _one_path(x, mod, seq_len, n_seq, wts, rope_tabs, s_ret0, s_dn0, emit_state):
    rows_per_cond = seq_len if mod.shape[0] > 1 else x.shape[0]
    proj, gates = _inproj(x, mod, wts["norm_w"], wts["w_a"], wts["w_b"], wts["w_gate"],
                          rows_per_cond)
    ret = _retention(proj, wts["logit_rep"], wts["ret_gn_w"], n_seq, seq_len,
                     rope_tabs=rope_tabs, s0=s_ret0, emit_state=emit_state)
    qkv = _gdn_prep(proj, wts["conv_w"], seq_len)
    gdn = _gdn(qkv, gates, wts["a_log"], wts["dt_bias"], n_seq, seq_len,
               s0=s_dn0, emit_state=emit_state)
    x1 = _merge(x, ret[0], gdn[0], gdn[1], proj, mod, wts["norm_w"], wts["dn_norm_w"],
                wts["w_ret_o"], wts["w_dn_o"], wts["w_out"], rows_per_cond)
    y = _ffn(x1, mod, wts["norm_w"], wts["w_ffn_in"], wts["w_ffn_out"], rows_per_cond)
    if emit_state:
        return y, ret[1], gdn[2]
    return y, None, None


def kernel(x_prompt, x_sample, c, state_ret, state_dn, c_ctx, w_mod, b_mod, norm_w, w_in, conv_w,
           ret_decay_logit, ret_gn_w, dn_a_log, dn_dt_bias, dn_norm_w, w_ret_o, w_dn_o, w_out,
           w_ffn_in, w_ffn_out):
    n_ctx, l_ctx, _ = x_prompt.shape
    n_lat, l_lat, _ = x_sample.shape
    assert w_mod.shape[0] == 1, "single-layer kernel"

    cond8 = jnp.zeros((8, D_MODEL), F32).at[0].set(c_ctx).at[1:1 + n_lat].set(c)
    mod = _modulation(cond8, w_mod[0], b_mod)
    mod_ctx, mod_lat = mod[0:1], mod[1:1 + n_lat]

    w_in_t = w_in[0].T
    c_gr = COL_GATE + 2 * N_GATE
    wts = {
        "norm_w": norm_w[0],
        "w_a": w_in_t[:COL_GATE].astype(BF16),
        "w_b": w_in_t[c_gr:].astype(BF16),
        "w_gate": w_in_t[COL_GATE:c_gr].astype(BF16),
        "logit_rep": jnp.broadcast_to(ret_decay_logit[0].T[:, :, None], (HEADS, 2, DK)),
        "ret_gn_w": ret_gn_w,
        "conv_w": conv_w[0],
        "a_log": dn_a_log.reshape(N_GATE, 1),
        "dt_bias": dn_dt_bias.reshape(N_GATE, 1),
        "dn_norm_w": dn_norm_w,
        "w_ret_o": w_ret_o[0].astype(BF16),
        "w_dn_o": w_dn_o[0].astype(BF16),
        "w_out": w_out[0].astype(BF16),
        "w_ffn_in": w_ffn_in[0].astype(BF16),
        "w_ffn_out": w_ffn_out[0].astype(BF16),
    }

    y_p, s_ret, s_dn = _one_path(x_prompt.reshape(n_ctx * l_ctx, D_MODEL), mod_ctx, l_ctx, n_ctx,
                                 wts, None, None, None, True)
    y_s, _, _ = _one_path(x_sample.reshape(n_lat * l_lat, D_MODEL), mod_lat, l_lat, n_lat,
                          wts, _rope_tables(l_lat), state_ret, state_dn, False)
    return (y_p.reshape(x_prompt.shape), y_s.reshape(x_sample.shape), s_ret, s_dn)
```

```python
import functools

import jax
import jax.numpy as jnp
from jax import lax
from jax.experimental import pallas as pl
from jax.experimental.pallas import tpu as pltpu

F32 = jnp.float32
BF16 = jnp.bfloat16

D_MODEL = 1024
HEADS = 4
DK = 128
DV = 256
QK_W = HEADS * DK
V_W = HEADS * DV
GDN_CHUNK = 64
TRI_BASE = 8
BLK = 256
SUB = 8
LANES = 128
HALO = 16
CONV_WIDTH = 3
GRID_W = 64
ROPE_BASE = 10000.0
EPS = 1e-6
D_FF = 2816
MXU_DIM = 256
FF_CHUNKS = (6 * MXU_DIM, 5 * MXU_DIM)
assert sum(FF_CHUNKS) == D_FF
N_GATE = 2 * HEADS

COL_RQ, COL_RK, COL_RV, COL_RG = 0, 512, 1024, 2048
COL_DQK, COL_DV, COL_DZ = 3072, 4096, 5120
COL_GATE = 6144
COL_GR, COL_GD = 6144, 7168
PROJ_W = 8192
PROJ_TN = 2048

VMEM_LIMIT = 56 * 1024 * 1024


def _cparams(sem):
    return pltpu.CompilerParams(dimension_semantics=sem, vmem_limit_bytes=VMEM_LIMIT)


def _mm(a, b):
    return jnp.dot(a.astype(BF16), b.astype(BF16), preferred_element_type=F32)


def _mm_nt(a, b):
    return lax.dot_general(a.astype(BF16), b.astype(BF16), (((1,), (1,)), ((), ())),
                           preferred_element_type=F32)


def _mm_tn(a, b):
    return lax.dot_general(a.astype(BF16), b.astype(BF16), (((0,), (0,)), ((), ())),
                           preferred_element_type=F32)


def _sigmoid(x):
    return 1.0 / (1.0 + jnp.exp(-x))


def _silu(x):
    return x * _sigmoid(x)


def _softplus(x):
    return jnp.maximum(x, 0.0) + jnp.log(1.0 + jnp.exp(-jnp.abs(x)))


def _log_sigmoid(x):
    return -_softplus(-x)


def _rms(x, w):
    return x * lax.rsqrt(jnp.mean(x * x, axis=-1, keepdims=True) + EPS) * w


def _mod_kernel(c_ref, w_ref, b_ref, o_ref):
    o_ref[...] = _mm(_silu(c_ref[...]), w_ref[...]) + b_ref[...]


def _modulation(cond8, w_mod, b_mod):
    n = w_mod.shape[1]
    tn = 1536
    return pl.pallas_call(
        _mod_kernel,
        grid=(n // tn,),
        in_specs=[pl.BlockSpec((8, D_MODEL), lambda j: (0, 0)),
                  pl.BlockSpec((D_MODEL, tn), lambda j: (0, j)),
                  pl.BlockSpec((1, tn), lambda j: (0, j))],
        out_specs=pl.BlockSpec((8, tn), lambda j: (0, j)),
        out_shape=jax.ShapeDtypeStruct((8, n), F32),
        compiler_params=_cparams(("arbitrary",)),
        name="modulation",
    )(cond8, w_mod, b_mod)


CHUNK_W = 256
PLAIN, RET_Q, RET_K, DN_Q, DN_K, DN_V = range(6)
TILE_KINDS = (
    (RET_Q, RET_Q, RET_K, RET_K, PLAIN, PLAIN, PLAIN, PLAIN),
    (PLAIN, PLAIN, PLAIN, PLAIN, DN_Q, DN_Q, DN_K, DN_K),
    (DN_V, DN_V, DN_V, DN_V, PLAIN, PLAIN, PLAIN, PLAIN),
    (PLAIN,) * 8,
)


def _pair_swap(x):
    lane = lax.broadcasted_iota(jnp.int32, x.shape, 1)
    nxt = pltpu.roll(x, x.shape[1] - 1, 1)
    prv = pltpu.roll(x, 1, 1)
    return jnp.where((lane & 1) == 0, nxt, prv)


def _inproj_kernel(*refs, tm, rows_per_cond, n_a, seq_len, rope):
    it = iter(refs)
    x_ref, xp_ref, xn_ref, mod_ref, nw_ref, wa_ref, wb_ref, wg_ref, cw_ref = (next(it) for _ in range(9))
    cos_ref = next(it) if rope else None
    sin_ref = next(it) if rope else None
    o_ref, g_ref, h_sc = next(it), next(it), next(it)

    i = pl.program_id(0)
    j = pl.program_id(1)
    t0 = i * tm

    @pl.when(j == 0)
    def _():
        ci = t0 // rows_per_cond
        m = mod_ref[pl.ds(ci, 1), :]
        sh1 = m[:, 0:D_MODEL]
        sc1 = m[:, D_MODEL:2 * D_MODEL]

        def pre(x):
            return _rms(x, nw_ref[0:1, :]) * (1.0 + sc1) + sh1

        hb = pre(x_ref[...]).astype(BF16)
        h_sc[0:tm, :] = hb
        halo = jnp.concatenate([pre(xp_ref[...]), pre(xn_ref[...])], axis=0)
        h_sc[tm:tm + 2 * SUB, :] = halo.astype(BF16)
        g_ref[...] = _mm_nt(wg_ref[...], hb)

    def conv_silu(p_all, cw_lo):
        width = p_all.shape[1]
        cur = p_all[0:tm]
        before = p_all[tm + SUB - 1:tm + SUB]
        after = p_all[tm + SUB:tm + SUB + 1]
        row = lax.broadcasted_iota(jnp.int32, cur.shape, 0)
        if seq_len >= tm:
            before = jnp.where((t0 & (seq_len - 1)) == 0, 0.0, before)
            after = jnp.where(((t0 + tm) & (seq_len - 1)) == 0, 0.0, after)
        prev = jnp.where(row == 0, before, pltpu.roll(cur, 1, 0))
        nxt = jnp.where(row == tm - 1, after, pltpu.roll(cur, tm - 1, 0))
        if seq_len < tm:
            pos = (row + t0) & (seq_len - 1)
            prev = jnp.where(pos == 0, 0.0, prev)
            nxt = jnp.where(pos == seq_len - 1, 0.0, nxt)
        cw = cw_ref[:, cw_lo:cw_lo + width]
        return _silu(prev * cw[0:1, :] + cur * cw[1:2, :] + nxt * cw[2:3, :])

    def l2n_heads(y, scale):
        outs = []
        for lo in range(0, y.shape[1], DK):
            yh = y[:, lo:lo + DK]
            outs.append(yh * (lax.rsqrt(jnp.sum(yh * yh, axis=-1, keepdims=True) + EPS) * scale))
        return jnp.concatenate(outs, axis=1)

    def rotary(p):
        reps = p.shape[1] // DK
        cos = jnp.concatenate([cos_ref[...]] * reps, axis=1)
        sin = jnp.concatenate([sin_ref[...]] * reps, axis=1)
        return p * cos + _pair_swap(p) * sin

    def tile(w_ref, kinds):
        for c, kind in enumerate(kinds):
            lo = c * CHUNK_W
            w = w_ref[lo:lo + CHUNK_W, :]
            if kind in (DN_Q, DN_K, DN_V):
                p_all = _mm_nt(h_sc[...], w)
                cw_lo = {DN_Q: 0, DN_K: QK_W, DN_V: 2 * QK_W}[kind] + (lo % (2 * QK_W) if kind == DN_V
                                                                      else lo % QK_W)
                y = conv_silu(p_all, cw_lo)
                if kind == DN_Q:
                    y = l2n_heads(y, DK ** -0.5)
                elif kind == DN_K:
                    y = l2n_heads(y, 1.0)
            else:
                y = _mm_nt(h_sc[0:tm, :], w)
                if kind == RET_Q:
                    y = y * (DK ** -0.5)
                if rope and kind in (RET_Q, RET_K):
                    y = rotary(y)
            o_ref[:, lo:lo + CHUNK_W] = y.astype(o_ref.dtype)

    for jt, kinds in enumerate(TILE_KINDS):
        w_ref = wa_ref if jt < n_a else wb_ref

        @pl.when(j == jt)
        def _(w_ref=w_ref, kinds=kinds):
            tile(w_ref, kinds)


def _inproj(x, mod, norm_w, w_a, w_b, w_gate, conv_w, rows_per_cond, seq_len, rope_tabs):
    t = x.shape[0]
    tm, tn = 1024, PROJ_TN
    n_a = w_a.shape[0] // tn
    rope = rope_tabs is not None
    assert tn == len(TILE_KINDS[0]) * CHUNK_W and PROJ_W == len(TILE_KINDS) * tn
    assert seq_len & (seq_len - 1) == 0 and (tm % seq_len == 0 or seq_len % tm == 0)
    kern = functools.partial(_inproj_kernel, tm=tm, rows_per_cond=rows_per_cond, n_a=n_a,
                             seq_len=seq_len, rope=rope)
    sub_per_tile = tm // SUB
    n_sub = t // SUB
    in_specs = [pl.BlockSpec((tm, D_MODEL), lambda i, j: (i, 0)),
                pl.BlockSpec((SUB, D_MODEL), lambda i, j: (jnp.maximum(i * sub_per_tile - 1, 0), 0)),
                pl.BlockSpec((SUB, D_MODEL),
                             lambda i, j: (jnp.minimum((i + 1) * sub_per_tile, n_sub - 1), 0)),
                pl.BlockSpec(mod.shape, lambda i, j: (0, 0)),
                pl.BlockSpec(norm_w.shape, lambda i, j: (0, 0)),
                pl.BlockSpec((tn, D_MODEL), lambda i, j: (jnp.minimum(j, n_a - 1), 0)),
                pl.BlockSpec((tn, D_MODEL), lambda i, j: (jnp.maximum(j - n_a, 0), 0)),
                pl.BlockSpec((2 * N_GATE, D_MODEL), lambda i, j: (0, 0)),
                pl.BlockSpec(conv_w.shape, lambda i, j: (0, 0))]
    args = [x, x, x, mod, norm_w, w_a, w_b, w_gate, conv_w]
    if rope:
        tiles_per_seq = max(seq_len // tm, 1)
        in_specs += [pl.BlockSpec((tm, DK), lambda i, j: (i % tiles_per_seq, 0))] * 2
        args += list(rope_tabs)
    return pl.pallas_call(
        kern,
        grid=(t // tm, PROJ_W // tn),
        in_specs=in_specs,
        out_specs=[pl.BlockSpec((tm, tn), lambda i, j: (i, j)),
                   pl.BlockSpec((2 * N_GATE, tm), lambda i, j: (0, i))],
        out_shape=[jax.ShapeDtypeStruct((t, PROJ_W), BF16),
                   jax.ShapeDtypeStruct((2 * N_GATE, t), F32)],
        scratch_shapes=[pltpu.VMEM((tm + 2 * SUB, D_MODEL), BF16)],
        compiler_params=_cparams(("arbitrary", "arbitrary")),
        name="inproj",
    )(*args)


RET_GROUP = 8


def _ret_kernel(*refs, seq_len, has_s0, emit_state):
    it = iter(refs)
    q_ref, k_ref, v_ref = next(it), next(it), next(it)
    logit_ref, gnw_ref = next(it), next(it)
    s0_ref = next(it) if has_s0 else None
    o_ref = next(it)
    st_ref = next(it) if emit_state else None
    q_sc, k_sc, st_sc = next(it), next(it), next(it)

    nb = seq_len // BLK
    cross = has_s0 or nb > 1

    def rows(c):
        return slice(c * BLK, (c + 1) * BLK)

    def qk_cols(h):
        return slice(h * DK, (h + 1) * DK)

    def v_cols(h):
        return slice(h * DV, (h + 1) * DV)

    q_sc[...] = q_ref[...].astype(F32)
    k_sc[...] = k_ref[...].astype(F32)

    pos = lax.broadcasted_iota(jnp.int32, (BLK, DK), 0).astype(F32)

    def log_gammas(h):
        lg = _log_sigmoid(logit_ref[h])
        return lg[0:1, :], lg[1:2, :]

    for h in range(HEADS):
        lgf, lgb = log_gammas(h)
        kdec_f = jnp.exp(lgf * (BLK - 1.0 - pos))
        kdec_b = jnp.exp(lgb * pos)
        cdec_f = jnp.exp(lgf[:, 0:1] * float(BLK))
        cdec_b = jnp.exp(lgb[:, 0:1] * float(BLK))
        af = {c: _mm_tn(k_sc[rows(c), qk_cols(h)] * kdec_f, v_ref[rows(c), v_cols(h)])
              for c in range(nb) if c < nb - 1 or emit_state}
        ab = {c: _mm_tn(k_sc[rows(c), qk_cols(h)] * kdec_b, v_ref[rows(c), v_cols(h)])
              for c in range(nb) if c > 0 or emit_state}
        s = s0_ref[0, h] if has_s0 else jnp.zeros((DK, DV), F32)
        for c in range(nb):
            if cross:
                st_sc[h, c, 0:DK, :] = s.astype(BF16)
            if c in af:
                s = s * cdec_f + af[c]
        if emit_state:
            st_ref[0, h] = s
        s = s0_ref[1, h] if has_s0 else jnp.zeros((DK, DV), F32)
        for c in range(nb - 1, -1, -1):
            if cross:
                st_sc[h, c, DK:2 * DK, :] = s.astype(BF16)
            if c in ab:
                s = s * cdec_b + ab[c]
        if emit_state:
            st_ref[1, h] = s

    ii = lax.broadcasted_iota(jnp.int32, (BLK, BLK), 0)
    jj = lax.broadcasted_iota(jnp.int32, (BLK, BLK), 1)
    diff = (ii - jj).astype(F32)
    pos2 = lax.broadcasted_iota(jnp.int32, (BLK, 2 * DK), 0).astype(F32)
    first_half = lax.broadcasted_iota(jnp.int32, (BLK, 2 * DK), 1) < DK

    def head_consts(h):
        lgf, lgb = log_gammas(h)
        lgf1, lgb1 = lgf[:, 0:1], lgb[:, 0:1]
        dmat = (jnp.where(diff >= 0, jnp.exp(lgf1 * jnp.maximum(diff, 0.0)), 0.0)
                + jnp.where(diff <= 0, jnp.exp(lgb1 * jnp.maximum(-diff, 0.0)), 0.0))
        qdec = jnp.where(first_half, jnp.exp(lgf1 * (pos2 + 1.0)), jnp.exp(lgb1 * (BLK - pos2)))
        return dmat, qdec

    def unit(h, c, consts):
        dmat, qdec = consts
        qc = q_sc[rows(c), qk_cols(h)]
        s = _mm_nt(qc, k_sc[rows(c), qk_cols(h)])
        yield
        o = _mm(s * dmat, v_ref[rows(c), v_cols(h)])
        if cross:
            o_cross = _mm(jnp.concatenate([qc, qc], axis=1) * qdec, st_sc[h, c])
        yield
        if cross:
            o = o + o_cross
        mu = jnp.mean(o, axis=-1, keepdims=True)
        oc = o - mu
        var = jnp.mean(oc * oc, axis=-1, keepdims=True)
        o_ref[rows(c), v_cols(h)] = (oc * lax.rsqrt(var + EPS) * gnw_ref[:, v_cols(h)]).astype(o_ref.dtype)
        yield

    units = [(h, c) for h in range(HEADS) for c in range(nb)]
    for g0 in range(0, len(units), RET_GROUP):
        consts = {}
        gens = []
        for h, c in units[g0:g0 + RET_GROUP]:
            if h not in consts:
                consts[h] = head_consts(h)
            gens.append(unit(h, c, consts[h]))
        for _ in zip(*gens):
            pass


def _retention(proj, logit_rep, gn_w, n_seq, seq_len, s0=None, emit_state=False):
    t = proj.shape[0]
    has_s0 = s0 is not None
    kern = functools.partial(_ret_kernel, seq_len=seq_len, has_s0=has_s0, emit_state=emit_state)
    in_specs = [pl.BlockSpec((seq_len, QK_W), lambda b: (b, COL_RQ // QK_W)),
                pl.BlockSpec((seq_len, QK_W), lambda b: (b, COL_RK // QK_W)),
                pl.BlockSpec((seq_len, V_W), lambda b: (b, COL_RV // V_W))]
    args = [proj, proj, proj]
    in_specs += [pl.BlockSpec((HEADS, 2, DK), lambda b: (0, 0, 0)),
                 pl.BlockSpec((1, V_W), lambda b: (0, 0))]
    args += [logit_rep, gn_w]
    state_spec = pl.BlockSpec((None, None, 2, HEADS, DK, DV), lambda b: (b, 0, 0, 0, 0, 0))
    if has_s0:
        in_specs.append(state_spec)
        args.append(s0)
    out_specs = [pl.BlockSpec((seq_len, V_W), lambda b: (b, 0))]
    out_shape = [jax.ShapeDtypeStruct((t, V_W), BF16)]
    if emit_state:
        out_specs.append(state_spec)
        out_shape.append(jax.ShapeDtypeStruct((n_seq, 1, 2, HEADS, DK, DV), F32))
    nb = seq_len // BLK
    return pl.pallas_call(
        kern,
        grid=(n_seq,),
        in_specs=in_specs,
        out_specs=out_specs,
        out_shape=out_shape,
        scratch_shapes=[pltpu.VMEM((seq_len, QK_W), F32), pltpu.VMEM((seq_len, QK_W), F32),
                        pltpu.VMEM((HEADS, nb, 2 * DK, DV), BF16)],
        compiler_params=_cparams(("arbitrary",)),
        name="retention",
    )(*args)


N_MASK = 8
(MASK_EQ_F, MASK_EQ_B, MASK_OFFDIAG, MASK_EYE, MASK_BASE, MASK_PAIR0) = 0, 1, 2, 3, 4, 5


def _build_masks(mask_sc, eye_sc):
    ii = lax.broadcasted_iota(jnp.int32, (BLK, BLK), 0)
    jj = lax.broadcasted_iota(jnp.int32, (BLK, BLK), 1)

    def blk(v, size):
        return lax.shift_right_logical(v, size.bit_length() - 1)

    same = blk(ii, GDN_CHUNK) == blk(jj, GDN_CHUNK)
    mask_sc[MASK_EQ_F] = (same & (jj <= ii)).astype(F32)
    mask_sc[MASK_EQ_B] = (same & (jj >= ii)).astype(F32)
    mask_sc[MASK_OFFDIAG] = (ii != jj).astype(F32)
    mask_sc[MASK_EYE] = (ii == jj).astype(F32)
    mask_sc[MASK_BASE] = (blk(ii, TRI_BASE) == blk(jj, TRI_BASE)).astype(F32)
    size, idx = TRI_BASE, MASK_PAIR0
    while size < GDN_CHUNK:
        pair = (blk(ii, 2 * size) == blk(jj, 2 * size)) & (blk(ii, size) != blk(jj, size))
        mask_sc[idx] = pair.astype(F32)
        size, idx = 2 * size, idx + 1
    eye_sc[...] = (ii == jj).astype(BF16)


def _seg_cumsums(x):
    n = x.shape[1]
    pos = lax.broadcasted_iota(jnp.int32, x.shape, 1) & (GDN_CHUNK - 1)
    up, down = x, x
    s = 1
    while s < GDN_CHUNK:
        up = up + jnp.where(pos >= s, pltpu.roll(up, s, 1), 0.0)
        down = down + jnp.where(pos < GDN_CHUNK - s, pltpu.roll(down, n - s, 1), 0.0)
        s *= 2
    return up, down


def _rows_to_cols(rows, eye_sc):
    p1 = rows.astype(BF16).astype(F32)
    r1 = rows - p1
    p2 = r1.astype(BF16).astype(F32)
    p3 = r1 - p2
    pieces = jnp.concatenate([p1, p2, p3], axis=0).astype(BF16)
    c3 = lax.dot_general(eye_sc[...], pieces, (((1,), (1,)), ((), ())), preferred_element_type=F32)
    return c3[:, 0:SUB] + c3[:, SUB:2 * SUB] + c3[:, 2 * SUB:3 * SUB]


def _gdn_kernel(*refs, nb, has_s0, emit_state):
    it = iter(refs)
    xqk_refs = (next(it), next(it))
    xv_refs = (next(it), next(it))
    g_refs = (next(it), next(it))
    alog_ref, dtb_ref = next(it), next(it)
    s0_ref = next(it) if has_s0 else None
    o_refs = (next(it), next(it))
    st_ref = next(it) if emit_state else None
    st_sc, mask_sc, eye_sc = next(it), next(it), next(it)

    b = pl.program_id(0)
    step = pl.program_id(1)

    @pl.when((b == 0) & (step == 0))
    def _():
        _build_masks(mask_sc, eye_sc)

    @pl.when(step == 0)
    def _():
        for d in range(2):
            for h in range(HEADS):
                st_sc[d, h] = s0_ref[d, h] if has_s0 else jnp.zeros((DK, DV), F32)

    blk_of = (step, nb - 1 - step)

    row8 = lax.broadcasted_iota(jnp.int32, (N_GATE, BLK), 0)
    fwd_rows = row8 < HEADS
    logit_b = jnp.where(fwd_rows, g_refs[0][0:N_GATE, :], g_refs[1][0:N_GATE, :])
    logit_a = jnp.where(fwd_rows, g_refs[0][N_GATE:2 * N_GATE, :], g_refs[1][N_GATE:2 * N_GATE, :])
    beta = _sigmoid(logit_b)
    la = -jnp.exp(alog_ref[...]) * _softplus(logit_a + dtb_ref[...])
    up, down = _seg_cumsums(la)
    g8 = jnp.where(fwd_rows, up, down)
    r8 = jnp.where(fwd_rows, down, up) - la
    eg8 = jnp.exp(g8)
    erb8 = jnp.exp(r8) * beta
    etot8 = jnp.exp(up + down - la)
    gcols = _rows_to_cols(g8, eye_sc)

    nck = BLK // GDN_CHUNK

    def chain(d, h):
        c = d * HEADS + h
        q16 = xqk_refs[d][:, h * DK:(h + 1) * DK]
        k16 = xqk_refs[d][:, QK_W + h * DK:QK_W + (h + 1) * DK]
        v16 = xv_refs[d][:, h * DV:(h + 1) * DV]
        q = q16.astype(F32)
        k = k16.astype(F32)
        kk = _mm_nt(k16, k16)
        qk = _mm_nt(q16, k16)
        yield
        g_col = gcols[:, c:c + 1]
        g_row = g8[c:c + 1, :]
        m_eq = mask_sc[MASK_EQ_F if d == 0 else MASK_EQ_B]
        decay_beta = (jnp.exp(jnp.minimum(g_col - g_row, 0.0)) * m_eq) * beta[c:c + 1, :]
        lp = kk * decay_beta * mask_sc[MASK_OFFDIAG]
        sc16 = (qk * decay_beta).astype(BF16)
        qg16 = (q * jnp.exp(g_col)).astype(BF16)
        kt16 = (k.T * erb8[c:c + 1, :]).astype(BF16)

        x = -(lp * mask_sc[MASK_BASE])
        t = mask_sc[MASK_EYE] + x
        for _ in range(TRI_BASE.bit_length() - 2):
            x = _mm(x, x)
            yield
            t = t + _mm(t, x)
            yield
        size, idx = TRI_BASE, MASK_PAIR0
        while size < GDN_CHUNK:
            groups = [slice(r * size, (r + 1) * size) for r in range(BLK // size)]
            late = [(r % 2 == 1) == (d == 0) for r in range(BLK // size)]
            e_late = jnp.concatenate([lp[g, :] * mask_sc[idx, g, :] for g, on in zip(groups, late) if on],
                                     axis=0)
            et = _mm(e_late, t)
            yield
            t_late = jnp.concatenate([t[g, :] for g, on in zip(groups, late) if on], axis=0)
            et_rows = iter(et[j * size:(j + 1) * size, :] for j in range(BLK // size // 2))
            et_full = jnp.concatenate([next(et_rows) if on else jnp.zeros((size, BLK), F32)
                                       for on in late], axis=0)
            t_late = t_late - _mm(t_late, et_full)
            yield
            new_rows = iter(t_late[j * size:(j + 1) * size, :] for j in range(BLK // size // 2))
            t = jnp.concatenate([next(new_rows) if on else t[g, :] for g, on in zip(groups, late)],
                                axis=0)
            size, idx = 2 * size, idx + 1
        u = _mm(t, v16)
        w16 = _mm(t * eg8[c:c + 1, :], k16).astype(BF16)
        yield

        order = range(nck) if d == 0 else range(nck - 1, -1, -1)
        for cc in order:
            rs = slice(cc * GDN_CHUNK, (cc + 1) * GDN_CHUNK)
            s_prev = st_sc[d, h]
            wq = jnp.concatenate([w16[rs, :], qg16[rs, :]], axis=0)
            ws_qs = _mm(wq, s_prev)
            yield
            v_new = (u[rs, :] - ws_qs[0:GDN_CHUNK]).astype(BF16)
            o = ws_qs[GDN_CHUNK:] + _mm(sc16[rs, rs], v_new)
            decay_all = etot8[c:c + 1, cc * GDN_CHUNK:cc * GDN_CHUNK + 1]
            st_sc[d, h] = s_prev * decay_all + _mm(kt16[:, rs], v_new)
            o_refs[d][rs, h * DV:(h + 1) * DV] = o.astype(o_refs[d].dtype)
            yield

    chains = [chain(d, h) for d in range(2) for h in range(HEADS)]
    for _ in zip(*chains):
        pass

    if emit_state:
        @pl.when(step == nb - 1)
        def _():
            for d in range(2):
                for h in range(HEADS):
                    st_ref[d, h] = st_sc[d, h]


def _gdn(proj, gates, a_log, dt_bias, n_seq, seq_len, s0=None, emit_state=False):
    t = proj.shape[0]
    nb = seq_len // BLK
    has_s0 = s0 is not None
    kern = functools.partial(_gdn_kernel, nb=nb, has_s0=has_s0, emit_state=emit_state)

    def fblk(b, s):
        return b * nb + s

    def bblk(b, s):
        return b * nb + nb - 1 - s

    def spec_x(blk, col):
        return pl.BlockSpec((BLK, V_W), lambda b, s: (blk(b, s), col // V_W))

    def spec_g(blk):
        return pl.BlockSpec((2 * N_GATE, BLK), lambda b, s: (0, blk(b, s)))

    const2 = lambda b, s: (0, 0)
    in_specs = [spec_x(fblk, COL_DQK), spec_x(bblk, COL_DQK), spec_x(fblk, COL_DV), spec_x(bblk, COL_DV),
                spec_g(fblk), spec_g(bblk),
                pl.BlockSpec((N_GATE, 1), const2),
                pl.BlockSpec((N_GATE, 1), const2)]
    args = [proj, proj, proj, proj, gates, gates, a_log, dt_bias]
    state_spec = pl.BlockSpec((None, None, 2, HEADS, DK, DV), lambda b, s: (b, 0, 0, 0, 0, 0))
    if has_s0:
        in_specs.append(state_spec)
        args.append(s0)
    out_specs = [pl.BlockSpec((BLK, V_W), lambda b, s: (fblk(b, s), 0)),
                 pl.BlockSpec((BLK, V_W), lambda b, s: (bblk(b, s), 0))]
    out_shape = [jax.ShapeDtypeStruct((t, V_W), BF16), jax.ShapeDtypeStruct((t, V_W), BF16)]
    if emit_state:
        out_specs.append(state_spec)
        out_shape.append(jax.ShapeDtypeStruct((n_seq, 1, 2, HEADS, DK, DV), F32))
    scratch = [pltpu.VMEM((2, HEADS, DK, DV), F32),
               pltpu.VMEM((N_MASK, BLK, BLK), F32),
               pltpu.VMEM((BLK, BLK), BF16)]
    return pl.pallas_call(
        kern,
        grid=(n_seq, nb),
        in_specs=in_specs,
        out_specs=out_specs,
        out_shape=out_shape,
        scratch_shapes=scratch,
        compiler_params=_cparams(("arbitrary", "arbitrary")),
        name="gated_delta",
    )(*args)


def _merge_kernel(x_ref, or_ref, of_ref, ob_ref, rg_ref, dz_ref, gr_ref, gd_ref, mod_ref, nw_ref,
                  dnw_ref, wr_ref, wd_ref, wo_ref, o_ref, *, tm, rows_per_cond):
    i = pl.program_id(0)
    ci = (i * tm) // rows_per_cond
    m = mod_ref[pl.ds(ci, 1), :]
    g1 = m[:, 2 * D_MODEL:3 * D_MODEL]
    y_r = _mm(_silu(rg_ref[...].astype(F32)) * or_ref[...].astype(F32), wr_ref[...])
    dnw = dnw_ref[...]
    heads = []
    for h in range(HEADS):
        cols = slice(h * DV, (h + 1) * DV)
        od = _rms(of_ref[:, cols].astype(F32) + ob_ref[:, cols].astype(F32), dnw)
        heads.append((od * _silu(dz_ref[:, cols].astype(F32))).astype(BF16))
    y_d = jnp.dot(jnp.concatenate(heads, axis=1), wd_ref[...], preferred_element_type=F32)
    merged = _sigmoid(gr_ref[...].astype(F32)) * y_r + _sigmoid(gd_ref[...].astype(F32)) * y_d
    mo = _mm(merged, wo_ref[...])
    o_ref[...] = x_ref[...] + g1 * _rms(mo, nw_ref[1:2, :])


def _merge(x, o_r, o_f, o_b, proj, mod, norm_w, dn_norm_w, w_ret_o, w_dn_o, w_out, rows_per_cond):
    t = x.shape[0]
    tm = 512
    kern = functools.partial(_merge_kernel, tm=tm, rows_per_cond=rows_per_cond)
    row = lambda i: (i, 0)
    const = lambda i: (0, 0)
    wspec = pl.BlockSpec((D_MODEL, D_MODEL), const)
    return pl.pallas_call(
        kern,
        grid=(t // tm,),
        in_specs=[pl.BlockSpec((tm, D_MODEL), row),
                  pl.BlockSpec((tm, V_W), row),
                  pl.BlockSpec((tm, V_W), row),
                  pl.BlockSpec((tm, V_W), row),
                  pl.BlockSpec((tm, V_W), lambda i: (i, COL_RG // V_W)),
                  pl.BlockSpec((tm, V_W), lambda i: (i, COL_DZ // V_W)),
                  pl.BlockSpec((tm, D_MODEL), lambda i: (i, COL_GR // D_MODEL)),
                  pl.BlockSpec((tm, D_MODEL), lambda i: (i, COL_GD // D_MODEL)),
                  pl.BlockSpec(mod.shape, const),
                  pl.BlockSpec(norm_w.shape, const),
                  pl.BlockSpec(dn_norm_w.shape, const),
                  wspec, wspec, wspec],
        out_specs=pl.BlockSpec((tm, D_MODEL), row),
        out_shape=jax.ShapeDtypeStruct((t, D_MODEL), F32),
        compiler_params=_cparams(("arbitrary",)),
        name="merge_out",
    )(x, o_r, o_f, o_b, proj, proj, proj, proj, mod, norm_w, dn_norm_w, w_ret_o, w_dn_o, w_out)


def _ffn_kernel(x_ref, mod_ref, nw_ref, wi_ref, wo_ref, o_ref, *, tm, rows_per_cond):
    i = pl.program_id(0)
    ci = (i * tm) // rows_per_cond
    m = mod_ref[pl.ds(ci, 1), :]
    sh2 = m[:, 3 * D_MODEL:4 * D_MODEL]
    sc2 = m[:, 4 * D_MODEL:5 * D_MODEL]
    g2 = m[:, 5 * D_MODEL:6 * D_MODEL]
    x = x_ref[...]
    hb = (_rms(x, nw_ref[2:3, :]) * (1.0 + sc2) + sh2).astype(BF16)
    f = None
    lo = 0
    for width in FF_CHUNKS:
        gate = jnp.dot(hb, wi_ref[:, lo:lo + width], preferred_element_type=F32)
        up = jnp.dot(hb, wi_ref[:, D_FF + lo:D_FF + lo + width], preferred_element_type=F32)
        part = _mm(_silu(gate) * up, wo_ref[lo:lo + width, :])
        lo += width
        f = part if f is None else f + part
    o_ref[...] = x + g2 * _rms(f, nw_ref[3:4, :])


def _ffn(x, mod, norm_w, w_in, w_out, rows_per_cond):
    t = x.shape[0]
    tm = 512
    kern = functools.partial(_ffn_kernel, tm=tm, rows_per_cond=rows_per_cond)
    row = lambda i: (i, 0)
    const = lambda i: (0, 0)
    resident = pl.Buffered(1)
    return pl.pallas_call(
        kern,
        grid=(t // tm,),
        in_specs=[pl.BlockSpec((tm, D_MODEL), row),
                  pl.BlockSpec(mod.shape, const),
                  pl.BlockSpec(norm_w.shape, const),
                  pl.BlockSpec((D_MODEL, 2 * D_FF), const, pipeline_mode=resident),
                  pl.BlockSpec((D_FF, D_MODEL), const, pipeline_mode=resident)],
        out_specs=pl.BlockSpec((tm, D_MODEL), row),
        out_shape=jax.ShapeDtypeStruct((t, D_MODEL), F32),
        compiler_params=_cparams(("arbitrary",)),
        name="swiglu",
    )(x, mod, norm_w, w_in, w_out)


def _rope_tables(seq_len):
    rows = seq_len // GRID_W
    row_idx = jnp.repeat(jnp.arange(rows, dtype=F32), GRID_W)
    col_idx = (jnp.arange(seq_len) % GRID_W).astype(F32)
    n_freq = DK // 4
    freqs = ROPE_BASE ** (-jnp.arange(n_freq, dtype=F32) / n_freq)
    ang = jnp.concatenate([row_idx[:, None] * freqs, col_idx[:, None] * freqs], axis=-1)
    cos = jnp.repeat(jnp.cos(ang), 2, axis=-1)
    sin = jnp.repeat(jnp.sin(ang), 2, axis=-1)
    sign = jnp.tile(jnp.array([-1.0, 1.0], F32), DK // 2)
    return cos, sin * sign


def _one_path(x, mod, seq_len, n_seq, wts, rope_tabs, s_ret0, s_dn0, emit_state):
    rows_per_cond = seq_len if mod.shape[0] > 1 else x.shape[0]
    proj, gates = _inproj(x, mod, wts["norm_w"], wts["w_a"], wts["w_b"], wts["w_gate"],
                          wts["conv_w"], rows_per_cond, seq_len, rope_tabs)
    ret = _retention(proj, wts["logit_rep"], wts["ret_gn_w"], n_seq, seq_len,
                     s0=s_ret0, emit_state=emit_state)
    gdn = _gdn(proj, gates, wts["a_log"], wts["dt_bias"], n_seq, seq_len,
               s0=s_dn0, emit_state=emit_state)
    x1 = _merge(x, ret[0], gdn[0], gdn[1], proj, mod, wts["norm_w"], wts["dn_norm_w"],
                wts["w_ret_o"], wts["w_dn_o"], wts["w_out"], rows_per_cond)
    y = _ffn(x1, mod, wts["norm_w"], wts["w_ffn_in"], wts["w_ffn_out"], rows_per_cond)
    if emit_state:
        return y, ret[1], gdn[2]
    return y, None, None


def kernel(x_prompt, x_sample, c, state_ret, state_dn, c_ctx, w_mod, b_mod, norm_w, w_in, conv_w,
           ret_decay_logit, ret_gn_w, dn_a_log, dn_dt_bias, dn_norm_w, w_ret_o, w_dn_o, w_out,
           w_ffn_in, w_ffn_out):
    n_ctx, l_ctx, _ = x_prompt.shape
    n_lat, l_lat, _ = x_sample.shape
    assert w_mod.shape[0] == 1, "single-layer kernel"

    cond8 = jnp.zeros((8, D_MODEL), F32).at[0].set(c_ctx).at[1:1 + n_lat].set(c)
    mod = _modulation(cond8, w_mod[0], b_mod)
    mod_ctx, mod_lat = mod[0:1], mod[1:1 + n_lat]

    w_in_t = w_in[0].T
    c_gr = COL_GATE + 2 * N_GATE
    wts = {
        "norm_w": norm_w[0],
        "w_a": w_in_t[:COL_GATE].astype(BF16),
        "w_b": w_in_t[c_gr:].astype(BF16),
        "w_gate": w_in_t[COL_GATE:c_gr].astype(BF16),
        "logit_rep": jnp.broadcast_to(ret_decay_logit[0].T[:, :, None], (HEADS, 2, DK)),
        "ret_gn_w": ret_gn_w,
        "conv_w": conv_w[0],
        "a_log": dn_a_log.reshape(N_GATE, 1),
        "dt_bias": dn_dt_bias.reshape(N_GATE, 1),
        "dn_norm_w": dn_norm_w,
        "w_ret_o": w_ret_o[0].astype(BF16),
        "w_dn_o": w_dn_o[0].astype(BF16),
        "w_out": w_out[0].astype(BF16),
        "w_ffn_in": w_ffn_in[0].astype(BF16),
        "w_ffn_out": w_ffn_out[0].astype(BF16),
    }

    y_p, s_ret, s_dn = _one_path(x_prompt.reshape(n_ctx * l_ctx, D_MODEL), mod_ctx, l_ctx, n_ctx,
                                 wts, None, None, None, True)
    y_s, _, _ = _one_path(x_sample.reshape(n_lat * l_lat, D_MODEL), mod_lat, l_lat, n_lat,
                          wts, _rope_tables(l_lat), state_ret, state_dn, False)
    return (y_p.reshape(x_prompt.shape), y_s.reshape(x_sample.shape), s_ret, s_dn)
```

```python
import functools

import jax
import jax.numpy as jnp
from jax import lax
from jax.experimental import pallas as pl
from jax.experimental.pallas import tpu as pltpu

F32 = jnp.float32
BF16 = jnp.bfloat16

D_MODEL = 1024
HEADS = 4
DK = 128
DV = 256
QK_W = HEADS * DK
V_W = HEADS * DV
GDN_CHUNK = 64
TRI_BASE = 8
BLK = 256
SUB = 8
LANES = 128
HALO = 16
CONV_WIDTH = 3
GRID_W = 64
ROPE_BASE = 10000.0
EPS = 1e-6
D_FF = 2816
MXU_DIM = 256
FF_CHUNKS = (6 * MXU_DIM, 5 * MXU_DIM)
assert sum(FF_CHUNKS) == D_FF
N_GATE = 2 * HEADS

COL_RQ, COL_RK, COL_RV, COL_RG = 0, 512, 1024, 2048
COL_DQK, COL_DV, COL_DZ = 3072, 4096, 5120
COL_GATE = 6144
COL_GR, COL_GD = 6144, 7168
PROJ_W = 8192
PROJ_TN = 2048

VMEM_LIMIT = 56 * 1024 * 1024


def _cparams(sem):
    return pltpu.CompilerParams(dimension_semantics=sem, vmem_limit_bytes=VMEM_LIMIT)


def _mm(a, b):
    return jnp.dot(a.astype(BF16), b.astype(BF16), preferred_element_type=F32)


def _mm_nt(a, b):
    return lax.dot_general(a.astype(BF16), b.astype(BF16), (((1,), (1,)), ((), ())),
                           preferred_element_type=F32)


def _mm_tn(a, b):
    return lax.dot_general(a.astype(BF16), b.astype(BF16), (((0,), (0,)), ((), ())),
                           preferred_element_type=F32)


def _sigmoid(x):
    return 1.0 / (1.0 + jnp.exp(-x))


def _silu(x):
    return x * _sigmoid(x)


def _softplus(x):
    return jnp.maximum(x, 0.0) + jnp.log(1.0 + jnp.exp(-jnp.abs(x)))


def _log_sigmoid(x):
    return -_softplus(-x)


def _rms(x, w):
    return x * lax.rsqrt(jnp.mean(x * x, axis=-1, keepdims=True) + EPS) * w


def _mod_kernel(c_ref, w_ref, b_ref, o_ref):
    o_ref[...] = _mm(_silu(c_ref[...]), w_ref[...]) + b_ref[...]


def _modulation(cond8, w_mod, b_mod):
    n = w_mod.shape[1]
    tn = 1536
    return pl.pallas_call(
        _mod_kernel,
        grid=(n // tn,),
        in_specs=[pl.BlockSpec((8, D_MODEL), lambda j: (0, 0)),
                  pl.BlockSpec((D_MODEL, tn), lambda j: (0, j)),
                  pl.BlockSpec((1, tn), lambda j: (0, j))],
        out_specs=pl.BlockSpec((8, tn), lambda j: (0, j)),
        out_shape=jax.ShapeDtypeStruct((8, n), F32),
        compiler_params=_cparams(("arbitrary",)),
        name="modulation",
    )(cond8, w_mod, b_mod)


CHUNK_W = 256
SUBROWS = 128
PLAIN, RET_Q, RET_K, DN_Q, DN_K, DN_V = range(6)
TILE_KINDS = (
    (RET_Q, RET_Q, RET_K, RET_K, PLAIN, PLAIN, PLAIN, PLAIN),
    (PLAIN, PLAIN, PLAIN, PLAIN, DN_Q, DN_Q, DN_K, DN_K),
    (DN_V, DN_V, DN_V, DN_V, PLAIN, PLAIN, PLAIN, PLAIN),
    (PLAIN,) * 8,
)


def _pair_swap(x):
    lane = lax.broadcasted_iota(jnp.int32, x.shape, 1)
    nxt = pltpu.roll(x, x.shape[1] - 1, 1)
    prv = pltpu.roll(x, 1, 1)
    return jnp.where((lane & 1) == 0, nxt, prv)


def _inproj_kernel(*refs, tm, rows_per_cond, n_a, seq_len, rope):
    it = iter(refs)
    x_ref, xp_ref, xn_ref, mod_ref, nw_ref, wa_ref, wb_ref, wg_ref, cw_ref = (next(it) for _ in range(9))
    cos_ref = next(it) if rope else None
    sin_ref = next(it) if rope else None
    o_ref, g_ref, h_sc, p_sc = next(it), next(it), next(it), next(it)

    i = pl.program_id(0)
    j = pl.program_id(1)
    t0 = i * tm
    tile_rows = slice(HALO, HALO + tm)

    @pl.when(j == 0)
    def _():
        ci = t0 // rows_per_cond
        m = mod_ref[pl.ds(ci, 1), :]
        sh1 = m[:, 0:D_MODEL]
        sc1 = m[:, D_MODEL:2 * D_MODEL]

        def pre(x):
            return (_rms(x, nw_ref[0:1, :]) * (1.0 + sc1) + sh1).astype(BF16)

        hb = pre(x_ref[...])
        h_sc[tile_rows, :] = hb
        h_sc[0:HALO, :] = pre(xp_ref[...])
        h_sc[HALO + tm:2 * HALO + tm, :] = pre(xn_ref[...])
        g_ref[...] = _mm_nt(wg_ref[...], hb)

    seq_start = (t0 & (seq_len - 1)) == 0
    seq_end = ((t0 + tm) & (seq_len - 1)) == 0

    def conv_silu(win, r0, cw_lo):
        width = win.shape[1]
        inner = slice(SUB, SUB + SUBROWS)
        cur = win[inner]
        prev = pltpu.roll(win, 1, 0)[inner]
        nxt = pltpu.roll(win, win.shape[0] - 1, 0)[inner]
        row = lax.broadcasted_iota(jnp.int32, cur.shape, 0)
        if seq_len >= tm:
            if r0 == 0:
                prev = jnp.where((row == 0) & seq_start, 0.0, prev)
            if r0 + SUBROWS == tm:
                nxt = jnp.where((row == SUBROWS - 1) & seq_end, 0.0, nxt)
        else:
            if r0 % seq_len == 0:
                prev = jnp.where(row == 0, 0.0, prev)
            if (r0 + SUBROWS) % seq_len == 0:
                nxt = jnp.where(row == SUBROWS - 1, 0.0, nxt)
        cw = cw_ref[:, cw_lo:cw_lo + width]
        return _silu(prev * cw[0:1, :] + cur * cw[1:2, :] + nxt * cw[2:3, :])

    def l2n_heads(y, scale):
        outs = []
        for lo in range(0, y.shape[1], DK):
            yh = y[:, lo:lo + DK]
            outs.append(yh * (lax.rsqrt(jnp.sum(yh * yh, axis=-1, keepdims=True) + EPS) * scale))
        return jnp.concatenate(outs, axis=1)

    def rotary(p, r0):
        reps = p.shape[1] // DK
        cos = jnp.concatenate([cos_ref[r0:r0 + SUBROWS, :]] * reps, axis=1)
        sin = jnp.concatenate([sin_ref[r0:r0 + SUBROWS, :]] * reps, axis=1)
        return p * cos + _pair_swap(p) * sin

    def tile(w_ref, kinds):
        n_staged = 0
        for c, kind in enumerate(kinds):
            lo = c * CHUNK_W
            w = w_ref[lo:lo + CHUNK_W, :]
            conv = kind in (DN_Q, DN_K, DN_V)
            if not conv and not (rope and kind in (RET_Q, RET_K)):
                y = _mm_nt(h_sc[tile_rows, :], w)
                if kind == RET_Q:
                    y = y * (DK ** -0.5)
                o_ref[:, lo:lo + CHUNK_W] = y.astype(o_ref.dtype)
                continue
            buf = p_sc.at[n_staged % 2]
            n_staged += 1
            if conv:
                buf[...] = _mm_nt(h_sc[...], w)
                cw_lo = {DN_Q: 0, DN_K: QK_W, DN_V: 2 * QK_W}[kind] + (lo % (2 * QK_W) if kind == DN_V
                                                                      else lo % QK_W)
            else:
                buf[tile_rows, :] = _mm_nt(h_sc[tile_rows, :], w)
            for r0 in range(0, tm, SUBROWS):
                if conv:
                    y = conv_silu(buf[HALO + r0 - SUB:HALO + r0 + SUBROWS + SUB, :], r0, cw_lo)
                    if kind == DN_Q:
                        y = l2n_heads(y, DK ** -0.5)
                    elif kind == DN_K:
                        y = l2n_heads(y, 1.0)
                else:
                    y = buf[HALO + r0:HALO + r0 + SUBROWS, :]
                    if kind == RET_Q:
                        y = y * (DK ** -0.5)
                    y = rotary(y, r0)
                o_ref[r0:r0 + SUBROWS, lo:lo + CHUNK_W] = y.astype(o_ref.dtype)

    for jt, kinds in enumerate(TILE_KINDS):
        w_ref = wa_ref if jt < n_a else wb_ref

        @pl.when(j == jt)
        def _(w_ref=w_ref, kinds=kinds):
            tile(w_ref, kinds)


def _inproj(x, mod, norm_w, w_a, w_b, w_gate, conv_w, rows_per_cond, seq_len, rope_tabs):
    t = x.shape[0]
    tm, tn = 1024, PROJ_TN
    n_a = w_a.shape[0] // tn
    rope = rope_tabs is not None
    assert tn == len(TILE_KINDS[0]) * CHUNK_W and PROJ_W == len(TILE_KINDS) * tn
    assert seq_len & (seq_len - 1) == 0 and (tm % seq_len == 0 or seq_len % tm == 0)
    kern = functools.partial(_inproj_kernel, tm=tm, rows_per_cond=rows_per_cond, n_a=n_a,
                             seq_len=seq_len, rope=rope)
    halo_per_tile = tm // HALO
    n_halo = t // HALO
    in_specs = [pl.BlockSpec((tm, D_MODEL), lambda i, j: (i, 0)),
                pl.BlockSpec((HALO, D_MODEL), lambda i, j: (jnp.maximum(i * halo_per_tile - 1, 0), 0)),
                pl.BlockSpec((HALO, D_MODEL),
                             lambda i, j: (jnp.minimum((i + 1) * halo_per_tile, n_halo - 1), 0)),
                pl.BlockSpec(mod.shape, lambda i, j: (0, 0)),
                pl.BlockSpec(norm_w.shape, lambda i, j: (0, 0)),
                pl.BlockSpec((tn, D_MODEL), lambda i, j: (jnp.minimum(j, n_a - 1), 0)),
                pl.BlockSpec((tn, D_MODEL), lambda i, j: (jnp.maximum(j - n_a, 0), 0)),
                pl.BlockSpec((2 * N_GATE, D_MODEL), lambda i, j: (0, 0)),
                pl.BlockSpec(conv_w.shape, lambda i, j: (0, 0))]
    args = [x, x, x, mod, norm_w, w_a, w_b, w_gate, conv_w]
    if rope:
        tiles_per_seq = max(seq_len // tm, 1)
        in_specs += [pl.BlockSpec((tm, DK), lambda i, j: (i % tiles_per_seq, 0))] * 2
        args += list(rope_tabs)
    return pl.pallas_call(
        kern,
        grid=(t // tm, PROJ_W // tn),
        in_specs=in_specs,
        out_specs=[pl.BlockSpec((tm, tn), lambda i, j: (i, j)),
                   pl.BlockSpec((2 * N_GATE, tm), lambda i, j: (0, i))],
        out_shape=[jax.ShapeDtypeStruct((t, PROJ_W), BF16),
                   jax.ShapeDtypeStruct((2 * N_GATE, t), F32)],
        scratch_shapes=[pltpu.VMEM((tm + 2 * HALO, D_MODEL), BF16),
                        pltpu.VMEM((2, tm + 2 * HALO, CHUNK_W), F32)],
        compiler_params=_cparams(("arbitrary", "arbitrary")),
        name="inproj",
    )(*args)


RET_GROUP = 8


def _ret_kernel(*refs, seq_len, has_s0, emit_state):
    it = iter(refs)
    q_ref, k_ref, v_ref = next(it), next(it), next(it)
    logit_ref, gnw_ref = next(it), next(it)
    s0_ref = next(it) if has_s0 else None
    o_ref = next(it)
    st_ref = next(it) if emit_state else None
    q_sc, k_sc, st_sc = next(it), next(it), next(it)

    nb = seq_len // BLK
    cross = has_s0 or nb > 1

    def rows(c):
        return slice(c * BLK, (c + 1) * BLK)

    def qk_cols(h):
        return slice(h * DK, (h + 1) * DK)

    def v_cols(h):
        return slice(h * DV, (h + 1) * DV)

    q_sc[...] = q_ref[...].astype(F32)
    k_sc[...] = k_ref[...].astype(F32)

    pos = lax.broadcasted_iota(jnp.int32, (BLK, DK), 0).astype(F32)

    def log_gammas(h):
        lg = _log_sigmoid(logit_ref[h])
        return lg[0:1, :], lg[1:2, :]

    for h in range(HEADS):
        lgf, lgb = log_gammas(h)
        kdec_f = jnp.exp(lgf * (BLK - 1.0 - pos))
        kdec_b = jnp.exp(lgb * pos)
        cdec_f = jnp.exp(lgf[:, 0:1] * float(BLK))
        cdec_b = jnp.exp(lgb[:, 0:1] * float(BLK))
        af = {c: _mm_tn(k_sc[rows(c), qk_cols(h)] * kdec_f, v_ref[rows(c), v_cols(h)])
              for c in range(nb) if c < nb - 1 or emit_state}
        ab = {c: _mm_tn(k_sc[rows(c), qk_cols(h)] * kdec_b, v_ref[rows(c), v_cols(h)])
              for c in range(nb) if c > 0 or emit_state}
        s = s0_ref[0, h] if has_s0 else jnp.zeros((DK, DV), F32)
        for c in range(nb):
            if cross:
                st_sc[h, c, 0:DK, :] = s.astype(BF16)
            if c in af:
                s = s * cdec_f + af[c]
        if emit_state:
            st_ref[0, h] = s
        s = s0_ref[1, h] if has_s0 else jnp.zeros((DK, DV), F32)
        for c in range(nb - 1, -1, -1):
            if cross:
                st_sc[h, c, DK:2 * DK, :] = s.astype(BF16)
            if c in ab:
                s = s * cdec_b + ab[c]
        if emit_state:
            st_ref[1, h] = s

    ii = lax.broadcasted_iota(jnp.int32, (BLK, BLK), 0)
    jj = lax.broadcasted_iota(jnp.int32, (BLK, BLK), 1)
    diff = (ii - jj).astype(F32)
    pos2 = lax.broadcasted_iota(jnp.int32, (BLK, 2 * DK), 0).astype(F32)
    first_half = lax.broadcasted_iota(jnp.int32, (BLK, 2 * DK), 1) < DK

    def head_consts(h):
        lgf, lgb = log_gammas(h)
        lgf1, lgb1 = lgf[:, 0:1], lgb[:, 0:1]
        dmat = (jnp.where(diff >= 0, jnp.exp(lgf1 * jnp.maximum(diff, 0.0)), 0.0)
                + jnp.where(diff <= 0, jnp.exp(lgb1 * jnp.maximum(-diff, 0.0)), 0.0))
        qdec = jnp.where(first_half, jnp.exp(lgf1 * (pos2 + 1.0)), jnp.exp(lgb1 * (BLK - pos2)))
        return dmat, qdec

    def unit(h, c, consts):
        dmat, qdec = consts
        qc = q_sc[rows(c), qk_cols(h)]
        s = _mm_nt(qc, k_sc[rows(c), qk_cols(h)])
        yield
        o = _mm(s * dmat, v_ref[rows(c), v_cols(h)])
        if cross:
            o_cross = _mm(jnp.concatenate([qc, qc], axis=1) * qdec, st_sc[h, c])
        yield
        if cross:
            o = o + o_cross
        mu = jnp.mean(o, axis=-1, keepdims=True)
        oc = o - mu
        var = jnp.mean(oc * oc, axis=-1, keepdims=True)
        o_ref[rows(c), v_cols(h)] = (oc * lax.rsqrt(var + EPS) * gnw_ref[:, v_cols(h)]).astype(o_ref.dtype)
        yield

    units = [(h, c) for h in range(HEADS) for c in range(nb)]
    for g0 in range(0, len(units), RET_GROUP):
        consts = {}
        gens = []
        for h, c in units[g0:g0 + RET_GROUP]:
            if h not in consts:
                consts[h] = head_consts(h)
            gens.append(unit(h, c, consts[h]))
        for _ in zip(*gens):
            pass


def _retention(proj, logit_rep, gn_w, n_seq, seq_len, s0=None, emit_state=False):
    t = proj.shape[0]
    has_s0 = s0 is not None
    kern = functools.partial(_ret_kernel, seq_len=seq_len, has_s0=has_s0, emit_state=emit_state)
    in_specs = [pl.BlockSpec((seq_len, QK_W), lambda b: (b, COL_RQ // QK_W)),
                pl.BlockSpec((seq_len, QK_W), lambda b: (b, COL_RK // QK_W)),
                pl.BlockSpec((seq_len, V_W), lambda b: (b, COL_RV // V_W))]
    args = [proj, proj, proj]
    in_specs += [pl.BlockSpec((HEADS, 2, DK), lambda b: (0, 0, 0)),
                 pl.BlockSpec((1, V_W), lambda b: (0, 0))]
    args += [logit_rep, gn_w]
    state_spec = pl.BlockSpec((None, None, 2, HEADS, DK, DV), lambda b: (b, 0, 0, 0, 0, 0))
    if has_s0:
        in_specs.append(state_spec)
        args.append(s0)
    out_specs = [pl.BlockSpec((seq_len, V_W), lambda b: (b, 0))]
    out_shape = [jax.ShapeDtypeStruct((t, V_W), BF16)]
    if emit_state:
        out_specs.append(state_spec)
        out_shape.append(jax.ShapeDtypeStruct((n_seq, 1, 2, HEADS, DK, DV), F32))
    nb = seq_len // BLK
    return pl.pallas_call(
        kern,
        grid=(n_seq,),
        in_specs=in_specs,
        out_specs=out_specs,
        out_shape=out_shape,
        scratch_shapes=[pltpu.VMEM((seq_len, QK_W), F32), pltpu.VMEM((seq_len, QK_W), F32),
                        pltpu.VMEM((HEADS, nb, 2 * DK, DV), BF16)],
        compiler_params=_cparams(("arbitrary",)),
        name="retention",
    )(*args)


N_MASK = 8
(MASK_EQ_F, MASK_EQ_B, MASK_OFFDIAG, MASK_EYE, MASK_BASE, MASK_PAIR0) = 0, 1, 2, 3, 4, 5


def _build_masks(mask_sc, eye_sc):
    ii = lax.broadcasted_iota(jnp.int32, (BLK, BLK), 0)
    jj = lax.broadcasted_iota(jnp.int32, (BLK, BLK), 1)

    def blk(v, size):
        return lax.shift_right_logical(v, size.bit_length() - 1)

    same = blk(ii, GDN_CHUNK) == blk(jj, GDN_CHUNK)
    mask_sc[MASK_EQ_F] = (same & (jj <= ii)).astype(F32)
    mask_sc[MASK_EQ_B] = (same & (jj >= ii)).astype(F32)
    mask_sc[MASK_OFFDIAG] = (ii != jj).astype(F32)
    mask_sc[MASK_EYE] = (ii == jj).astype(F32)
    mask_sc[MASK_BASE] = (blk(ii, TRI_BASE) == blk(jj, TRI_BASE)).astype(F32)
    size, idx = TRI_BASE, MASK_PAIR0
    while size < GDN_CHUNK:
        pair = (blk(ii, 2 * size) == blk(jj, 2 * size)) & (blk(ii, size) != blk(jj, size))
        mask_sc[idx] = pair.astype(F32)
        size, idx = 2 * size, idx + 1
    eye_sc[...] = (ii == jj).astype(BF16)


def _seg_cumsums(x):
    n = x.shape[1]
    pos = lax.broadcasted_iota(jnp.int32, x.shape, 1) & (GDN_CHUNK - 1)
    up, down = x, x
    s = 1
    while s < GDN_CHUNK:
        up = up + jnp.where(pos >= s, pltpu.roll(up, s, 1), 0.0)
        down = down + jnp.where(pos < GDN_CHUNK - s, pltpu.roll(down, n - s, 1), 0.0)
        s *= 2
    return up, down


def _rows_to_cols(rows, eye_sc):
    p1 = rows.astype(BF16).astype(F32)
    r1 = rows - p1
    p2 = r1.astype(BF16).astype(F32)
    p3 = r1 - p2
    pieces = jnp.concatenate([p1, p2, p3], axis=0).astype(BF16)
    c3 = lax.dot_general(eye_sc[...], pieces, (((1,), (1,)), ((), ())), preferred_element_type=F32)
    return c3[:, 0:SUB] + c3[:, SUB:2 * SUB] + c3[:, 2 * SUB:3 * SUB]


def _gdn_kernel(*refs, nb, has_s0, emit_state):
    it = iter(refs)
    xqk_refs = (next(it), next(it))
    xv_refs = (next(it), next(it))
    g_refs = (next(it), next(it))
    alog_ref, dtb_ref = next(it), next(it)
    s0_ref = next(it) if has_s0 else None
    o_refs = (next(it), next(it))
    st_ref = next(it) if emit_state else None
    st_sc, mask_sc, eye_sc = next(it), next(it), next(it)

    b = pl.program_id(0)
    step = pl.program_id(1)

    @pl.when((b == 0) & (step == 0))
    def _():
        _build_masks(mask_sc, eye_sc)

    @pl.when(step == 0)
    def _():
        for d in range(2):
            for h in range(HEADS):
                st_sc[d, h] = s0_ref[d, h] if has_s0 else jnp.zeros((DK, DV), F32)

    nck = BLK // GDN_CHUNK

    def gates():
        row8 = lax.broadcasted_iota(jnp.int32, (N_GATE, BLK), 0)
        fwd_rows = row8 < HEADS
        logit_b = jnp.where(fwd_rows, g_refs[0][0:N_GATE, :], g_refs[1][0:N_GATE, :])
        logit_a = jnp.where(fwd_rows, g_refs[0][N_GATE:2 * N_GATE, :], g_refs[1][N_GATE:2 * N_GATE, :])
        beta = _sigmoid(logit_b)
        la = -jnp.exp(alog_ref[...]) * _softplus(logit_a + dtb_ref[...])
        up, down = _seg_cumsums(la)
        g8 = jnp.where(fwd_rows, up, down)
        r8 = jnp.where(fwd_rows, down, up) - la
        etot8 = jnp.exp(up + down - la)
        return dict(beta=beta, g8=g8, eg8=jnp.exp(g8), erb8=jnp.exp(r8) * beta, etot8=etot8,
                    gcols=_rows_to_cols(g8, eye_sc))

    gate = {}

    def chain(d, h):
        c = d * HEADS + h
        q16 = xqk_refs[d][:, h * DK:(h + 1) * DK]
        k16 = xqk_refs[d][:, QK_W + h * DK:QK_W + (h + 1) * DK]
        v16 = xv_refs[d][:, h * DV:(h + 1) * DV]
        q = q16.astype(F32)
        k = k16.astype(F32)
        kk = _mm_nt(k16, k16)
        qk = _mm_nt(q16, k16)
        yield
        g_col = gate["gcols"][:, c:c + 1]
        g_row = gate["g8"][c:c + 1, :]
        m_eq = mask_sc[MASK_EQ_F if d == 0 else MASK_EQ_B]
        decay_beta = (jnp.exp(jnp.minimum(g_col - g_row, 0.0)) * m_eq) * gate["beta"][c:c + 1, :]
        lp = kk * decay_beta * mask_sc[MASK_OFFDIAG]
        sc16 = (qk * decay_beta).astype(BF16)
        qg16 = (q * jnp.exp(g_col)).astype(BF16)
        kt16 = (k.T * gate["erb8"][c:c + 1, :]).astype(BF16)

        x = -(lp * mask_sc[MASK_BASE])
        t = mask_sc[MASK_EYE] + x
        for _ in range(TRI_BASE.bit_length() - 2):
            x = _mm(x, x)
            yield
            t = t + _mm(t, x)
            yield
        size, idx = TRI_BASE, MASK_PAIR0
        while size < GDN_CHUNK:
            groups = [slice(r * size, (r + 1) * size) for r in range(BLK // size)]
            late = [(r % 2 == 1) == (d == 0) for r in range(BLK // size)]
            e_late = jnp.concatenate([lp[g, :] * mask_sc[idx, g, :] for g, on in zip(groups, late) if on],
                                     axis=0)
            et = _mm(e_late, t)
            yield
            t_late = jnp.concatenate([t[g, :] for g, on in zip(groups, late) if on], axis=0)
            et_rows = iter(et[j * size:(j + 1) * size, :] for j in range(BLK // size // 2))
            et_full = jnp.concatenate([next(et_rows) if on else jnp.zeros((size, BLK), F32)
                                       for on in late], axis=0)
            t_late = t_late - _mm(t_late, et_full)
            yield
            new_rows = iter(t_late[j * size:(j + 1) * size, :] for j in range(BLK // size // 2))
            t = jnp.concatenate([next(new_rows) if on else t[g, :] for g, on in zip(groups, late)],
                                axis=0)
            size, idx = 2 * size, idx + 1
        u = _mm(t, v16)
        w16 = _mm(t * gate["eg8"][c:c + 1, :], k16).astype(BF16)
        yield

        order = range(nck) if d == 0 else range(nck - 1, -1, -1)
        for cc in order:
            rs = slice(cc * GDN_CHUNK, (cc + 1) * GDN_CHUNK)
            s_prev = st_sc[d, h]
            wq = jnp.concatenate([w16[rs, :], qg16[rs, :]], axis=0)
            ws_qs = _mm(wq, s_prev)
            yield
            v_new = (u[rs, :] - ws_qs[0:GDN_CHUNK]).astype(BF16)
            sk = _mm(jnp.concatenate([sc16[rs, rs], kt16[:, rs]], axis=0), v_new)
            o = ws_qs[GDN_CHUNK:] + sk[0:GDN_CHUNK]
            decay_all = gate["etot8"][c:c + 1, cc * GDN_CHUNK:cc * GDN_CHUNK + 1]
            st_sc[d, h] = s_prev * decay_all + sk[GDN_CHUNK:]
            o_refs[d][rs, h * DV:(h + 1) * DV] = o.astype(o_refs[d].dtype)
            yield

    chains = [chain(d, h) for d in range(2) for h in range(HEADS)]
    for ch in chains:
        next(ch)
    gate.update(gates())
    for _ in zip(*chains):
        pass

    if emit_state:
        @pl.when(step == nb - 1)
        def _():
            for d in range(2):
                for h in range(HEADS):
                    st_ref[d, h] = st_sc[d, h]


def _gdn(proj, gates, a_log, dt_bias, n_seq, seq_len, s0=None, emit_state=False):
    t = proj.shape[0]
    nb = seq_len // BLK
    has_s0 = s0 is not None
    kern = functools.partial(_gdn_kernel, nb=nb, has_s0=has_s0, emit_state=emit_state)

    def fblk(b, s):
        return b * nb + s

    def bblk(b, s):
        return b * nb + nb - 1 - s

    def spec_x(blk, col):
        return pl.BlockSpec((BLK, V_W), lambda b, s: (blk(b, s), col // V_W))

    def spec_g(blk):
        return pl.BlockSpec((2 * N_GATE, BLK), lambda b, s: (0, blk(b, s)))

    const2 = lambda b, s: (0, 0)
    in_specs = [spec_x(fblk, COL_DQK), spec_x(bblk, COL_DQK), spec_x(fblk, COL_DV), spec_x(bblk, COL_DV),
                spec_g(fblk), spec_g(bblk),
                pl.BlockSpec((N_GATE, 1), const2),
                pl.BlockSpec((N_GATE, 1), const2)]
    args = [proj, proj, proj, proj, gates, gates, a_log, dt_bias]
    state_spec = pl.BlockSpec((None, None, 2, HEADS, DK, DV), lambda b, s: (b, 0, 0, 0, 0, 0))
    if has_s0:
        in_specs.append(state_spec)
        args.append(s0)
    out_specs = [pl.BlockSpec((BLK, V_W), lambda b, s: (fblk(b, s), 0)),
                 pl.BlockSpec((BLK, V_W), lambda b, s: (bblk(b, s), 0))]
    out_shape = [jax.ShapeDtypeStruct((t, V_W), BF16), jax.ShapeDtypeStruct((t, V_W), BF16)]
    if emit_state:
        out_specs.append(state_spec)
        out_shape.append(jax.ShapeDtypeStruct((n_seq, 1, 2, HEADS, DK, DV), F32))
    scratch = [pltpu.VMEM((2, HEADS, DK, DV), F32),
               pltpu.VMEM((N_MASK, BLK, BLK), F32),
               pltpu.VMEM((BLK, BLK), BF16)]
    return pl.pallas_call(
        kern,
        grid=(n_seq, nb),
        in_specs=in_specs,
        out_specs=out_specs,
        out_shape=out_shape,
        scratch_shapes=scratch,
        compiler_params=_cparams(("arbitrary", "arbitrary")),
        name="gated_delta",
    )(*args)


def _merge_kernel(x_ref, or_ref, of_ref, ob_ref, rg_ref, dz_ref, gr_ref, gd_ref, mod_ref, nw_ref,
                  dnw_ref, wr_ref, wd_ref, wo_ref, o_ref, *, tm, rows_per_cond):
    i = pl.program_id(0)
    ci = (i * tm) // rows_per_cond
    m = mod_ref[pl.ds(ci, 1), :]
    g1 = m[:, 2 * D_MODEL:3 * D_MODEL]
    y_r = _mm(_silu(rg_ref[...].astype(F32)) * or_ref[...].astype(F32), wr_ref[...])
    dnw = dnw_ref[...]
    heads = []
    for h in range(HEADS):
        cols = slice(h * DV, (h + 1) * DV)
        od = _rms(of_ref[:, cols].astype(F32) + ob_ref[:, cols].astype(F32), dnw)
        heads.append((od * _silu(dz_ref[:, cols].astype(F32))).astype(BF16))
    y_d = jnp.dot(jnp.concatenate(heads, axis=1), wd_ref[...], preferred_element_type=F32)
    merged = _sigmoid(gr_ref[...].astype(F32)) * y_r + _sigmoid(gd_ref[...].astype(F32)) * y_d
    mo = _mm(merged, wo_ref[...])
    o_ref[...] = x_ref[...] + g1 * _rms(mo, nw_ref[1:2, :])


def _merge(x, o_r, o_f, o_b, proj, mod, norm_w, dn_norm_w, w_ret_o, w_dn_o, w_out, rows_per_cond):
    t = x.shape[0]
    tm = 512
    kern = functools.partial(_merge_kernel, tm=tm, rows_per_cond=rows_per_cond)
    row = lambda i: (i, 0)
    const = lambda i: (0, 0)
    wspec = pl.BlockSpec((D_MODEL, D_MODEL), const)
    return pl.pallas_call(
        kern,
        grid=(t // tm,),
        in_specs=[pl.BlockSpec((tm, D_MODEL), row),
                  pl.BlockSpec((tm, V_W), row),
                  pl.BlockSpec((tm, V_W), row),
                  pl.BlockSpec((tm, V_W), row),
                  pl.BlockSpec((tm, V_W), lambda i: (i, COL_RG // V_W)),
                  pl.BlockSpec((tm, V_W), lambda i: (i, COL_DZ // V_W)),
                  pl.BlockSpec((tm, D_MODEL), lambda i: (i, COL_GR // D_MODEL)),
                  pl.BlockSpec((tm, D_MODEL), lambda i: (i, COL_GD // D_MODEL)),
                  pl.BlockSpec(mod.shape, const),
                  pl.BlockSpec(norm_w.shape, const),
                  pl.BlockSpec(dn_norm_w.shape, const),
                  wspec, wspec, wspec],
        out_specs=pl.BlockSpec((tm, D_MODEL), row),
        out_shape=jax.ShapeDtypeStruct((t, D_MODEL), F32),
        compiler_params=_cparams(("arbitrary",)),
        name="merge_out",
    )(x, o_r, o_f, o_b, proj, proj, proj, proj, mod, norm_w, dn_norm_w, w_ret_o, w_dn_o, w_out)


def _ffn_kernel(x_ref, mod_ref, nw_ref, wi_ref, wo_ref, o_ref, *, tm, rows_per_cond):
    i = pl.program_id(0)
    ci = (i * tm) // rows_per_cond
    m = mod_ref[pl.ds(ci, 1), :]
    sh2 = m[:, 3 * D_MODEL:4 * D_MODEL]
    sc2 = m[:, 4 * D_MODEL:5 * D_MODEL]
    g2 = m[:, 5 * D_MODEL:6 * D_MODEL]
    x = x_ref[...]
    hb = (_rms(x, nw_ref[2:3, :]) * (1.0 + sc2) + sh2).astype(BF16)
    f = None
    lo = 0
    for width in FF_CHUNKS:
        gate = jnp.dot(hb, wi_ref[:, lo:lo + width], preferred_element_type=F32)
        up = jnp.dot(hb, wi_ref[:, D_FF + lo:D_FF + lo + width], preferred_element_type=F32)
        part = _mm(_silu(gate) * up, wo_ref[lo:lo + width, :])
        lo += width
        f = part if f is None else f + part
    o_ref[...] = x + g2 * _rms(f, nw_ref[3:4, :])


def _ffn(x, mod, norm_w, w_in, w_out, rows_per_cond):
    t = x.shape[0]
    tm = 512
    kern = functools.partial(_ffn_kernel, tm=tm, rows_per_cond=rows_per_cond)
    row = lambda i: (i, 0)
    const = lambda i: (0, 0)
    resident = pl.Buffered(1)
    return pl.pallas_call(
        kern,
        grid=(t // tm,),
        in_specs=[pl.BlockSpec((tm, D_MODEL), row),
                  pl.BlockSpec(mod.shape, const),
                  pl.BlockSpec(norm_w.shape, const),
                  pl.BlockSpec((D_MODEL, 2 * D_FF), const, pipeline_mode=resident),
                  pl.BlockSpec((D_FF, D_MODEL), const, pipeline_mode=resident)],
        out_specs=pl.BlockSpec((tm, D_MODEL), row),
        out_shape=jax.ShapeDtypeStruct((t, D_MODEL), F32),
        compiler_params=_cparams(("arbitrary",)),
        name="swiglu",
    )(x, mod, norm_w, w_in, w_out)


def _rope_tables(seq_len):
    rows = seq_len // GRID_W
    row_idx = jnp.repeat(jnp.arange(rows, dtype=F32), GRID_W)
    col_idx = (jnp.arange(seq_len) % GRID_W).astype(F32)
    n_freq = DK // 4
    freqs = ROPE_BASE ** (-jnp.arange(n_freq, dtype=F32) / n_freq)
    ang = jnp.concatenate([row_idx[:, None] * freqs, col_idx[:, None] * freqs], axis=-1)
    cos = jnp.repeat(jnp.cos(ang), 2, axis=-1)
    sin = jnp.repeat(jnp.sin(ang), 2, axis=-1)
    sign = jnp.tile(jnp.array([-1.0, 1.0], F32), DK // 2)
    return cos, sin * sign


def _one_path(x, mod, seq_len, n_seq, wts, rope_tabs, s_ret0, s_dn0, emit_state):
    rows_per_cond = seq_len if mod.shape[0] > 1 else x.shape[0]
    proj, gates = _inproj(x, mod, wts["norm_w"], wts["w_a"], wts["w_b"], wts["w_gate"],
                          wts["conv_w"], rows_per_cond, seq_len, rope_tabs)
    ret = _retention(proj, wts["logit_rep"], wts["ret_gn_w"], n_seq, seq_len,
                     s0=s_ret0, emit_state=emit_state)
    gdn = _gdn(proj, gates, wts["a_log"], wts["dt_bias"], n_seq, seq_len,
               s0=s_dn0, emit_state=emit_state)
    x1 = _merge(x, ret[0], gdn[0], gdn[1], proj, mod, wts["norm_w"], wts["dn_norm_w"],
                wts["w_ret_o"], wts["w_dn_o"], wts["w_out"], rows_per_cond)
    y = _ffn(x1, mod, wts["norm_w"], wts["w_ffn_in"], wts["w_ffn_out"], rows_per_cond)
    if emit_state:
        return y, ret[1], gdn[2]
    return y, None, None


def kernel(x_prompt, x_sample, c, state_ret, state_dn, c_ctx, w_mod, b_mod, norm_w, w_in, conv_w,
           ret_decay_logit, ret_gn_w, dn_a_log, dn_dt_bias, dn_norm_w, w_ret_o, w_dn_o, w_out,
           w_ffn_in, w_ffn_out):
    n_ctx, l_ctx, _ = x_prompt.shape
    n_lat, l_lat, _ = x_sample.shape
    assert w_mod.shape[0] == 1, "single-layer kernel"

    cond8 = jnp.zeros((8, D_MODEL), F32).at[0].set(c_ctx).at[1:1 + n_lat].set(c)
    mod = _modulation(cond8, w_mod[0], b_mod)
    mod_ctx, mod_lat = mod[0:1], mod[1:1 + n_lat]

    w_in_t = w_in[0].T
    c_gr = COL_GATE + 2 * N_GATE
    wts = {
        "norm_w": norm_w[0],
        "w_a": w_in_t[:COL_GATE].astype(BF16),
        "w_b": w_in_t[c_gr:].astype(BF16),
        "w_gate": w_in_t[COL_GATE:c_gr].astype(BF16),
        "logit_rep": jnp.broadcast_to(ret_decay_logit[0].T[:, :, None], (HEADS, 2, DK)),
        "ret_gn_w": ret_gn_w,
        "conv_w": conv_w[0],
        "a_log": dn_a_log.reshape(N_GATE, 1),
        "dt_bias": dn_dt_bias.reshape(N_GATE, 1),
        "dn_norm_w": dn_norm_w,
        "w_ret_o": w_ret_o[0].astype(BF16),
        "w_dn_o": w_dn_o[0].astype(BF16),
        "w_out": w_out[0].astype(BF16),
        "w_ffn_in": w_ffn_in[0].astype(BF16),
        "w_ffn_out": w_ffn_out[0].astype(BF16),
    }

    y_p, s_ret, s_dn = _one_path(x_prompt.reshape(n_ctx * l_ctx, D_MODEL), mod_ctx, l_ctx, n_ctx,
                                 wts, None, None, None, True)
    y_s, _, _ = _one_path(x_sample.reshape(n_lat * l_lat, D_MODEL), mod_lat, l_lat, n_lat,
                          wts, _rope_tables(l_lat), state_ret, state_dn, False)
    return (y_p.reshape(x_prompt.shape), y_s.reshape(x_sample.shape), s_ret, s_dn)
```

```python
import functools

import jax
import jax.numpy as jnp
from jax import lax
from jax.experimental import pallas as pl
from jax.experimental.pallas import tpu as pltpu

F32 = jnp.float32
BF16 = jnp.bfloat16

D_MODEL = 1024
HEADS = 4
DK = 128
DV = 256
QK_W = HEADS * DK
V_W = HEADS * DV
GDN_CHUNK = 64
TRI_BASE = 8
BLK = 256
SUB = 8
LANES = 128
HALO = 16
CONV_WIDTH = 3
GRID_W = 64
ROPE_BASE = 10000.0
EPS = 1e-6
D_FF = 2816
MXU_DIM = 256
FF_CHUNKS = (6 * MXU_DIM, 5 * MXU_DIM)
assert sum(FF_CHUNKS) == D_FF
N_GATE = 2 * HEADS

COL_RQ, COL_RK, COL_RV, COL_RG = 0, 512, 1024, 2048
COL_DQK, COL_DV, COL_DZ = 3072, 4096, 5120
COL_GATE = 6144
COL_GR, COL_GD = 6144, 7168
PROJ_W = 8192
PROJ_TN = 2048

VMEM_LIMIT = 56 * 1024 * 1024


def _cparams(sem):
    return pltpu.CompilerParams(dimension_semantics=sem, vmem_limit_bytes=VMEM_LIMIT)


def _mm(a, b):
    return jnp.dot(a.astype(BF16), b.astype(BF16), preferred_element_type=F32)


def _mm_nt(a, b):
    return lax.dot_general(a.astype(BF16), b.astype(BF16), (((1,), (1,)), ((), ())),
                           preferred_element_type=F32)


def _mm_tn(a, b):
    return lax.dot_general(a.astype(BF16), b.astype(BF16), (((0,), (0,)), ((), ())),
                           preferred_element_type=F32)


def _sigmoid(x):
    return 1.0 / (1.0 + jnp.exp(-x))


def _silu(x):
    return x * _sigmoid(x)


def _softplus(x):
    return jnp.maximum(x, 0.0) + jnp.log(1.0 + jnp.exp(-jnp.abs(x)))


def _log_sigmoid(x):
    return -_softplus(-x)


def _rms(x, w):
    return x * lax.rsqrt(jnp.mean(x * x, axis=-1, keepdims=True) + EPS) * w


def _mod_kernel(c_ref, w_ref, b_ref, o_ref):
    o_ref[...] = _mm(_silu(c_ref[...]), w_ref[...]) + b_ref[...]


def _modulation(cond8, w_mod, b_mod):
    n = w_mod.shape[1]
    tn = 1536
    return pl.pallas_call(
        _mod_kernel,
        grid=(n // tn,),
        in_specs=[pl.BlockSpec((8, D_MODEL), lambda j: (0, 0)),
                  pl.BlockSpec((D_MODEL, tn), lambda j: (0, j)),
                  pl.BlockSpec((1, tn), lambda j: (0, j))],
        out_specs=pl.BlockSpec((8, tn), lambda j: (0, j)),
        out_shape=jax.ShapeDtypeStruct((8, n), F32),
        compiler_params=_cparams(("arbitrary",)),
        name="modulation",
    )(cond8, w_mod, b_mod)


CHUNK_W = 256
SUBROWS = 128
PLAIN, RET_Q, RET_K, DN_Q, DN_K, DN_V = range(6)
TILE_KINDS = (
    (RET_Q, RET_Q, RET_K, RET_K, PLAIN, PLAIN, PLAIN, PLAIN),
    (PLAIN, PLAIN, PLAIN, PLAIN, DN_Q, DN_Q, DN_K, DN_K),
    (DN_V, DN_V, DN_V, DN_V, PLAIN, PLAIN, PLAIN, PLAIN),
    (PLAIN,) * 8,
)


def _pair_swap(x):
    lane = lax.broadcasted_iota(jnp.int32, x.shape, 1)
    nxt = pltpu.roll(x, x.shape[1] - 1, 1)
    prv = pltpu.roll(x, 1, 1)
    return jnp.where((lane & 1) == 0, nxt, prv)


def _inproj_kernel(*refs, tm, rows_per_cond, n_a, seq_len, rope):
    it = iter(refs)
    x_ref, xp_ref, xn_ref, mod_ref, nw_ref, wa_ref, wb_ref, wg_ref, cw_ref = (next(it) for _ in range(9))
    cos_ref = next(it) if rope else None
    sin_ref = next(it) if rope else None
    o_ref, g_ref, h_sc, p_sc = next(it), next(it), next(it), next(it)

    i = pl.program_id(0)
    j = pl.program_id(1)
    t0 = i * tm
    tile_rows = slice(HALO, HALO + tm)

    @pl.when(j == 0)
    def _():
        ci = t0 // rows_per_cond
        m = mod_ref[pl.ds(ci, 1), :]
        sh1 = m[:, 0:D_MODEL]
        sc1 = m[:, D_MODEL:2 * D_MODEL]

        def pre(x):
            return (_rms(x, nw_ref[0:1, :]) * (1.0 + sc1) + sh1).astype(BF16)

        hb = pre(x_ref[...])
        h_sc[tile_rows, :] = hb
        h_sc[0:HALO, :] = pre(xp_ref[...])
        h_sc[HALO + tm:2 * HALO + tm, :] = pre(xn_ref[...])
        g_ref[...] = _mm_nt(wg_ref[...], hb)

    seq_start = (t0 & (seq_len - 1)) == 0
    seq_end = ((t0 + tm) & (seq_len - 1)) == 0

    def conv_silu(win, r0, cw_lo):
        width = win.shape[1]
        inner = slice(SUB, SUB + SUBROWS)
        cur = win[inner]
        prev = pltpu.roll(win, 1, 0)[inner]
        nxt = pltpu.roll(win, win.shape[0] - 1, 0)[inner]
        row = lax.broadcasted_iota(jnp.int32, cur.shape, 0)
        if seq_len >= tm:
            if r0 == 0:
                prev = jnp.where((row == 0) & seq_start, 0.0, prev)
            if r0 + SUBROWS == tm:
                nxt = jnp.where((row == SUBROWS - 1) & seq_end, 0.0, nxt)
        else:
            if r0 % seq_len == 0:
                prev = jnp.where(row == 0, 0.0, prev)
            if (r0 + SUBROWS) % seq_len == 0:
                nxt = jnp.where(row == SUBROWS - 1, 0.0, nxt)
        cw = cw_ref[:, cw_lo:cw_lo + width]
        return _silu(prev * cw[0:1, :] + cur * cw[1:2, :] + nxt * cw[2:3, :])

    def l2n_heads(y, scale):
        outs = []
        for lo in range(0, y.shape[1], DK):
            yh = y[:, lo:lo + DK]
            outs.append(yh * (lax.rsqrt(jnp.sum(yh * yh, axis=-1, keepdims=True) + EPS) * scale))
        return jnp.concatenate(outs, axis=1)

    def rotary(p, r0):
        reps = p.shape[1] // DK
        cos = jnp.concatenate([cos_ref[r0:r0 + SUBROWS, :]] * reps, axis=1)
        sin = jnp.concatenate([sin_ref[r0:r0 + SUBROWS, :]] * reps, axis=1)
        return p * cos + _pair_swap(p) * sin

    def tile(w_ref, w_row0, kinds):
        n_staged = 0
        for c, kind in enumerate(kinds):
            lo = c * CHUNK_W
            w = w_ref[w_row0 + lo:w_row0 + lo + CHUNK_W, :]
            conv = kind in (DN_Q, DN_K, DN_V)
            if not conv and not (rope and kind in (RET_Q, RET_K)):
                y = _mm_nt(h_sc[tile_rows, :], w)
                if kind == RET_Q:
                    y = y * (DK ** -0.5)
                o_ref[:, lo:lo + CHUNK_W] = y.astype(o_ref.dtype)
                continue
            buf = p_sc.at[n_staged % 2]
            n_staged += 1
            if conv:
                buf[...] = _mm_nt(h_sc[...], w)
                cw_lo = {DN_Q: 0, DN_K: QK_W, DN_V: 2 * QK_W}[kind] + (lo % (2 * QK_W) if kind == DN_V
                                                                      else lo % QK_W)
            else:
                buf[tile_rows, :] = _mm_nt(h_sc[tile_rows, :], w)
            for r0 in range(0, tm, SUBROWS):
                if conv:
                    y = conv_silu(buf[HALO + r0 - SUB:HALO + r0 + SUBROWS + SUB, :], r0, cw_lo)
                    if kind == DN_Q:
                        y = l2n_heads(y, DK ** -0.5)
                    elif kind == DN_K:
                        y = l2n_heads(y, 1.0)
                else:
                    y = buf[HALO + r0:HALO + r0 + SUBROWS, :]
                    if kind == RET_Q:
                        y = y * (DK ** -0.5)
                    y = rotary(y, r0)
                o_ref[r0:r0 + SUBROWS, lo:lo + CHUNK_W] = y.astype(o_ref.dtype)

    tn = len(TILE_KINDS[0]) * CHUNK_W
    for jt, kinds in enumerate(TILE_KINDS):
        w_ref, w_row0 = (wa_ref, jt * tn) if jt < n_a else (wb_ref, (jt - n_a) * tn)

        @pl.when(j == jt)
        def _(w_ref=w_ref, w_row0=w_row0, kinds=kinds):
            tile(w_ref, w_row0, kinds)


def _inproj(x, mod, norm_w, w_a, w_b, w_gate, conv_w, rows_per_cond, seq_len, rope_tabs):
    t = x.shape[0]
    tm, tn = 1024, PROJ_TN
    n_a = w_a.shape[0] // tn
    rope = rope_tabs is not None
    assert tn == len(TILE_KINDS[0]) * CHUNK_W and PROJ_W == len(TILE_KINDS) * tn
    assert seq_len & (seq_len - 1) == 0 and (tm % seq_len == 0 or seq_len % tm == 0)
    kern = functools.partial(_inproj_kernel, tm=tm, rows_per_cond=rows_per_cond, n_a=n_a,
                             seq_len=seq_len, rope=rope)
    halo_per_tile = tm // HALO
    n_halo = t // HALO
    in_specs = [pl.BlockSpec((tm, D_MODEL), lambda i, j: (i, 0)),
                pl.BlockSpec((HALO, D_MODEL), lambda i, j: (jnp.maximum(i * halo_per_tile - 1, 0), 0)),
                pl.BlockSpec((HALO, D_MODEL),
                             lambda i, j: (jnp.minimum((i + 1) * halo_per_tile, n_halo - 1), 0)),
                pl.BlockSpec(mod.shape, lambda i, j: (0, 0)),
                pl.BlockSpec(norm_w.shape, lambda i, j: (0, 0)),
                pl.BlockSpec(w_a.shape, lambda i, j: (0, 0), pipeline_mode=pl.Buffered(1)),
                pl.BlockSpec(w_b.shape, lambda i, j: (0, 0), pipeline_mode=pl.Buffered(1)),
                pl.BlockSpec((2 * N_GATE, D_MODEL), lambda i, j: (0, 0)),
                pl.BlockSpec(conv_w.shape, lambda i, j: (0, 0))]
    args = [x, x, x, mod, norm_w, w_a, w_b, w_gate, conv_w]
    if rope:
        tiles_per_seq = max(seq_len // tm, 1)
        in_specs += [pl.BlockSpec((tm, DK), lambda i, j: (i % tiles_per_seq, 0))] * 2
        args += list(rope_tabs)
    return pl.pallas_call(
        kern,
        grid=(t // tm, PROJ_W // tn),
        in_specs=in_specs,
        out_specs=[pl.BlockSpec((tm, tn), lambda i, j: (i, j)),
                   pl.BlockSpec((2 * N_GATE, tm), lambda i, j: (0, i))],
        out_shape=[jax.ShapeDtypeStruct((t, PROJ_W), BF16),
                   jax.ShapeDtypeStruct((2 * N_GATE, t), F32)],
        scratch_shapes=[pltpu.VMEM((tm + 2 * HALO, D_MODEL), BF16),
                        pltpu.VMEM((2, tm + 2 * HALO, CHUNK_W), F32)],
        compiler_params=_cparams(("arbitrary", "arbitrary")),
        name="inproj",
    )(*args)


RET_GROUP = 8


def _ret_kernel(*refs, seq_len, has_s0, emit_state):
    it = iter(refs)
    q_ref, k_ref, v_ref = next(it), next(it), next(it)
    logit_ref, gnw_ref = next(it), next(it)
    s0_ref = next(it) if has_s0 else None
    o_ref = next(it)
    st_ref = next(it) if emit_state else None
    q_sc, k_sc, st_sc = next(it), next(it), next(it)

    nb = seq_len // BLK
    cross = has_s0 or nb > 1

    def rows(c):
        return slice(c * BLK, (c + 1) * BLK)

    def qk_cols(h):
        return slice(h * DK, (h + 1) * DK)

    def v_cols(h):
        return slice(h * DV, (h + 1) * DV)

    q_sc[...] = q_ref[...].astype(F32)
    k_sc[...] = k_ref[...].astype(F32)

    pos = lax.broadcasted_iota(jnp.int32, (BLK, DK), 0).astype(F32)

    def log_gammas(h):
        lg = _log_sigmoid(logit_ref[h])
        return lg[0:1, :], lg[1:2, :]

    for h in range(HEADS):
        lgf, lgb = log_gammas(h)
        kdec_f = jnp.exp(lgf * (BLK - 1.0 - pos))
        kdec_b = jnp.exp(lgb * pos)
        cdec_f = jnp.exp(lgf[:, 0:1] * float(BLK))
        cdec_b = jnp.exp(lgb[:, 0:1] * float(BLK))
        af = {c: _mm_tn(k_sc[rows(c), qk_cols(h)] * kdec_f, v_ref[rows(c), v_cols(h)])
              for c in range(nb) if c < nb - 1 or emit_state}
        ab = {c: _mm_tn(k_sc[rows(c), qk_cols(h)] * kdec_b, v_ref[rows(c), v_cols(h)])
              for c in range(nb) if c > 0 or emit_state}
        s = s0_ref[0, h] if has_s0 else jnp.zeros((DK, DV), F32)
        for c in range(nb):
            if cross:
                st_sc[h, c, 0:DK, :] = s.astype(BF16)
            if c in af:
                s = s * cdec_f + af[c]
        if emit_state:
            st_ref[0, h] = s
        s = s0_ref[1, h] if has_s0 else jnp.zeros((DK, DV), F32)
        for c in range(nb - 1, -1, -1):
            if cross:
                st_sc[h, c, DK:2 * DK, :] = s.astype(BF16)
            if c in ab:
                s = s * cdec_b + ab[c]
        if emit_state:
            st_ref[1, h] = s

    ii = lax.broadcasted_iota(jnp.int32, (BLK, BLK), 0)
    jj = lax.broadcasted_iota(jnp.int32, (BLK, BLK), 1)
    diff = (ii - jj).astype(F32)
    pos2 = lax.broadcasted_iota(jnp.int32, (BLK, 2 * DK), 0).astype(F32)
    first_half = lax.broadcasted_iota(jnp.int32, (BLK, 2 * DK), 1) < DK

    def head_consts(h):
        lgf, lgb = log_gammas(h)
        lgf1, lgb1 = lgf[:, 0:1], lgb[:, 0:1]
        dmat = (jnp.where(diff >= 0, jnp.exp(lgf1 * jnp.maximum(diff, 0.0)), 0.0)
                + jnp.where(diff <= 0, jnp.exp(lgb1 * jnp.maximum(-diff, 0.0)), 0.0))
        qdec = jnp.where(first_half, jnp.exp(lgf1 * (pos2 + 1.0)), jnp.exp(lgb1 * (BLK - pos2)))
        return dmat, qdec

    def unit(h, c, consts):
        dmat, qdec = consts
        qc = q_sc[rows(c), qk_cols(h)]
        s = _mm_nt(qc, k_sc[rows(c), qk_cols(h)])
        yield
        o = _mm(s * dmat, v_ref[rows(c), v_cols(h)])
        if cross:
            o_cross = _mm(jnp.concatenate([qc, qc], axis=1) * qdec, st_sc[h, c])
        yield
        if cross:
            o = o + o_cross
        mu = jnp.mean(o, axis=-1, keepdims=True)
        oc = o - mu
        var = jnp.mean(oc * oc, axis=-1, keepdims=True)
        o_ref[rows(c), v_cols(h)] = (oc * lax.rsqrt(var + EPS) * gnw_ref[:, v_cols(h)]).astype(o_ref.dtype)
        yield

    units = [(h, c) for h in range(HEADS) for c in range(nb)]
    for g0 in range(0, len(units), RET_GROUP):
        consts = {}
        gens = []
        for h, c in units[g0:g0 + RET_GROUP]:
            if h not in consts:
                consts[h] = head_consts(h)
            gens.append(unit(h, c, consts[h]))
        for _ in zip(*gens):
            pass


def _retention(proj, logit_rep, gn_w, n_seq, seq_len, s0=None, emit_state=False):
    t = proj.shape[0]
    has_s0 = s0 is not None
    kern = functools.partial(_ret_kernel, seq_len=seq_len, has_s0=has_s0, emit_state=emit_state)
    in_specs = [pl.BlockSpec((seq_len, QK_W), lambda b: (b, COL_RQ // QK_W)),
                pl.BlockSpec((seq_len, QK_W), lambda b: (b, COL_RK // QK_W)),
                pl.BlockSpec((seq_len, V_W), lambda b: (b, COL_RV // V_W))]
    args = [proj, proj, proj]
    in_specs += [pl.BlockSpec((HEADS, 2, DK), lambda b: (0, 0, 0)),
                 pl.BlockSpec((1, V_W), lambda b: (0, 0))]
    args += [logit_rep, gn_w]
    state_spec = pl.BlockSpec((None, None, 2, HEADS, DK, DV), lambda b: (b, 0, 0, 0, 0, 0))
    if has_s0:
        in_specs.append(state_spec)
        args.append(s0)
    out_specs = [pl.BlockSpec((seq_len, V_W), lambda b: (b, 0))]
    out_shape = [jax.ShapeDtypeStruct((t, V_W), BF16)]
    if emit_state:
        out_specs.append(state_spec)
        out_shape.append(jax.ShapeDtypeStruct((n_seq, 1, 2, HEADS, DK, DV), F32))
    nb = seq_len // BLK
    return pl.pallas_call(
        kern,
        grid=(n_seq,),
        in_specs=in_specs,
        out_specs=out_specs,
        out_shape=out_shape,
        scratch_shapes=[pltpu.VMEM((seq_len, QK_W), F32), pltpu.VMEM((seq_len, QK_W), F32),
                        pltpu.VMEM((HEADS, nb, 2 * DK, DV), BF16)],
        compiler_params=_cparams(("arbitrary",)),
        name="retention",
    )(*args)


N_MASK = 8
(MASK_EQ_F, MASK_EQ_B, MASK_OFFDIAG, MASK_EYE, MASK_BASE, MASK_PAIR0) = 0, 1, 2, 3, 4, 5


def _build_masks(mask_sc, eye_sc):
    ii = lax.broadcasted_iota(jnp.int32, (BLK, BLK), 0)
    jj = lax.broadcasted_iota(jnp.int32, (BLK, BLK), 1)

    def blk(v, size):
        return lax.shift_right_logical(v, size.bit_length() - 1)

    same = blk(ii, GDN_CHUNK) == blk(jj, GDN_CHUNK)
    mask_sc[MASK_EQ_F] = (same & (jj <= ii)).astype(F32)
    mask_sc[MASK_EQ_B] = (same & (jj >= ii)).astype(F32)
    mask_sc[MASK_OFFDIAG] = (ii != jj).astype(F32)
    mask_sc[MASK_EYE] = (ii == jj).astype(F32)
    mask_sc[MASK_BASE] = (blk(ii, TRI_BASE) == blk(jj, TRI_BASE)).astype(F32)
    size, idx = TRI_BASE, MASK_PAIR0
    while size < GDN_CHUNK:
        pair = (blk(ii, 2 * size) == blk(jj, 2 * size)) & (blk(ii, size) != blk(jj, size))
        mask_sc[idx] = pair.astype(F32)
        size, idx = 2 * size, idx + 1
    eye_sc[...] = (ii == jj).astype(BF16)


def _seg_cumsums(x):
    n = x.shape[1]
    pos = lax.broadcasted_iota(jnp.int32, x.shape, 1) & (GDN_CHUNK - 1)
    up, down = x, x
    s = 1
    while s < GDN_CHUNK:
        up = up + jnp.where(pos >= s, pltpu.roll(up, s, 1), 0.0)
        down = down + jnp.where(pos < GDN_CHUNK - s, pltpu.roll(down, n - s, 1), 0.0)
        s *= 2
    return up, down


def _rows_to_cols(rows, eye_sc):
    p1 = rows.astype(BF16).astype(F32)
    r1 = rows - p1
    p2 = r1.astype(BF16).astype(F32)
    p3 = r1 - p2
    pieces = jnp.concatenate([p1, p2, p3], axis=0).astype(BF16)
    c3 = lax.dot_general(eye_sc[...], pieces, (((1,), (1,)), ((), ())), preferred_element_type=F32)
    return c3[:, 0:SUB] + c3[:, SUB:2 * SUB] + c3[:, 2 * SUB:3 * SUB]


def _gdn_kernel(*refs, nb, has_s0, emit_state):
    it = iter(refs)
    xqk_refs = (next(it), next(it))
    xv_refs = (next(it), next(it))
    g_refs = (next(it), next(it))
    alog_ref, dtb_ref = next(it), next(it)
    s0_ref = next(it) if has_s0 else None
    o_refs = (next(it), next(it))
    st_ref = next(it) if emit_state else None
    st_sc, mask_sc, eye_sc = next(it), next(it), next(it)

    b = pl.program_id(0)
    step = pl.program_id(1)

    @pl.when((b == 0) & (step == 0))
    def _():
        _build_masks(mask_sc, eye_sc)

    @pl.when(step == 0)
    def _():
        for d in range(2):
            for h in range(HEADS):
                st_sc[d, h] = s0_ref[d, h] if has_s0 else jnp.zeros((DK, DV), F32)

    nck = BLK // GDN_CHUNK

    def gates():
        row8 = lax.broadcasted_iota(jnp.int32, (N_GATE, BLK), 0)
        fwd_rows = row8 < HEADS
        logit_b = jnp.where(fwd_rows, g_refs[0][0:N_GATE, :], g_refs[1][0:N_GATE, :])
        logit_a = jnp.where(fwd_rows, g_refs[0][N_GATE:2 * N_GATE, :], g_refs[1][N_GATE:2 * N_GATE, :])
        beta = _sigmoid(logit_b)
        la = -jnp.exp(alog_ref[...]) * _softplus(logit_a + dtb_ref[...])
        up, down = _seg_cumsums(la)
        g8 = jnp.where(fwd_rows, up, down)
        r8 = jnp.where(fwd_rows, down, up) - la
        etot8 = jnp.exp(up + down - la)
        return dict(beta=beta, g8=g8, eg8=jnp.exp(g8), erb8=jnp.exp(r8) * beta, etot8=etot8,
                    gcols=_rows_to_cols(g8, eye_sc))

    gate = {}

    def chain(d, h):
        c = d * HEADS + h
        q16 = xqk_refs[d][:, h * DK:(h + 1) * DK]
        k16 = xqk_refs[d][:, QK_W + h * DK:QK_W + (h + 1) * DK]
        v16 = xv_refs[d][:, h * DV:(h + 1) * DV]
        q = q16.astype(F32)
        k = k16.astype(F32)
        kk = _mm_nt(k16, k16)
        qk = _mm_nt(q16, k16)
        yield
        g_col = gate["gcols"][:, c:c + 1]
        g_row = gate["g8"][c:c + 1, :]
        m_eq = mask_sc[MASK_EQ_F if d == 0 else MASK_EQ_B]
        decay_beta = (jnp.exp(jnp.minimum(g_col - g_row, 0.0)) * m_eq) * gate["beta"][c:c + 1, :]
        lp = kk * decay_beta * mask_sc[MASK_OFFDIAG]
        sc16 = (qk * decay_beta).astype(BF16)
        qg16 = (q * jnp.exp(g_col)).astype(BF16)
        kt16 = (k.T * gate["erb8"][c:c + 1, :]).astype(BF16)

        x = -(lp * mask_sc[MASK_BASE])
        t = mask_sc[MASK_EYE] + x
        for _ in range(TRI_BASE.bit_length() - 2):
            x = _mm(x, x)
            yield
            t = t + _mm(t, x)
            yield
        size, idx = TRI_BASE, MASK_PAIR0
        while size < GDN_CHUNK:
            groups = [slice(r * size, (r + 1) * size) for r in range(BLK // size)]
            late = [(r % 2 == 1) == (d == 0) for r in range(BLK // size)]
            e_late = jnp.concatenate([lp[g, :] * mask_sc[idx, g, :] for g, on in zip(groups, late) if on],
                                     axis=0)
            et = _mm(e_late, t)
            yield
            t_late = jnp.concatenate([t[g, :] for g, on in zip(groups, late) if on], axis=0)
            et_rows = iter(et[j * size:(j + 1) * size, :] for j in range(BLK // size // 2))
            et_full = jnp.concatenate([next(et_rows) if on else jnp.zeros((size, BLK), F32)
                                       for on in late], axis=0)
            t_late = t_late - _mm(t_late, et_full)
            yield
            new_rows = iter(t_late[j * size:(j + 1) * size, :] for j in range(BLK // size // 2))
            t = jnp.concatenate([next(new_rows) if on else t[g, :] for g, on in zip(groups, late)],
                                axis=0)
            size, idx = 2 * size, idx + 1
        u = _mm(t, v16)
        w16 = _mm(t * gate["eg8"][c:c + 1, :], k16).astype(BF16)
        yield

        order = range(nck) if d == 0 else range(nck - 1, -1, -1)
        for cc in order:
            rs = slice(cc * GDN_CHUNK, (cc + 1) * GDN_CHUNK)
            s_prev = st_sc[d, h]
            wq = jnp.concatenate([w16[rs, :], qg16[rs, :]], axis=0)
            ws_qs = _mm(wq, s_prev)
            yield
            v_new = (u[rs, :] - ws_qs[0:GDN_CHUNK]).astype(BF16)
            sk = _mm(jnp.concatenate([sc16[rs, rs], kt16[:, rs]], axis=0), v_new)
            o = ws_qs[GDN_CHUNK:] + sk[0:GDN_CHUNK]
            decay_all = gate["etot8"][c:c + 1, cc * GDN_CHUNK:cc * GDN_CHUNK + 1]
            st_sc[d, h] = s_prev * decay_all + sk[GDN_CHUNK:]
            o_refs[d][rs, h * DV:(h + 1) * DV] = o.astype(o_refs[d].dtype)
            yield

    chains = [chain(d, h) for d in range(2) for h in range(HEADS)]
    for ch in chains:
        next(ch)
    gate.update(gates())
    for _ in zip(*chains):
        pass

    if emit_state:
        @pl.when(step == nb - 1)
        def _():
            for d in range(2):
                for h in range(HEADS):
                    st_ref[d, h] = st_sc[d, h]


def _gdn(proj, gates, a_log, dt_bias, n_seq, seq_len, s0=None, emit_state=False):
    t = proj.shape[0]
    nb = seq_len // BLK
    has_s0 = s0 is not None
    kern = functools.partial(_gdn_kernel, nb=nb, has_s0=has_s0, emit_state=emit_state)

    def fblk(b, s):
        return b * nb + s

    def bblk(b, s):
        return b * nb + nb - 1 - s

    def spec_x(blk, col):
        return pl.BlockSpec((BLK, V_W), lambda b, s: (blk(b, s), col // V_W))

    def spec_g(blk):
        return pl.BlockSpec((2 * N_GATE, BLK), lambda b, s: (0, blk(b, s)))

    const2 = lambda b, s: (0, 0)
    in_specs = [spec_x(fblk, COL_DQK), spec_x(bblk, COL_DQK), spec_x(fblk, COL_DV), spec_x(bblk, COL_DV),
                spec_g(fblk), spec_g(bblk),
                pl.BlockSpec((N_GATE, 1), const2),
                pl.BlockSpec((N_GATE, 1), const2)]
    args = [proj, proj, proj, proj, gates, gates, a_log, dt_bias]
    state_spec = pl.BlockSpec((None, None, 2, HEADS, DK, DV), lambda b, s: (b, 0, 0, 0, 0, 0))
    if has_s0:
        in_specs.append(state_spec)
        args.append(s0)
    out_specs = [pl.BlockSpec((BLK, V_W), lambda b, s: (fblk(b, s), 0)),
                 pl.BlockSpec((BLK, V_W), lambda b, s: (bblk(b, s), 0))]
    out_shape = [jax.ShapeDtypeStruct((t, V_W), BF16), jax.ShapeDtypeStruct((t, V_W), BF16)]
    if emit_state:
        out_specs.append(state_spec)
        out_shape.append(jax.ShapeDtypeStruct((n_seq, 1, 2, HEADS, DK, DV), F32))
    scratch = [pltpu.VMEM((2, HEADS, DK, DV), F32),
               pltpu.VMEM((N_MASK, BLK, BLK), F32),
               pltpu.VMEM((BLK, BLK), BF16)]
    return pl.pallas_call(
        kern,
        grid=(n_seq, nb),
        in_specs=in_specs,
        out_specs=out_specs,
        out_shape=out_shape,
        scratch_shapes=scratch,
        compiler_params=_cparams(("arbitrary", "arbitrary")),
        name="gated_delta",
    )(*args)


def _merge_kernel(x_ref, or_ref, of_ref, ob_ref, rg_ref, dz_ref, gr_ref, gd_ref, mod_ref, nw_ref,
                  dnw_ref, wr_ref, wd_ref, wo_ref, o_ref, *, tm, rows_per_cond):
    i = pl.program_id(0)
    ci = (i * tm) // rows_per_cond
    m = mod_ref[pl.ds(ci, 1), :]
    g1 = m[:, 2 * D_MODEL:3 * D_MODEL]
    y_r = _mm(_silu(rg_ref[...].astype(F32)) * or_ref[...].astype(F32), wr_ref[...])
    dnw = dnw_ref[...]
    heads = []
    for h in range(HEADS):
        cols = slice(h * DV, (h + 1) * DV)
        od = _rms(of_ref[:, cols].astype(F32) + ob_ref[:, cols].astype(F32), dnw)
        heads.append((od * _silu(dz_ref[:, cols].astype(F32))).astype(BF16))
    y_d = jnp.dot(jnp.concatenate(heads, axis=1), wd_ref[...], preferred_element_type=F32)
    merged = _sigmoid(gr_ref[...].astype(F32)) * y_r + _sigmoid(gd_ref[...].astype(F32)) * y_d
    mo = _mm(merged, wo_ref[...])
    o_ref[...] = x_ref[...] + g1 * _rms(mo, nw_ref[1:2, :])


def _merge(x, o_r, o_f, o_b, proj, mod, norm_w, dn_norm_w, w_ret_o, w_dn_o, w_out, rows_per_cond):
    t = x.shape[0]
    tm = 512
    kern = functools.partial(_merge_kernel, tm=tm, rows_per_cond=rows_per_cond)
    row = lambda i: (i, 0)
    const = lambda i: (0, 0)
    wspec = pl.BlockSpec((D_MODEL, D_MODEL), const)
    return pl.pallas_call(
        kern,
        grid=(t // tm,),
        in_specs=[pl.BlockSpec((tm, D_MODEL), row),
                  pl.BlockSpec((tm, V_W), row),
                  pl.BlockSpec((tm, V_W), row),
                  pl.BlockSpec((tm, V_W), row),
                  pl.BlockSpec((tm, V_W), lambda i: (i, COL_RG // V_W)),
                  pl.BlockSpec((tm, V_W), lambda i: (i, COL_DZ // V_W)),
                  pl.BlockSpec((tm, D_MODEL), lambda i: (i, COL_GR // D_MODEL)),
                  pl.BlockSpec((tm, D_MODEL), lambda i: (i, COL_GD // D_MODEL)),
                  pl.BlockSpec(mod.shape, const),
                  pl.BlockSpec(norm_w.shape, const),
                  pl.BlockSpec(dn_norm_w.shape, const),
                  wspec, wspec, wspec],
        out_specs=pl.BlockSpec((tm, D_MODEL), row),
        out_shape=jax.ShapeDtypeStruct((t, D_MODEL), F32),
        compiler_params=_cparams(("arbitrary",)),
        name="merge_out",
    )(x, o_r, o_f, o_b, proj, proj, proj, proj, mod, norm_w, dn_norm_w, w_ret_o, w_dn_o, w_out)


def _ffn_kernel(x_ref, mod_ref, nw_ref, wi_ref, wo_ref, o_ref, *, tm, rows_per_cond):
    i = pl.program_id(0)
    ci = (i * tm) // rows_per_cond
    m = mod_ref[pl.ds(ci, 1), :]
    sh2 = m[:, 3 * D_MODEL:4 * D_MODEL]
    sc2 = m[:, 4 * D_MODEL:5 * D_MODEL]
    g2 = m[:, 5 * D_MODEL:6 * D_MODEL]
    x = x_ref[...]
    hb = (_rms(x, nw_ref[2:3, :]) * (1.0 + sc2) + sh2).astype(BF16)
    f = None
    lo = 0
    for width in FF_CHUNKS:
        gate = jnp.dot(hb, wi_ref[:, lo:lo + width], preferred_element_type=F32)
        up = jnp.dot(hb, wi_ref[:, D_FF + lo:D_FF + lo + width], preferred_element_type=F32)
        part = _mm(_silu(gate) * up, wo_ref[lo:lo + width, :])
        lo += width
        f = part if f is None else f + part
    o_ref[...] = x + g2 * _rms(f, nw_ref[3:4, :])


def _ffn(x, mod, norm_w, w_in, w_out, rows_per_cond):
    t = x.shape[0]
    tm = 512
    kern = functools.partial(_ffn_kernel, tm=tm, rows_per_cond=rows_per_cond)
    row = lambda i: (i, 0)
    const = lambda i: (0, 0)
    resident = pl.Buffered(1)
    return pl.pallas_call(
        kern,
        grid=(t // tm,),
        in_specs=[pl.BlockSpec((tm, D_MODEL), row),
                  pl.BlockSpec(mod.shape, const),
                  pl.BlockSpec(norm_w.shape, const),
                  pl.BlockSpec((D_MODEL, 2 * D_FF), const, pipeline_mode=resident),
                  pl.BlockSpec((D_FF, D_MODEL), const, pipeline_mode=resident)],
        out_specs=pl.BlockSpec((tm, D_MODEL), row),
        out_shape=jax.ShapeDtypeStruct((t, D_MODEL), F32),
        compiler_params=_cparams(("arbitrary",)),
        name="swiglu",
    )(x, mod, norm_w, w_in, w_out)


def _rope_tables(seq_len):
    rows = seq_len // GRID_W
    row_idx = jnp.repeat(jnp.arange(rows, dtype=F32), GRID_W)
    col_idx = (jnp.arange(seq_len) % GRID_W).astype(F32)
    n_freq = DK // 4
    freqs = ROPE_BASE ** (-jnp.arange(n_freq, dtype=F32) / n_freq)
    ang = jnp.concatenate([row_idx[:, None] * freqs, col_idx[:, None] * freqs], axis=-1)
    cos = jnp.repeat(jnp.cos(ang), 2, axis=-1)
    sin = jnp.repeat(jnp.sin(ang), 2, axis=-1)
    sign = jnp.tile(jnp.array([-1.0, 1.0], F32), DK // 2)
    return cos, sin * sign


def _one_path(x, mod, seq_len, n_seq, wts, rope_tabs, s_ret0, s_dn0, emit_state):
    rows_per_cond = seq_len if mod.shape[0] > 1 else x.shape[0]
    proj, gates = _inproj(x, mod, wts["norm_w"], wts["w_a"], wts["w_b"], wts["w_gate"],
                          wts["conv_w"], rows_per_cond, seq_len, rope_tabs)
    ret = _retention(proj, wts["logit_rep"], wts["ret_gn_w"], n_seq, seq_len,
                     s0=s_ret0, emit_state=emit_state)
    gdn = _gdn(proj, gates, wts["a_log"], wts["dt_bias"], n_seq, seq_len,
               s0=s_dn0, emit_state=emit_state)
    x1 = _merge(x, ret[0], gdn[0], gdn[1], proj, mod, wts["norm_w"], wts["dn_norm_w"],
                wts["w_ret_o"], wts["w_dn_o"], wts["w_out"], rows_per_cond)
    y = _ffn(x1, mod, wts["norm_w"], wts["w_ffn_in"], wts["w_ffn_out"], rows_per_cond)
    if emit_state:
        return y, ret[1], gdn[2]
    return y, None, None


def kernel(x_prompt, x_sample, c, state_ret, state_dn, c_ctx, w_mod, b_mod, norm_w, w_in, conv_w,
           ret_decay_logit, ret_gn_w, dn_a_log, dn_dt_bias, dn_norm_w, w_ret_o, w_dn_o, w_out,
           w_ffn_in, w_ffn_out):
    n_ctx, l_ctx, _ = x_prompt.shape
    n_lat, l_lat, _ = x_sample.shape
    assert w_mod.shape[0] == 1, "single-layer kernel"

    cond8 = jnp.zeros((8, D_MODEL), F32).at[0].set(c_ctx).at[1:1 + n_lat].set(c)
    mod = _modulation(cond8, w_mod[0], b_mod)
    mod_ctx, mod_lat = mod[0:1], mod[1:1 + n_lat]

    w_in_t = w_in[0].T
    c_gr = COL_GATE + 2 * N_GATE
    wts = {
        "norm_w": norm_w[0],
        "w_a": w_in_t[:COL_GATE].astype(BF16),
        "w_b": w_in_t[c_gr:].astype(BF16),
        "w_gate": w_in_t[COL_GATE:c_gr].astype(BF16),
        "logit_rep": jnp.broadcast_to(ret_decay_logit[0].T[:, :, None], (HEADS, 2, DK)),
        "ret_gn_w": ret_gn_w,
        "conv_w": conv_w[0],
        "a_log": dn_a_log.reshape(N_GATE, 1),
        "dt_bias": dn_dt_bias.reshape(N_GATE, 1),
        "dn_norm_w": dn_norm_w,
        "w_ret_o": w_ret_o[0].astype(BF16),
        "w_dn_o": w_dn_o[0].astype(BF16),
        "w_out": w_out[0].astype(BF16),
        "w_ffn_in": w_ffn_in[0].astype(BF16),
        "w_ffn_out": w_ffn_out[0].astype(BF16),
    }

    y_p, s_ret, s_dn = _one_path(x_prompt.reshape(n_ctx * l_ctx, D_MODEL), mod_ctx, l_ctx, n_ctx,
                                 wts, None, None, None, True)
    y_s, _, _ = _one_path(x_sample.reshape(n_lat * l_lat, D_MODEL), mod_lat, l_lat, n_lat,
                          wts, _rope_tables(l_lat), state_ret, state_dn, False)
    return (y_p.reshape(x_prompt.shape), y_s.reshape(x_sample.shape), s_ret, s_dn)
```

```python
import functools

import jax
import jax.numpy as jnp
from jax import lax
from jax.experimental import pallas as pl
from jax.experimental.pallas import tpu as pltpu

F32 = jnp.float32
BF16 = jnp.bfloat16

D_MODEL = 1024
HEADS = 4
DK = 128
DV = 256
QK_W = HEADS * DK
V_W = HEADS * DV
GDN_CHUNK = 64
TRI_BASE = 8
BLK = 256
SUB = 8
LANES = 128
HALO = 16
CONV_WIDTH = 3
GRID_W = 64
ROPE_BASE = 10000.0
EPS = 1e-6
D_FF = 2816
MXU_DIM = 256
FF_CHUNKS = (6 * MXU_DIM, 5 * MXU_DIM)
assert sum(FF_CHUNKS) == D_FF
N_GATE = 2 * HEADS

COL_RQ, COL_RK, COL_RV, COL_RG = 0, 512, 1024, 2048
COL_DQK, COL_DV, COL_DZ = 3072, 4096, 5120
COL_GATE = 6144
COL_GR, COL_GD = 6144, 7168
PROJ_W = 8192
PROJ_TN = 4096

VMEM_LIMIT = 56 * 1024 * 1024


def _cparams(sem):
    return pltpu.CompilerParams(dimension_semantics=sem, vmem_limit_bytes=VMEM_LIMIT)


def _mm(a, b):
    return jnp.dot(a.astype(BF16), b.astype(BF16), preferred_element_type=F32)


def _mm_nt(a, b):
    return lax.dot_general(a.astype(BF16), b.astype(BF16), (((1,), (1,)), ((), ())),
                           preferred_element_type=F32)


def _mm_tn(a, b):
    return lax.dot_general(a.astype(BF16), b.astype(BF16), (((0,), (0,)), ((), ())),
                           preferred_element_type=F32)


def _sigmoid(x):
    return 1.0 / (1.0 + jnp.exp(-x))


def _silu(x):
    return x * _sigmoid(x)


def _softplus(x):
    return jnp.maximum(x, 0.0) + jnp.log(1.0 + jnp.exp(-jnp.abs(x)))


def _log_sigmoid(x):
    return -_softplus(-x)


def _rms(x, w):
    return x * lax.rsqrt(jnp.mean(x * x, axis=-1, keepdims=True) + EPS) * w


def _mod_kernel(c_ref, w_ref, b_ref, o_ref):
    o_ref[...] = _mm(_silu(c_ref[...]), w_ref[...]) + b_ref[...]


def _modulation(cond8, w_mod, b_mod):
    n = w_mod.shape[1]
    tn = 1536
    return pl.pallas_call(
        _mod_kernel,
        grid=(n // tn,),
        in_specs=[pl.BlockSpec((8, D_MODEL), lambda j: (0, 0)),
                  pl.BlockSpec((D_MODEL, tn), lambda j: (0, j)),
                  pl.BlockSpec((1, tn), lambda j: (0, j))],
        out_specs=pl.BlockSpec((8, tn), lambda j: (0, j)),
        out_shape=jax.ShapeDtypeStruct((8, n), F32),
        compiler_params=_cparams(("arbitrary",)),
        name="modulation",
    )(cond8, w_mod, b_mod)


CHUNK_W = 256
SUBROWS = 128
PLAIN, RET_Q, RET_K, DN_Q, DN_K, DN_V = range(6)
CHUNK_KINDS = ((RET_Q, RET_Q, RET_K, RET_K) + (PLAIN,) * 4
               + (PLAIN,) * 4 + (DN_Q, DN_Q, DN_K, DN_K)
               + (DN_V,) * 4 + (PLAIN,) * 4
               + (PLAIN,) * 8)
assert len(CHUNK_KINDS) * CHUNK_W == PROJ_W and PROJ_TN % CHUNK_W == 0


def _pair_swap(x):
    lane = lax.broadcasted_iota(jnp.int32, x.shape, 1)
    nxt = pltpu.roll(x, x.shape[1] - 1, 1)
    prv = pltpu.roll(x, 1, 1)
    return jnp.where((lane & 1) == 0, nxt, prv)


def _inproj_kernel(*refs, tm, rows_per_cond, seq_len, rope):
    it = iter(refs)
    x_ref, xp_ref, xn_ref, mod_ref, nw_ref, w_ref, cw_ref = (next(it) for _ in range(7))
    cos_ref = next(it) if rope else None
    sin_ref = next(it) if rope else None
    o_ref, g_ref, h_sc, p_sc = next(it), next(it), next(it), next(it)

    i = pl.program_id(0)
    j = pl.program_id(1)
    t0 = i * tm
    tile_rows = slice(HALO, HALO + tm)

    @pl.when(j == 0)
    def _():
        ci = t0 // rows_per_cond
        m = mod_ref[pl.ds(ci, 1), :]
        sh1 = m[:, 0:D_MODEL]
        sc1 = m[:, D_MODEL:2 * D_MODEL]

        def pre(x):
            return (_rms(x, nw_ref[0:1, :]) * (1.0 + sc1) + sh1).astype(BF16)

        hb = pre(x_ref[...])
        h_sc[tile_rows, :] = hb
        h_sc[0:HALO, :] = pre(xp_ref[...])
        h_sc[HALO + tm:2 * HALO + tm, :] = pre(xn_ref[...])
        g_ref[...] = _mm_nt(w_ref[COL_GATE:COL_GATE + 2 * N_GATE, :], hb)

    seq_start = (t0 & (seq_len - 1)) == 0
    seq_end = ((t0 + tm) & (seq_len - 1)) == 0

    def conv_silu(win, r0, cw_lo):
        width = win.shape[1]
        inner = slice(SUB, SUB + SUBROWS)
        cur = win[inner]
        prev = pltpu.roll(win, 1, 0)[inner]
        nxt = pltpu.roll(win, win.shape[0] - 1, 0)[inner]
        row = lax.broadcasted_iota(jnp.int32, cur.shape, 0)
        if seq_len >= tm:
            if r0 == 0:
                prev = jnp.where((row == 0) & seq_start, 0.0, prev)
            if r0 + SUBROWS == tm:
                nxt = jnp.where((row == SUBROWS - 1) & seq_end, 0.0, nxt)
        else:
            if r0 % seq_len == 0:
                prev = jnp.where(row == 0, 0.0, prev)
            if (r0 + SUBROWS) % seq_len == 0:
                nxt = jnp.where(row == SUBROWS - 1, 0.0, nxt)
        cw = cw_ref[:, cw_lo:cw_lo + width]
        return _silu(prev * cw[0:1, :] + cur * cw[1:2, :] + nxt * cw[2:3, :])

    def l2n_heads(y, scale):
        outs = []
        for lo in range(0, y.shape[1], DK):
            yh = y[:, lo:lo + DK]
            outs.append(yh * (lax.rsqrt(jnp.sum(yh * yh, axis=-1, keepdims=True) + EPS) * scale))
        return jnp.concatenate(outs, axis=1)

    def rotary(p, r0):
        reps = p.shape[1] // DK
        cos = jnp.concatenate([cos_ref[r0:r0 + SUBROWS, :]] * reps, axis=1)
        sin = jnp.concatenate([sin_ref[r0:r0 + SUBROWS, :]] * reps, axis=1)
        return p * cos + _pair_swap(p) * sin

    def tile(col0, kinds):
        n_staged = 0
        for c, kind in enumerate(kinds):
            lo = c * CHUNK_W
            col = col0 + lo
            w_row = col if col < COL_GATE else col + 2 * N_GATE
            w = w_ref[w_row:w_row + CHUNK_W, :]
            conv = kind in (DN_Q, DN_K, DN_V)
            if not conv and not (rope and kind in (RET_Q, RET_K)):
                y = _mm_nt(h_sc[tile_rows, :], w)
                if kind == RET_Q:
                    y = y * (DK ** -0.5)
                o_ref[:, lo:lo + CHUNK_W] = y.astype(o_ref.dtype)
                continue
            buf = p_sc.at[n_staged % 2]
            n_staged += 1
            if conv:
                buf[...] = _mm_nt(h_sc[...], w)
                cw_lo = col - COL_DQK
            else:
                buf[tile_rows, :] = _mm_nt(h_sc[tile_rows, :], w)
            for r0 in range(0, tm, SUBROWS):
                if conv:
                    y = conv_silu(buf[HALO + r0 - SUB:HALO + r0 + SUBROWS + SUB, :], r0, cw_lo)
                    if kind == DN_Q:
                        y = l2n_heads(y, DK ** -0.5)
                    elif kind == DN_K:
                        y = l2n_heads(y, 1.0)
                else:
                    y = buf[HALO + r0:HALO + r0 + SUBROWS, :]
                    if kind == RET_Q:
                        y = y * (DK ** -0.5)
                    y = rotary(y, r0)
                o_ref[r0:r0 + SUBROWS, lo:lo + CHUNK_W] = y.astype(o_ref.dtype)

    per_tile = PROJ_TN // CHUNK_W
    for jt in range(PROJ_W // PROJ_TN):
        kinds = CHUNK_KINDS[jt * per_tile:(jt + 1) * per_tile]

        @pl.when(j == jt)
        def _(jt=jt, kinds=kinds):
            tile(jt * PROJ_TN, kinds)


def _inproj(x, mod, norm_w, w_t, conv_w, rows_per_cond, seq_len, rope_tabs):
    t = x.shape[0]
    tm, tn = 1024, PROJ_TN
    rope = rope_tabs is not None
    assert seq_len & (seq_len - 1) == 0 and (tm % seq_len == 0 or seq_len % tm == 0)
    kern = functools.partial(_inproj_kernel, tm=tm, rows_per_cond=rows_per_cond,
                             seq_len=seq_len, rope=rope)
    halo_per_tile = tm // HALO
    n_halo = t // HALO
    in_specs = [pl.BlockSpec((tm, D_MODEL), lambda i, j: (i, 0)),
                pl.BlockSpec((HALO, D_MODEL), lambda i, j: (jnp.maximum(i * halo_per_tile - 1, 0), 0)),
                pl.BlockSpec((HALO, D_MODEL),
                             lambda i, j: (jnp.minimum((i + 1) * halo_per_tile, n_halo - 1), 0)),
                pl.BlockSpec(mod.shape, lambda i, j: (0, 0)),
                pl.BlockSpec(norm_w.shape, lambda i, j: (0, 0)),
                pl.BlockSpec(w_t.shape, lambda i, j: (0, 0), pipeline_mode=pl.Buffered(1)),
                pl.BlockSpec(conv_w.shape, lambda i, j: (0, 0))]
    args = [x, x, x, mod, norm_w, w_t, conv_w]
    if rope:
        tiles_per_seq = max(seq_len // tm, 1)
        in_specs += [pl.BlockSpec((tm, DK), lambda i, j: (i % tiles_per_seq, 0))] * 2
        args += list(rope_tabs)
    return pl.pallas_call(
        kern,
        grid=(t // tm, PROJ_W // tn),
        in_specs=in_specs,
        out_specs=[pl.BlockSpec((tm, tn), lambda i, j: (i, j)),
                   pl.BlockSpec((2 * N_GATE, tm), lambda i, j: (0, i))],
        out_shape=[jax.ShapeDtypeStruct((t, PROJ_W), BF16),
                   jax.ShapeDtypeStruct((2 * N_GATE, t), F32)],
        scratch_shapes=[pltpu.VMEM((tm + 2 * HALO, D_MODEL), BF16),
                        pltpu.VMEM((2, tm + 2 * HALO, CHUNK_W), F32)],
        compiler_params=_cparams(("arbitrary", "arbitrary")),
        name="inproj",
    )(*args)


RET_GROUP = 8


def _ret_kernel(*refs, seq_len, has_s0, emit_state):
    it = iter(refs)
    q_ref, k_ref, v_ref = next(it), next(it), next(it)
    logit_ref, gnw_ref = next(it), next(it)
    s0_ref = next(it) if has_s0 else None
    o_ref = next(it)
    st_ref = next(it) if emit_state else None
    q_sc, k_sc, st_sc = next(it), next(it), next(it)

    nb = seq_len // BLK
    cross = has_s0 or nb > 1

    def rows(c):
        return slice(c * BLK, (c + 1) * BLK)

    def qk_cols(h):
        return slice(h * DK, (h + 1) * DK)

    def v_cols(h):
        return slice(h * DV, (h + 1) * DV)

    q_sc[...] = q_ref[...].astype(F32)
    k_sc[...] = k_ref[...].astype(F32)

    pos = lax.broadcasted_iota(jnp.int32, (BLK, DK), 0).astype(F32)

    def log_gammas(h):
        lg = _log_sigmoid(logit_ref[h])
        return lg[0:1, :], lg[1:2, :]

    for h in range(HEADS):
        lgf, lgb = log_gammas(h)
        kdec_f = jnp.exp(lgf * (BLK - 1.0 - pos))
        kdec_b = jnp.exp(lgb * pos)
        cdec_f = jnp.exp(lgf[:, 0:1] * float(BLK))
        cdec_b = jnp.exp(lgb[:, 0:1] * float(BLK))
        af = {c: _mm_tn(k_sc[rows(c), qk_cols(h)] * kdec_f, v_ref[rows(c), v_cols(h)])
              for c in range(nb) if c < nb - 1 or emit_state}
        ab = {c: _mm_tn(k_sc[rows(c), qk_cols(h)] * kdec_b, v_ref[rows(c), v_cols(h)])
              for c in range(nb) if c > 0 or emit_state}
        s = s0_ref[0, h] if has_s0 else jnp.zeros((DK, DV), F32)
        for c in range(nb):
            if cross:
                st_sc[h, c, 0:DK, :] = s.astype(BF16)
            if c in af:
                s = s * cdec_f + af[c]
        if emit_state:
            st_ref[0, h] = s
        s = s0_ref[1, h] if has_s0 else jnp.zeros((DK, DV), F32)
        for c in range(nb - 1, -1, -1):
            if cross:
                st_sc[h, c, DK:2 * DK, :] = s.astype(BF16)
            if c in ab:
                s = s * cdec_b + ab[c]
        if emit_state:
            st_ref[1, h] = s

    ii = lax.broadcasted_iota(jnp.int32, (BLK, BLK), 0)
    jj = lax.broadcasted_iota(jnp.int32, (BLK, BLK), 1)
    diff = (ii - jj).astype(F32)
    pos2 = lax.broadcasted_iota(jnp.int32, (BLK, 2 * DK), 0).astype(F32)
    first_half = lax.broadcasted_iota(jnp.int32, (BLK, 2 * DK), 1) < DK

    def head_consts(h):
        lgf, lgb = log_gammas(h)
        lgf1, lgb1 = lgf[:, 0:1], lgb[:, 0:1]
        dmat = (jnp.where(diff >= 0, jnp.exp(lgf1 * jnp.maximum(diff, 0.0)), 0.0)
                + jnp.where(diff <= 0, jnp.exp(lgb1 * jnp.maximum(-diff, 0.0)), 0.0))
        qdec = jnp.where(first_half, jnp.exp(lgf1 * (pos2 + 1.0)), jnp.exp(lgb1 * (BLK - pos2)))
        return dmat, qdec

    def unit(h, c, consts):
        dmat, qdec = consts
        qc = q_sc[rows(c), qk_cols(h)]
        s = _mm_nt(qc, k_sc[rows(c), qk_cols(h)])
        yield
        o = _mm(s * dmat, v_ref[rows(c), v_cols(h)])
        if cross:
            o_cross = _mm(jnp.concatenate([qc, qc], axis=1) * qdec, st_sc[h, c])
        yield
        if cross:
            o = o + o_cross
        mu = jnp.mean(o, axis=-1, keepdims=True)
        oc = o - mu
        var = jnp.mean(oc * oc, axis=-1, keepdims=True)
        o_ref[rows(c), v_cols(h)] = (oc * lax.rsqrt(var + EPS) * gnw_ref[:, v_cols(h)]).astype(o_ref.dtype)
        yield

    units = [(h, c) for h in range(HEADS) for c in range(nb)]
    for g0 in range(0, len(units), RET_GROUP):
        consts = {}
        gens = []
        for h, c in units[g0:g0 + RET_GROUP]:
            if h not in consts:
                consts[h] = head_consts(h)
            gens.append(unit(h, c, consts[h]))
        for _ in zip(*gens):
            pass


def _retention(proj, logit_rep, gn_w, n_seq, seq_len, s0=None, emit_state=False):
    t = proj.shape[0]
    has_s0 = s0 is not None
    kern = functools.partial(_ret_kernel, seq_len=seq_len, has_s0=has_s0, emit_state=emit_state)
    in_specs = [pl.BlockSpec((seq_len, QK_W), lambda b: (b, COL_RQ // QK_W)),
                pl.BlockSpec((seq_len, QK_W), lambda b: (b, COL_RK // QK_W)),
                pl.BlockSpec((seq_len, V_W), lambda b: (b, COL_RV // V_W))]
    args = [proj, proj, proj]
    in_specs += [pl.BlockSpec((HEADS, 2, DK), lambda b: (0, 0, 0)),
                 pl.BlockSpec((1, V_W), lambda b: (0, 0))]
    args += [logit_rep, gn_w]
    state_spec = pl.BlockSpec((None, None, 2, HEADS, DK, DV), lambda b: (b, 0, 0, 0, 0, 0))
    if has_s0:
        in_specs.append(state_spec)
        args.append(s0)
    out_specs = [pl.BlockSpec((seq_len, V_W), lambda b: (b, 0))]
    out_shape = [jax.ShapeDtypeStruct((t, V_W), BF16)]
    if emit_state:
        out_specs.append(state_spec)
        out_shape.append(jax.ShapeDtypeStruct((n_seq, 1, 2, HEADS, DK, DV), F32))
    nb = seq_len // BLK
    return pl.pallas_call(
        kern,
        grid=(n_seq,),
        in_specs=in_specs,
        out_specs=out_specs,
        out_shape=out_shape,
        scratch_shapes=[pltpu.VMEM((seq_len, QK_W), F32), pltpu.VMEM((seq_len, QK_W), F32),
                        pltpu.VMEM((HEADS, nb, 2 * DK, DV), BF16)],
        compiler_params=_cparams(("arbitrary",)),
        name="retention",
    )(*args)


N_MASK = 8
(MASK_EQ_F, MASK_EQ_B, MASK_OFFDIAG, MASK_EYE, MASK_BASE, MASK_PAIR0) = 0, 1, 2, 3, 4, 5


def _build_masks(mask_sc, eye_sc):
    ii = lax.broadcasted_iota(jnp.int32, (BLK, BLK), 0)
    jj = lax.broadcasted_iota(jnp.int32, (BLK, BLK), 1)

    def blk(v, size):
        return lax.shift_right_logical(v, size.bit_length() - 1)

    same = blk(ii, GDN_CHUNK) == blk(jj, GDN_CHUNK)
    mask_sc[MASK_EQ_F] = (same & (jj <= ii)).astype(F32)
    mask_sc[MASK_EQ_B] = (same & (jj >= ii)).astype(F32)
    mask_sc[MASK_OFFDIAG] = (ii != jj).astype(F32)
    mask_sc[MASK_EYE] = (ii == jj).astype(F32)
    mask_sc[MASK_BASE] = (blk(ii, TRI_BASE) == blk(jj, TRI_BASE)).astype(F32)
    size, idx = TRI_BASE, MASK_PAIR0
    while size < GDN_CHUNK:
        pair = (blk(ii, 2 * size) == blk(jj, 2 * size)) & (blk(ii, size) != blk(jj, size))
        mask_sc[idx] = pair.astype(F32)
        size, idx = 2 * size, idx + 1
    eye_sc[...] = (ii == jj).astype(BF16)


def _seg_cumsums(x):
    n = x.shape[1]
    pos = lax.broadcasted_iota(jnp.int32, x.shape, 1) & (GDN_CHUNK - 1)
    up, down = x, x
    s = 1
    while s < GDN_CHUNK:
        up = up + jnp.where(pos >= s, pltpu.roll(up, s, 1), 0.0)
        down = down + jnp.where(pos < GDN_CHUNK - s, pltpu.roll(down, n - s, 1), 0.0)
        s *= 2
    return up, down


def _rows_to_cols(rows, eye_sc):
    p1 = rows.astype(BF16).astype(F32)
    r1 = rows - p1
    p2 = r1.astype(BF16).astype(F32)
    p3 = r1 - p2
    pieces = jnp.concatenate([p1, p2, p3], axis=0).astype(BF16)
    c3 = lax.dot_general(eye_sc[...], pieces, (((1,), (1,)), ((), ())), preferred_element_type=F32)
    return c3[:, 0:SUB] + c3[:, SUB:2 * SUB] + c3[:, 2 * SUB:3 * SUB]


def _gdn_kernel(*refs, nb, has_s0, emit_state):
    it = iter(refs)
    xqk_refs = (next(it), next(it))
    xv_refs = (next(it), next(it))
    g_refs = (next(it), next(it))
    alog_ref, dtb_ref = next(it), next(it)
    s0_ref = next(it) if has_s0 else None
    o_refs = (next(it), next(it))
    st_ref = next(it) if emit_state else None
    st_sc, mask_sc, eye_sc = next(it), next(it), next(it)

    b = pl.program_id(0)
    step = pl.program_id(1)

    @pl.when((b == 0) & (step == 0))
    def _():
        _build_masks(mask_sc, eye_sc)

    @pl.when(step == 0)
    def _():
        for d in range(2):
            for h in range(HEADS):
                st_sc[d, h] = s0_ref[d, h] if has_s0 else jnp.zeros((DK, DV), F32)

    nck = BLK // GDN_CHUNK

    def gates():
        row8 = lax.broadcasted_iota(jnp.int32, (N_GATE, BLK), 0)
        fwd_rows = row8 < HEADS
        logit_b = jnp.where(fwd_rows, g_refs[0][0:N_GATE, :], g_refs[1][0:N_GATE, :])
        logit_a = jnp.where(fwd_rows, g_refs[0][N_GATE:2 * N_GATE, :], g_refs[1][N_GATE:2 * N_GATE, :])
        beta = _sigmoid(logit_b)
        la = -jnp.exp(alog_ref[...]) * _softplus(logit_a + dtb_ref[...])
        up, down = _seg_cumsums(la)
        g8 = jnp.where(fwd_rows, up, down)
        r8 = jnp.where(fwd_rows, down, up) - la
        etot8 = jnp.exp(up + down - la)
        return dict(beta=beta, g8=g8, eg8=jnp.exp(g8), erb8=jnp.exp(r8) * beta, etot8=etot8,
                    gcols=_rows_to_cols(g8, eye_sc))

    gate = {}

    def chain(d, h):
        c = d * HEADS + h
        q16 = xqk_refs[d][:, h * DK:(h + 1) * DK]
        k16 = xqk_refs[d][:, QK_W + h * DK:QK_W + (h + 1) * DK]
        v16 = xv_refs[d][:, h * DV:(h + 1) * DV]
        q = q16.astype(F32)
        k = k16.astype(F32)
        kk = _mm_nt(k16, k16)
        qk = _mm_nt(q16, k16)
        yield
        g_col = gate["gcols"][:, c:c + 1]
        g_row = gate["g8"][c:c + 1, :]
        m_eq = mask_sc[MASK_EQ_F if d == 0 else MASK_EQ_B]
        decay_beta = (jnp.exp(jnp.minimum(g_col - g_row, 0.0)) * m_eq) * gate["beta"][c:c + 1, :]
        lp = kk * decay_beta * mask_sc[MASK_OFFDIAG]
        sc16 = (qk * decay_beta).astype(BF16)
        qg16 = (q * jnp.exp(g_col)).astype(BF16)
        kt16 = (k.T * gate["erb8"][c:c + 1, :]).astype(BF16)

        x = -(lp * mask_sc[MASK_BASE])
        t = mask_sc[MASK_EYE] + x
        for _ in range(TRI_BASE.bit_length() - 2):
            x = _mm(x, x)
            yield
            t = t + _mm(t, x)
            yield
        size, idx = TRI_BASE, MASK_PAIR0
        while size < GDN_CHUNK:
            groups = [slice(r * size, (r + 1) * size) for r in range(BLK // size)]
            late = [(r % 2 == 1) == (d == 0) for r in range(BLK // size)]
            e_late = jnp.concatenate([lp[g, :] * mask_sc[idx, g, :] for g, on in zip(groups, late) if on],
                                     axis=0)
            et = _mm(e_late, t)
            yield
            t_late = jnp.concatenate([t[g, :] for g, on in zip(groups, late) if on], axis=0)
            et_rows = iter(et[j * size:(j + 1) * size, :] for j in range(BLK // size // 2))
            et_full = jnp.concatenate([next(et_rows) if on else jnp.zeros((size, BLK), F32)
                                       for on in late], axis=0)
            t_late = t_late - _mm(t_late, et_full)
            yield
            new_rows = iter(t_late[j * size:(j + 1) * size, :] for j in range(BLK // size // 2))
            t = jnp.concatenate([next(new_rows) if on else t[g, :] for g, on in zip(groups, late)],
                                axis=0)
            size, idx = 2 * size, idx + 1
        u = _mm(t, v16)
        w16 = _mm(t * gate["eg8"][c:c + 1, :], k16).astype(BF16)
        yield

        order = range(nck) if d == 0 else range(nck - 1, -1, -1)
        for cc in order:
            rs = slice(cc * GDN_CHUNK, (cc + 1) * GDN_CHUNK)
            s_prev = st_sc[d, h]
            wq = jnp.concatenate([w16[rs, :], qg16[rs, :]], axis=0)
            ws_qs = _mm(wq, s_prev)
            yield
            v_new = (u[rs, :] - ws_qs[0:GDN_CHUNK]).astype(BF16)
            sk = _mm(jnp.concatenate([sc16[rs, rs], kt16[:, rs]], axis=0), v_new)
            o = ws_qs[GDN_CHUNK:] + sk[0:GDN_CHUNK]
            decay_all = gate["etot8"][c:c + 1, cc * GDN_CHUNK:cc * GDN_CHUNK + 1]
            st_sc[d, h] = s_prev * decay_all + sk[GDN_CHUNK:]
            o_refs[d][rs, h * DV:(h + 1) * DV] = o.astype(o_refs[d].dtype)
            yield

    chains = [chain(d, h) for d in range(2) for h in range(HEADS)]
    for ch in chains:
        next(ch)
    gate.update(gates())
    for _ in zip(*chains):
        pass

    if emit_state:
        @pl.when(step == nb - 1)
        def _():
            for d in range(2):
                for h in range(HEADS):
                    st_ref[d, h] = st_sc[d, h]


def _gdn(proj, gates, a_log, dt_bias, n_seq, seq_len, s0=None, emit_state=False):
    t = proj.shape[0]
    nb = seq_len // BLK
    has_s0 = s0 is not None
    kern = functools.partial(_gdn_kernel, nb=nb, has_s0=has_s0, emit_state=emit_state)

    def fblk(b, s):
        return b * nb + s

    def bblk(b, s):
        return b * nb + nb - 1 - s

    def spec_x(blk, col):
        return pl.BlockSpec((BLK, V_W), lambda b, s: (blk(b, s), col // V_W))

    def spec_g(blk):
        return pl.BlockSpec((2 * N_GATE, BLK), lambda b, s: (0, blk(b, s)))

    const2 = lambda b, s: (0, 0)
    in_specs = [spec_x(fblk, COL_DQK), spec_x(bblk, COL_DQK), spec_x(fblk, COL_DV), spec_x(bblk, COL_DV),
                spec_g(fblk), spec_g(bblk),
                pl.BlockSpec((N_GATE, 1), const2),
                pl.BlockSpec((N_GATE, 1), const2)]
    args = [proj, proj, proj, proj, gates, gates, a_log, dt_bias]
    state_spec = pl.BlockSpec((None, None, 2, HEADS, DK, DV), lambda b, s: (b, 0, 0, 0, 0, 0))
    if has_s0:
        in_specs.append(state_spec)
        args.append(s0)
    out_specs = [pl.BlockSpec((BLK, V_W), lambda b, s: (fblk(b, s), 0)),
                 pl.BlockSpec((BLK, V_W), lambda b, s: (bblk(b, s), 0))]
    out_shape = [jax.ShapeDtypeStruct((t, V_W), BF16), jax.ShapeDtypeStruct((t, V_W), BF16)]
    if emit_state:
        out_specs.append(state_spec)
        out_shape.append(jax.ShapeDtypeStruct((n_seq, 1, 2, HEADS, DK, DV), F32))
    scratch = [pltpu.VMEM((2, HEADS, DK, DV), F32),
               pltpu.VMEM((N_MASK, BLK, BLK), F32),
               pltpu.VMEM((BLK, BLK), BF16)]
    return pl.pallas_call(
        kern,
        grid=(n_seq, nb),
        in_specs=in_specs,
        out_specs=out_specs,
        out_shape=out_shape,
        scratch_shapes=scratch,
        compiler_params=_cparams(("arbitrary", "arbitrary")),
        name="gated_delta",
    )(*args)


def _merge_kernel(x_ref, or_ref, of_ref, ob_ref, rg_ref, dz_ref, gr_ref, gd_ref, mod_ref, nw_ref,
                  dnw_ref, wr_ref, wd_ref, wo_ref, o_ref, *, tm, rows_per_cond):
    i = pl.program_id(0)
    ci = (i * tm) // rows_per_cond
    m = mod_ref[pl.ds(ci, 1), :]
    g1 = m[:, 2 * D_MODEL:3 * D_MODEL]
    y_r = _mm(_silu(rg_ref[...].astype(F32)) * or_ref[...].astype(F32), wr_ref[...])
    dnw = dnw_ref[...]
    heads = []
    for h in range(HEADS):
        cols = slice(h * DV, (h + 1) * DV)
        od = _rms(of_ref[:, cols].astype(F32) + ob_ref[:, cols].astype(F32), dnw)
        heads.append((od * _silu(dz_ref[:, cols].astype(F32))).astype(BF16))
    y_d = jnp.dot(jnp.concatenate(heads, axis=1), wd_ref[...], preferred_element_type=F32)
    merged = _sigmoid(gr_ref[...].astype(F32)) * y_r + _sigmoid(gd_ref[...].astype(F32)) * y_d
    mo = _mm(merged, wo_ref[...])
    o_ref[...] = x_ref[...] + g1 * _rms(mo, nw_ref[1:2, :])


def _merge(x, o_r, o_f, o_b, proj, mod, norm_w, dn_norm_w, w_ret_o, w_dn_o, w_out, rows_per_cond):
    t = x.shape[0]
    tm = 512
    kern = functools.partial(_merge_kernel, tm=tm, rows_per_cond=rows_per_cond)
    row = lambda i: (i, 0)
    const = lambda i: (0, 0)
    wspec = pl.BlockSpec((D_MODEL, D_MODEL), const)
    return pl.pallas_call(
        kern,
        grid=(t // tm,),
        in_specs=[pl.BlockSpec((tm, D_MODEL), row),
                  pl.BlockSpec((tm, V_W), row),
                  pl.BlockSpec((tm, V_W), row),
                  pl.BlockSpec((tm, V_W), row),
                  pl.BlockSpec((tm, V_W), lambda i: (i, COL_RG // V_W)),
                  pl.BlockSpec((tm, V_W), lambda i: (i, COL_DZ // V_W)),
                  pl.BlockSpec((tm, D_MODEL), lambda i: (i, COL_GR // D_MODEL)),
                  pl.BlockSpec((tm, D_MODEL), lambda i: (i, COL_GD // D_MODEL)),
                  pl.BlockSpec(mod.shape, const),
                  pl.BlockSpec(norm_w.shape, const),
                  pl.BlockSpec(dn_norm_w.shape, const),
                  wspec, wspec, wspec],
        out_specs=pl.BlockSpec((tm, D_MODEL), row),
        out_shape=jax.ShapeDtypeStruct((t, D_MODEL), F32),
        compiler_params=_cparams(("arbitrary",)),
        name="merge_out",
    )(x, o_r, o_f, o_b, proj, proj, proj, proj, mod, norm_w, dn_norm_w, w_ret_o, w_dn_o, w_out)


def _ffn_kernel(x_ref, mod_ref, nw_ref, wi_ref, wo_ref, o_ref, *, tm, rows_per_cond):
    i = pl.program_id(0)
    ci = (i * tm) // rows_per_cond
    m = mod_ref[pl.ds(ci, 1), :]
    sh2 = m[:, 3 * D_MODEL:4 * D_MODEL]
    sc2 = m[:, 4 * D_MODEL:5 * D_MODEL]
    g2 = m[:, 5 * D_MODEL:6 * D_MODEL]
    x = x_ref[...]
    hb = (_rms(x, nw_ref[2:3, :]) * (1.0 + sc2) + sh2).astype(BF16)
    f = None
    lo = 0
    for width in FF_CHUNKS:
        gate = jnp.dot(hb, wi_ref[:, lo:lo + width], preferred_element_type=F32)
        up = jnp.dot(hb, wi_ref[:, D_FF + lo:D_FF + lo + width], preferred_element_type=F32)
        part = _mm(_silu(gate) * up, wo_ref[lo:lo + width, :])
        lo += width
        f = part if f is None else f + part
    o_ref[...] = x + g2 * _rms(f, nw_ref[3:4, :])


def _ffn(x, mod, norm_w, w_in, w_out, rows_per_cond):
    t = x.shape[0]
    tm = 512
    kern = functools.partial(_ffn_kernel, tm=tm, rows_per_cond=rows_per_cond)
    row = lambda i: (i, 0)
    const = lambda i: (0, 0)
    resident = pl.Buffered(1)
    return pl.pallas_call(
        kern,
        grid=(t // tm,),
        in_specs=[pl.BlockSpec((tm, D_MODEL), row),
                  pl.BlockSpec(mod.shape, const),
                  pl.BlockSpec(norm_w.shape, const),
                  pl.BlockSpec((D_MODEL, 2 * D_FF), const, pipeline_mode=resident),
                  pl.BlockSpec((D_FF, D_MODEL), const, pipeline_mode=resident)],
        out_specs=pl.BlockSpec((tm, D_MODEL), row),
        out_shape=jax.ShapeDtypeStruct((t, D_MODEL), F32),
        compiler_params=_cparams(("arbitrary",)),
        name="swiglu",
    )(x, mod, norm_w, w_in, w_out)


def _rope_tables(seq_len):
    rows = seq_len // GRID_W
    row_idx = jnp.repeat(jnp.arange(rows, dtype=F32), GRID_W)
    col_idx = (jnp.arange(seq_len) % GRID_W).astype(F32)
    n_freq = DK // 4
    freqs = ROPE_BASE ** (-jnp.arange(n_freq, dtype=F32) / n_freq)
    ang = jnp.concatenate([row_idx[:, None] * freqs, col_idx[:, None] * freqs], axis=-1)
    cos = jnp.repeat(jnp.cos(ang), 2, axis=-1)
    sin = jnp.repeat(jnp.sin(ang), 2, axis=-1)
    sign = jnp.tile(jnp.array([-1.0, 1.0], F32), DK // 2)
    return cos, sin * sign


def _one_path(x, mod, seq_len, n_seq, wts, rope_tabs, s_ret0, s_dn0, emit_state):
    rows_per_cond = seq_len if mod.shape[0] > 1 else x.shape[0]
    proj, gates = _inproj(x, mod, wts["norm_w"], wts["w_in_t"], wts["conv_w"], rows_per_cond,
                          seq_len, rope_tabs)
    ret = _retention(proj, wts["logit_rep"], wts["ret_gn_w"], n_seq, seq_len,
                     s0=s_ret0, emit_state=emit_state)
    gdn = _gdn(proj, gates, wts["a_log"], wts["dt_bias"], n_seq, seq_len,
               s0=s_dn0, emit_state=emit_state)
    x1 = _merge(x, ret[0], gdn[0], gdn[1], proj, mod, wts["norm_w"], wts["dn_norm_w"],
                wts["w_ret_o"], wts["w_dn_o"], wts["w_out"], rows_per_cond)
    y = _ffn(x1, mod, wts["norm_w"], wts["w_ffn_in"], wts["w_ffn_out"], rows_per_cond)
    if emit_state:
        return y, ret[1], gdn[2]
    return y, None, None


def kernel(x_prompt, x_sample, c, state_ret, state_dn, c_ctx, w_mod, b_mod, norm_w, w_in, conv_w,
           ret_decay_logit, ret_gn_w, dn_a_log, dn_dt_bias, dn_norm_w, w_ret_o, w_dn_o, w_out,
           w_ffn_in, w_ffn_out):
    n_ctx, l_ctx, _ = x_prompt.shape
    n_lat, l_lat, _ = x_sample.shape
    assert w_mod.shape[0] == 1, "single-layer kernel"

    cond8 = jnp.zeros((8, D_MODEL), F32).at[0].set(c_ctx).at[1:1 + n_lat].set(c)
    mod = _modulation(cond8, w_mod[0], b_mod)
    mod_ctx, mod_lat = mod[0:1], mod[1:1 + n_lat]

    wts = {
        "norm_w": norm_w[0],
        "w_in_t": w_in[0].T.astype(BF16),
        "logit_rep": jnp.broadcast_to(ret_decay_logit[0].T[:, :, None], (HEADS, 2, DK)),
        "ret_gn_w": ret_gn_w,
        "conv_w": conv_w[0],
        "a_log": dn_a_log.reshape(N_GATE, 1),
        "dt_bias": dn_dt_bias.reshape(N_GATE, 1),
        "dn_norm_w": dn_norm_w,
        "w_ret_o": w_ret_o[0].astype(BF16),
        "w_dn_o": w_dn_o[0].astype(BF16),
        "w_out": w_out[0].astype(BF16),
        "w_ffn_in": w_ffn_in[0].astype(BF16),
        "w_ffn_out": w_ffn_out[0].astype(BF16),
    }

    y_p, s_ret, s_dn = _one_path(x_prompt.reshape(n_ctx * l_ctx, D_MODEL), mod_ctx, l_ctx, n_ctx,
                                 wts, None, None, None, True)
    y_s, _, _ = _one_path(x_sample.reshape(n_lat * l_lat, D_MODEL), mod_lat, l_lat, n_lat,
                          wts, _rope_tables(l_lat), state_ret, state_dn, False)
    return (y_p.reshape(x_prompt.shape), y_s.reshape(x_sample.shape), s_ret, s_dn)
```

```python
import functools

import jax
import jax.numpy as jnp
from jax import lax
from jax.experimental import pallas as pl
from jax.experimental.pallas import tpu as pltpu

F32 = jnp.float32
BF16 = jnp.bfloat16

D_MODEL = 1024
HEADS = 4
DK = 128
DV = 256
QK_W = HEADS * DK
V_W = HEADS * DV
GDN_CHUNK = 64
TRI_BASE = 8
BLK = 256
SUB = 8
LANES = 128
HALO = 16
CONV_WIDTH = 3
GRID_W = 64
ROPE_BASE = 10000.0
EPS = 1e-6
D_FF = 2816
MXU_DIM = 256
FF_CHUNKS = (6 * MXU_DIM, 5 * MXU_DIM)
assert sum(FF_CHUNKS) == D_FF
N_GATE = 2 * HEADS

COL_RQ, COL_RK, COL_RV, COL_RG = 0, 512, 1024, 2048
COL_DQK, COL_DV, COL_DZ = 3072, 4096, 5120
COL_GATE = 6144
COL_GR, COL_GD = 6144, 7168
PROJ_W = 8192
PROJ_TN = 4096

VMEM_LIMIT = 56 * 1024 * 1024


def _cparams(sem):
    return pltpu.CompilerParams(dimension_semantics=sem, vmem_limit_bytes=VMEM_LIMIT)


def _mm(a, b):
    return jnp.dot(a.astype(BF16), b.astype(BF16), preferred_element_type=F32)


def _mm_nt(a, b):
    return lax.dot_general(a.astype(BF16), b.astype(BF16), (((1,), (1,)), ((), ())),
                           preferred_element_type=F32)


def _mm_tn(a, b):
    return lax.dot_general(a.astype(BF16), b.astype(BF16), (((0,), (0,)), ((), ())),
                           preferred_element_type=F32)


def _sigmoid(x):
    return 1.0 / (1.0 + jnp.exp(-x))


def _silu(x):
    return x * _sigmoid(x)


def _softplus(x):
    return jnp.maximum(x, 0.0) + jnp.log(1.0 + jnp.exp(-jnp.abs(x)))


def _log_sigmoid(x):
    return -_softplus(-x)


def _rms(x, w):
    return x * lax.rsqrt(jnp.mean(x * x, axis=-1, keepdims=True) + EPS) * w


def _mod_kernel(c_ref, w_ref, b_ref, o_ref):
    o_ref[...] = _mm(_silu(c_ref[...]), w_ref[...]) + b_ref[...]


def _modulation(cond8, w_mod, b_mod):
    n = w_mod.shape[1]
    tn = 1536
    return pl.pallas_call(
        _mod_kernel,
        grid=(n // tn,),
        in_specs=[pl.BlockSpec((8, D_MODEL), lambda j: (0, 0)),
                  pl.BlockSpec((D_MODEL, tn), lambda j: (0, j)),
                  pl.BlockSpec((1, tn), lambda j: (0, j))],
        out_specs=pl.BlockSpec((8, tn), lambda j: (0, j)),
        out_shape=jax.ShapeDtypeStruct((8, n), F32),
        compiler_params=_cparams(("arbitrary",)),
        name="modulation",
    )(cond8, w_mod, b_mod)


CHUNK_W = 256
SUBROWS = 128
PLAIN, RET_Q, RET_K, DN_Q, DN_K, DN_V = range(6)
CHUNK_KINDS = ((RET_Q, RET_Q, RET_K, RET_K) + (PLAIN,) * 4
               + (PLAIN,) * 4 + (DN_Q, DN_Q, DN_K, DN_K)
               + (DN_V,) * 4 + (PLAIN,) * 4
               + (PLAIN,) * 8)
assert len(CHUNK_KINDS) * CHUNK_W == PROJ_W and PROJ_TN % CHUNK_W == 0


def _pair_swap(x):
    lane = lax.broadcasted_iota(jnp.int32, x.shape, 1)
    nxt = pltpu.roll(x, x.shape[1] - 1, 1)
    prv = pltpu.roll(x, 1, 1)
    return jnp.where((lane & 1) == 0, nxt, prv)


def _inproj_kernel(*refs, tm, rows_per_cond, seq_len, rope):
    it = iter(refs)
    x_ref, xp_ref, xn_ref, mod_ref, nw_ref, w_ref, cw_ref = (next(it) for _ in range(7))
    cos_ref = next(it) if rope else None
    sin_ref = next(it) if rope else None
    o_ref, g_ref, h_sc, p_sc = next(it), next(it), next(it), next(it)

    i = pl.program_id(0)
    j = pl.program_id(1)
    t0 = i * tm
    tile_rows = slice(HALO, HALO + tm)

    @pl.when(j == 0)
    def _():
        ci = t0 // rows_per_cond
        m = mod_ref[pl.ds(ci, 1), :]
        sh1 = m[:, 0:D_MODEL]
        sc1 = m[:, D_MODEL:2 * D_MODEL]

        def pre(x):
            return (_rms(x, nw_ref[0:1, :]) * (1.0 + sc1) + sh1).astype(BF16)

        hb = pre(x_ref[...])
        h_sc[tile_rows, :] = hb
        h_sc[0:HALO, :] = pre(xp_ref[...])
        h_sc[HALO + tm:2 * HALO + tm, :] = pre(xn_ref[...])
        g_ref[...] = _mm_nt(w_ref[COL_GATE:COL_GATE + 2 * N_GATE, :], hb)

    seq_start = (t0 & (seq_len - 1)) == 0
    seq_end = ((t0 + tm) & (seq_len - 1)) == 0

    def conv_silu(win, r0, cw_lo):
        width = win.shape[1]
        inner = slice(SUB, SUB + SUBROWS)
        cur = win[inner]
        prev = pltpu.roll(win, 1, 0)[inner]
        nxt = pltpu.roll(win, win.shape[0] - 1, 0)[inner]
        row = lax.broadcasted_iota(jnp.int32, cur.shape, 0)
        if seq_len >= tm:
            if r0 == 0:
                prev = jnp.where((row == 0) & seq_start, 0.0, prev)
            if r0 + SUBROWS == tm:
                nxt = jnp.where((row == SUBROWS - 1) & seq_end, 0.0, nxt)
        else:
            if r0 % seq_len == 0:
                prev = jnp.where(row == 0, 0.0, prev)
            if (r0 + SUBROWS) % seq_len == 0:
                nxt = jnp.where(row == SUBROWS - 1, 0.0, nxt)
        cw = cw_ref[:, cw_lo:cw_lo + width]
        return _silu(prev * cw[0:1, :] + cur * cw[1:2, :] + nxt * cw[2:3, :])

    def l2n_heads(y, scale):
        outs = []
        for lo in range(0, y.shape[1], DK):
            yh = y[:, lo:lo + DK]
            outs.append(yh * (lax.rsqrt(jnp.sum(yh * yh, axis=-1, keepdims=True) + EPS) * scale))
        return jnp.concatenate(outs, axis=1)

    def rotary(p, r0):
        reps = p.shape[1] // DK
        cos = jnp.concatenate([cos_ref[r0:r0 + SUBROWS, :]] * reps, axis=1)
        sin = jnp.concatenate([sin_ref[r0:r0 + SUBROWS, :]] * reps, axis=1)
        return p * cos + _pair_swap(p) * sin

    def tile(col0, kinds):
        n_staged = 0
        for c, kind in enumerate(kinds):
            lo = c * CHUNK_W
            col = col0 + lo
            w_row = col if col < COL_GATE else col + 2 * N_GATE
            w = w_ref[w_row:w_row + CHUNK_W, :]
            conv = kind in (DN_Q, DN_K, DN_V)
            if not conv and not (rope and kind in (RET_Q, RET_K)):
                y = _mm_nt(h_sc[tile_rows, :], w)
                if kind == RET_Q:
                    y = y * (DK ** -0.5)
                o_ref[:, lo:lo + CHUNK_W] = y.astype(o_ref.dtype)
                continue
            buf = p_sc.at[n_staged % 2]
            n_staged += 1
            if conv:
                buf[...] = _mm_nt(h_sc[...], w)
                cw_lo = col - COL_DQK
            else:
                buf[tile_rows, :] = _mm_nt(h_sc[tile_rows, :], w)
            for r0 in range(0, tm, SUBROWS):
                if conv:
                    y = conv_silu(buf[HALO + r0 - SUB:HALO + r0 + SUBROWS + SUB, :], r0, cw_lo)
                    if kind == DN_Q:
                        y = l2n_heads(y, DK ** -0.5)
                    elif kind == DN_K:
                        y = l2n_heads(y, 1.0)
                else:
                    y = buf[HALO + r0:HALO + r0 + SUBROWS, :]
                    if kind == RET_Q:
                        y = y * (DK ** -0.5)
                    y = rotary(y, r0)
                o_ref[r0:r0 + SUBROWS, lo:lo + CHUNK_W] = y.astype(o_ref.dtype)

    per_tile = PROJ_TN // CHUNK_W
    for jt in range(PROJ_W // PROJ_TN):
        kinds = CHUNK_KINDS[jt * per_tile:(jt + 1) * per_tile]

        @pl.when(j == jt)
        def _(jt=jt, kinds=kinds):
            tile(jt * PROJ_TN, kinds)


def _inproj(x, mod, norm_w, w_t, conv_w, rows_per_cond, seq_len, rope_tabs):
    t = x.shape[0]
    tm, tn = 1024, PROJ_TN
    rope = rope_tabs is not None
    assert seq_len & (seq_len - 1) == 0 and (tm % seq_len == 0 or seq_len % tm == 0)
    kern = functools.partial(_inproj_kernel, tm=tm, rows_per_cond=rows_per_cond,
                             seq_len=seq_len, rope=rope)
    halo_per_tile = tm // HALO
    n_halo = t // HALO
    in_specs = [pl.BlockSpec((tm, D_MODEL), lambda i, j: (i, 0)),
                pl.BlockSpec((HALO, D_MODEL), lambda i, j: (jnp.maximum(i * halo_per_tile - 1, 0), 0)),
                pl.BlockSpec((HALO, D_MODEL),
                             lambda i, j: (jnp.minimum((i + 1) * halo_per_tile, n_halo - 1), 0)),
                pl.BlockSpec(mod.shape, lambda i, j: (0, 0)),
                pl.BlockSpec(norm_w.shape, lambda i, j: (0, 0)),
                pl.BlockSpec(w_t.shape, lambda i, j: (0, 0), pipeline_mode=pl.Buffered(1)),
                pl.BlockSpec(conv_w.shape, lambda i, j: (0, 0))]
    args = [x, x, x, mod, norm_w, w_t, conv_w]
    if rope:
        tiles_per_seq = max(seq_len // tm, 1)
        in_specs += [pl.BlockSpec((tm, DK), lambda i, j: (i % tiles_per_seq, 0))] * 2
        args += list(rope_tabs)
    return pl.pallas_call(
        kern,
        grid=(t // tm, PROJ_W // tn),
        in_specs=in_specs,
        out_specs=[pl.BlockSpec((tm, tn), lambda i, j: (i, j)),
                   pl.BlockSpec((2 * N_GATE, tm), lambda i, j: (0, i))],
        out_shape=[jax.ShapeDtypeStruct((t, PROJ_W), BF16),
                   jax.ShapeDtypeStruct((2 * N_GATE, t), F32)],
        scratch_shapes=[pltpu.VMEM((tm + 2 * HALO, D_MODEL), BF16),
                        pltpu.VMEM((2, tm + 2 * HALO, CHUNK_W), F32)],
        compiler_params=_cparams(("arbitrary", "arbitrary")),
        name="inproj",
    )(*args)


RET_GROUP = 8


def _ret_kernel(*refs, seq_len, has_s0, emit_state):
    it = iter(refs)
    q_ref, k_ref, v_ref = next(it), next(it), next(it)
    logit_ref, gnw_ref = next(it), next(it)
    s0_ref = next(it) if has_s0 else None
    o_ref = next(it)
    st_ref = next(it) if emit_state else None
    st_sc = next(it)

    nb = seq_len // BLK
    cross = has_s0 or nb > 1

    def rows(c):
        return slice(c * BLK, (c + 1) * BLK)

    def qk_cols(h):
        return slice(h * DK, (h + 1) * DK)

    def v_cols(h):
        return slice(h * DV, (h + 1) * DV)

    pos =lax.broadcasted_iota(jnp.int32, (BLK, DK), 0).astype(F32)

    def log_gammas(h):
        lg = _log_sigmoid(logit_ref[h])
        return lg[0:1, :], lg[1:2, :]

    for h in range(HEADS):
        lgf, lgb = log_gammas(h)
        kdec_f = jnp.exp(lgf * (BLK - 1.0 - pos))
        kdec_b = jnp.exp(lgb * pos)
        cdec_f = jnp.exp(lgf[:, 0:1] * float(BLK))
        cdec_b = jnp.exp(lgb[:, 0:1] * float(BLK))
        af = {c: _mm_tn(k_ref[rows(c), qk_cols(h)].astype(F32) * kdec_f, v_ref[rows(c), v_cols(h)])
              for c in range(nb) if c < nb - 1 or emit_state}
        ab = {c: _mm_tn(k_ref[rows(c), qk_cols(h)].astype(F32) * kdec_b, v_ref[rows(c), v_cols(h)])
              for c in range(nb) if c > 0 or emit_state}
        s = s0_ref[0, h] if has_s0 else jnp.zeros((DK, DV), F32)
        for c in range(nb):
            if cross:
                st_sc[h, c, 0:DK, :] = s.astype(BF16)
            if c in af:
                s = s * cdec_f + af[c]
        if emit_state:
            st_ref[0, h] = s
        s = s0_ref[1, h] if has_s0 else jnp.zeros((DK, DV), F32)
        for c in range(nb - 1, -1, -1):
            if cross:
                st_sc[h, c, DK:2 * DK, :] = s.astype(BF16)
            if c in ab:
                s = s * cdec_b + ab[c]
        if emit_state:
            st_ref[1, h] = s

    ii = lax.broadcasted_iota(jnp.int32, (BLK, BLK), 0)
    jj = lax.broadcasted_iota(jnp.int32, (BLK, BLK), 1)
    diff = (ii - jj).astype(F32)
    pos2 = lax.broadcasted_iota(jnp.int32, (BLK, 2 * DK), 0).astype(F32)
    first_half = lax.broadcasted_iota(jnp.int32, (BLK, 2 * DK), 1) < DK

    def head_consts(h):
        lgf, lgb = log_gammas(h)
        lgf1, lgb1 = lgf[:, 0:1], lgb[:, 0:1]
        dmat = (jnp.where(diff >= 0, jnp.exp(lgf1 * jnp.maximum(diff, 0.0)), 0.0)
                + jnp.where(diff <= 0, jnp.exp(lgb1 * jnp.maximum(-diff, 0.0)), 0.0))
        qdec = jnp.where(first_half, jnp.exp(lgf1 * (pos2 + 1.0)), jnp.exp(lgb1 * (BLK - pos2)))
        return dmat, qdec

    def unit(h, c, consts):
        dmat, qdec = consts
        qc = q_ref[rows(c), qk_cols(h)]
        s = _mm_nt(qc, k_ref[rows(c), qk_cols(h)])
        yield
        o = _mm(s * dmat, v_ref[rows(c), v_cols(h)])
        if cross:
            o_cross = _mm(jnp.concatenate([qc, qc], axis=1).astype(F32) * qdec, st_sc[h, c])
        yield
        if cross:
            o = o + o_cross
        mu = jnp.mean(o, axis=-1, keepdims=True)
        oc = o - mu
        var = jnp.mean(oc * oc, axis=-1, keepdims=True)
        o_ref[rows(c), v_cols(h)] = (oc * lax.rsqrt(var + EPS) * gnw_ref[:, v_cols(h)]).astype(o_ref.dtype)
        yield

    units = [(h, c) for h in range(HEADS) for c in range(nb)]
    for g0 in range(0, len(units), RET_GROUP):
        consts = {}
        gens = []
        for h, c in units[g0:g0 + RET_GROUP]:
            if h not in consts:
                consts[h] = head_consts(h)
            gens.append(unit(h, c, consts[h]))
        for _ in zip(*gens):
            pass


def _retention(proj, logit_rep, gn_w, n_seq, seq_len, s0=None, emit_state=False):
    t = proj.shape[0]
    has_s0 = s0 is not None
    kern = functools.partial(_ret_kernel, seq_len=seq_len, has_s0=has_s0, emit_state=emit_state)
    in_specs = [pl.BlockSpec((seq_len, QK_W), lambda b: (b, COL_RQ // QK_W)),
                pl.BlockSpec((seq_len, QK_W), lambda b: (b, COL_RK // QK_W)),
                pl.BlockSpec((seq_len, V_W), lambda b: (b, COL_RV // V_W))]
    args = [proj, proj, proj]
    in_specs += [pl.BlockSpec((HEADS, 2, DK), lambda b: (0, 0, 0)),
                 pl.BlockSpec((1, V_W), lambda b: (0, 0))]
    args += [logit_rep, gn_w]
    state_spec = pl.BlockSpec((None, None, 2, HEADS, DK, DV), lambda b: (b, 0, 0, 0, 0, 0))
    if has_s0:
        in_specs.append(state_spec)
        args.append(s0)
    out_specs = [pl.BlockSpec((seq_len, V_W), lambda b: (b, 0))]
    out_shape = [jax.ShapeDtypeStruct((t, V_W), BF16)]
    if emit_state:
        out_specs.append(state_spec)
        out_shape.append(jax.ShapeDtypeStruct((n_seq, 1, 2, HEADS, DK, DV), F32))
    nb = seq_len // BLK
    return pl.pallas_call(
        kern,
        grid=(n_seq,),
        in_specs=in_specs,
        out_specs=out_specs,
        out_shape=out_shape,
        scratch_shapes=[pltpu.VMEM((HEADS, nb, 2 * DK, DV), BF16)],
        compiler_params=_cparams(("arbitrary",)),
        name="retention",
    )(*args)


N_MASK = 8
(MASK_EQ_F, MASK_EQ_B, MASK_OFFDIAG, MASK_EYE, MASK_BASE, MASK_PAIR0) = 0, 1, 2, 3, 4, 5


def _build_masks(mask_sc, eye_sc):
    ii = lax.broadcasted_iota(jnp.int32, (BLK, BLK), 0)
    jj = lax.broadcasted_iota(jnp.int32, (BLK, BLK), 1)

    def blk(v, size):
        return lax.shift_right_logical(v, size.bit_length() - 1)

    same = blk(ii, GDN_CHUNK) == blk(jj, GDN_CHUNK)
    mask_sc[MASK_EQ_F] = (same & (jj <= ii)).astype(F32)
    mask_sc[MASK_EQ_B] = (same & (jj >= ii)).astype(F32)
    mask_sc[MASK_OFFDIAG] = (ii != jj).astype(F32)
    mask_sc[MASK_EYE] = (ii == jj).astype(F32)
    mask_sc[MASK_BASE] = (blk(ii, TRI_BASE) == blk(jj, TRI_BASE)).astype(F32)
    size, idx = TRI_BASE, MASK_PAIR0
    while size < GDN_CHUNK:
        pair = (blk(ii, 2 * size) == blk(jj, 2 * size)) & (blk(ii, size) != blk(jj, size))
        mask_sc[idx] = pair.astype(F32)
        size, idx = 2 * size, idx + 1
    eye_sc[...] = (ii == jj).astype(BF16)


def _seg_cumsums(x):
    n = x.shape[1]
    pos = lax.broadcasted_iota(jnp.int32, x.shape, 1) & (GDN_CHUNK - 1)
    up, down = x, x
    s = 1
    while s < GDN_CHUNK:
        up = up + jnp.where(pos >= s, pltpu.roll(up, s, 1), 0.0)
        down = down + jnp.where(pos < GDN_CHUNK - s, pltpu.roll(down, n - s, 1), 0.0)
        s *= 2
    return up, down


def _rows_to_cols(rows, eye_sc):
    p1 = rows.astype(BF16).astype(F32)
    r1 = rows - p1
    p2 = r1.astype(BF16).astype(F32)
    p3 = r1 - p2
    pieces = jnp.concatenate([p1, p2, p3], axis=0).astype(BF16)
    c3 = lax.dot_general(eye_sc[...], pieces, (((1,), (1,)), ((), ())), preferred_element_type=F32)
    return c3[:, 0:SUB] + c3[:, SUB:2 * SUB] + c3[:, 2 * SUB:3 * SUB]


def _gdn_kernel(*refs, nb, has_s0, emit_state):
    it = iter(refs)
    xqk_refs = (next(it), next(it))
    xv_refs = (next(it), next(it))
    g_refs = (next(it), next(it))
    alog_ref, dtb_ref = next(it), next(it)
    s0_ref = next(it) if has_s0 else None
    o_refs = (next(it), next(it))
    st_ref = next(it) if emit_state else None
    st_sc, mask_sc, eye_sc = next(it), next(it), next(it)

    b = pl.program_id(0)
    step = pl.program_id(1)

    @pl.when((b == 0) & (step == 0))
    def _():
        _build_masks(mask_sc, eye_sc)

    @pl.when(step == 0)
    def _():
        for d in range(2):
            for h in range(HEADS):
                st_sc[d, h] = s0_ref[d, h] if has_s0 else jnp.zeros((DK, DV), F32)

    nck = BLK // GDN_CHUNK

    def gates():
        row8 = lax.broadcasted_iota(jnp.int32, (N_GATE, BLK), 0)
        fwd_rows = row8 < HEADS
        logit_b = jnp.where(fwd_rows, g_refs[0][0:N_GATE, :], g_refs[1][0:N_GATE, :])
        logit_a = jnp.where(fwd_rows, g_refs[0][N_GATE:2 * N_GATE, :], g_refs[1][N_GATE:2 * N_GATE, :])
        beta = _sigmoid(logit_b)
        la = -jnp.exp(alog_ref[...]) * _softplus(logit_a + dtb_ref[...])
        up, down = _seg_cumsums(la)
        g8 = jnp.where(fwd_rows, up, down)
        r8 = jnp.where(fwd_rows, down, up) - la
        etot8 = jnp.exp(up + down - la)
        return dict(beta=beta, g8=g8, eg8=jnp.exp(g8), erb8=jnp.exp(r8) * beta, etot8=etot8,
                    gcols=_rows_to_cols(g8, eye_sc))

    gate = {}

    def chain(d, h):
        c = d * HEADS + h
        q16 = xqk_refs[d][:, h * DK:(h + 1) * DK]
        k16 = xqk_refs[d][:, QK_W + h * DK:QK_W + (h + 1) * DK]
        v16 = xv_refs[d][:, h * DV:(h + 1) * DV]
        q = q16.astype(F32)
        k = k16.astype(F32)
        kk = _mm_nt(k16, k16)
        qk = _mm_nt(q16, k16)
        yield
        g_col = gate["gcols"][:, c:c + 1]
        g_row = gate["g8"][c:c + 1, :]
        m_eq = mask_sc[MASK_EQ_F if d == 0 else MASK_EQ_B]
        decay_beta = (jnp.exp(jnp.minimum(g_col - g_row, 0.0)) * m_eq) * gate["beta"][c:c + 1, :]
        lp = kk * decay_beta * mask_sc[MASK_OFFDIAG]
        sc16 = (qk * decay_beta).astype(BF16)
        qg16 = (q * jnp.exp(g_col)).astype(BF16)
        kt16 = (k.T * gate["erb8"][c:c + 1, :]).astype(BF16)

        x = -(lp * mask_sc[MASK_BASE])
        t = mask_sc[MASK_EYE] + x
        for _ in range(TRI_BASE.bit_length() - 2):
            x = _mm(x, x)
            yield
            t = t + _mm(t, x)
            yield
        size, idx = TRI_BASE, MASK_PAIR0
        while size < GDN_CHUNK:
            groups = [slice(r * size, (r + 1) * size) for r in range(BLK // size)]
            late = [(r % 2 == 1) == (d == 0) for r in range(BLK // size)]
            e_late = jnp.concatenate([lp[g, :] * mask_sc[idx, g, :] for g, on in zip(groups, late) if on],
                                     axis=0)
            et = _mm(e_late, t)
            yield
            t_late = jnp.concatenate([t[g, :] for g, on in zip(groups, late) if on], axis=0)
            et_rows = iter(et[j * size:(j + 1) * size, :] for j in range(BLK // size // 2))
            et_full = jnp.concatenate([next(et_rows) if on else jnp.zeros((size, BLK), F32)
                                       for on in late], axis=0)
            t_late = t_late - _mm(t_late, et_full)
            yield
            new_rows = iter(t_late[j * size:(j + 1) * size, :] for j in range(BLK // size // 2))
            t = jnp.concatenate([next(new_rows) if on else t[g, :] for g, on in zip(groups, late)],
                                axis=0)
            size, idx = 2 * size, idx + 1
        u = _mm(t, v16)
        w16 = _mm(t * gate["eg8"][c:c + 1, :], k16).astype(BF16)
        yield

        order = range(nck) if d == 0 else range(nck - 1, -1, -1)
        for cc in order:
            rs = slice(cc * GDN_CHUNK, (cc + 1) * GDN_CHUNK)
            s_prev = st_sc[d, h]
            wq = jnp.concatenate([w16[rs, :], qg16[rs, :]], axis=0)
            ws_qs = _mm(wq, s_prev)
            yield
            v_new = (u[rs, :] - ws_qs[0:GDN_CHUNK]).astype(BF16)
            sk = _mm(jnp.concatenate([sc16[rs, rs], kt16[:, rs]], axis=0), v_new)
            o = ws_qs[GDN_CHUNK:] + sk[0:GDN_CHUNK]
            decay_all = gate["etot8"][c:c + 1, cc * GDN_CHUNK:cc * GDN_CHUNK + 1]
            st_sc[d, h] = s_prev * decay_all + sk[GDN_CHUNK:]
            o_refs[d][rs, h * DV:(h + 1) * DV] = o.astype(o_refs[d].dtype)
            yield

    chains = [chain(d, h) for d in range(2) for h in range(HEADS)]
    for ch in chains:
        next(ch)
    gate.update(gates())
    for _ in zip(*chains):
        pass

    if emit_state:
        @pl.when(step == nb - 1)
        def _():
            for d in range(2):
                for h in range(HEADS):
                    st_ref[d, h] = st_sc[d, h]


def _gdn(proj, gates, a_log, dt_bias, n_seq, seq_len, s0=None, emit_state=False):
    t = proj.shape[0]
    nb = seq_len // BLK
    has_s0 = s0 is not None
    kern = functools.partial(_gdn_kernel, nb=nb, has_s0=has_s0, emit_state=emit_state)

    def fblk(b, s):
        return b * nb + s

    def bblk(b, s):
        return b * nb + nb - 1 - s

    def spec_x(blk, col):
        return pl.BlockSpec((BLK, V_W), lambda b, s: (blk(b, s), col // V_W))

    def spec_g(blk):
        return pl.BlockSpec((2 * N_GATE, BLK), lambda b, s: (0, blk(b, s)))

    const2 = lambda b, s: (0, 0)
    in_specs = [spec_x(fblk, COL_DQK), spec_x(bblk, COL_DQK), spec_x(fblk, COL_DV), spec_x(bblk, COL_DV),
                spec_g(fblk), spec_g(bblk),
                pl.BlockSpec((N_GATE, 1), const2),
                pl.BlockSpec((N_GATE, 1), const2)]
    args = [proj, proj, proj, proj, gates, gates, a_log, dt_bias]
    state_spec = pl.BlockSpec((None, None, 2, HEADS, DK, DV), lambda b, s: (b, 0, 0, 0, 0, 0))
    if has_s0:
        in_specs.append(state_spec)
        args.append(s0)
    out_specs = [pl.BlockSpec((BLK, V_W), lambda b, s: (fblk(b, s), 0)),
                 pl.BlockSpec((BLK, V_W), lambda b, s: (bblk(b, s), 0))]
    out_shape = [jax.ShapeDtypeStruct((t, V_W), BF16), jax.ShapeDtypeStruct((t, V_W), BF16)]
    if emit_state:
        out_specs.append(state_spec)
        out_shape.append(jax.ShapeDtypeStruct((n_seq, 1, 2, HEADS, DK, DV), F32))
    scratch = [pltpu.VMEM((2, HEADS, DK, DV), F32),
               pltpu.VMEM((N_MASK, BLK, BLK), F32),
               pltpu.VMEM((BLK, BLK), BF16)]
    return pl.pallas_call(
        kern,
        grid=(n_seq, nb),
        in_specs=in_specs,
        out_specs=out_specs,
        out_shape=out_shape,
        scratch_shapes=scratch,
        compiler_params=_cparams(("arbitrary", "arbitrary")),
        name="gated_delta",
    )(*args)


def _merge_kernel(x_ref, or_ref, of_ref, ob_ref, rg_ref, dz_ref, gr_ref, gd_ref, mod_ref, nw_ref,
                  dnw_ref, wr_ref, wd_ref, wo_ref, o_ref, *, tm, rows_per_cond):
    i = pl.program_id(0)
    ci = (i * tm) // rows_per_cond
    m = mod_ref[pl.ds(ci, 1), :]
    g1 = m[:, 2 * D_MODEL:3 * D_MODEL]
    y_r = _mm(_silu(rg_ref[...].astype(F32)) * or_ref[...].astype(F32), wr_ref[...])
    dnw = dnw_ref[...]
    heads = []
    for h in range(HEADS):
        cols = slice(h * DV, (h + 1) * DV)
        od = _rms(of_ref[:, cols].astype(F32) + ob_ref[:, cols].astype(F32), dnw)
        heads.append((od * _silu(dz_ref[:, cols].astype(F32))).astype(BF16))
    y_d = jnp.dot(jnp.concatenate(heads, axis=1), wd_ref[...], preferred_element_type=F32)
    merged = _sigmoid(gr_ref[...].astype(F32)) * y_r + _sigmoid(gd_ref[...].astype(F32)) * y_d
    mo = _mm(merged, wo_ref[...])
    o_ref[...] = x_ref[...] + g1 * _rms(mo, nw_ref[1:2, :])


def _merge(x, o_r, o_f, o_b, proj, mod, norm_w, dn_norm_w, w_ret_o, w_dn_o, w_out, rows_per_cond):
    t = x.shape[0]
    tm = 512
    kern = functools.partial(_merge_kernel, tm=tm, rows_per_cond=rows_per_cond)
    row = lambda i: (i, 0)
    const = lambda i: (0, 0)
    wspec = pl.BlockSpec((D_MODEL, D_MODEL), const)
    return pl.pallas_call(
        kern,
        grid=(t // tm,),
        in_specs=[pl.BlockSpec((tm, D_MODEL), row),
                  pl.BlockSpec((tm, V_W), row),
                  pl.BlockSpec((tm, V_W), row),
                  pl.BlockSpec((tm, V_W), row),
                  pl.BlockSpec((tm, V_W), lambda i: (i, COL_RG // V_W)),
                  pl.BlockSpec((tm, V_W), lambda i: (i, COL_DZ // V_W)),
                  pl.BlockSpec((tm, D_MODEL), lambda i: (i, COL_GR // D_MODEL)),
                  pl.BlockSpec((tm, D_MODEL), lambda i: (i, COL_GD // D_MODEL)),
                  pl.BlockSpec(mod.shape, const),
                  pl.BlockSpec(norm_w.shape, const),
                  pl.BlockSpec(dn_norm_w.shape, const),
                  wspec, wspec, wspec],
        out_specs=pl.BlockSpec((tm, D_MODEL), row),
        out_shape=jax.ShapeDtypeStruct((t, D_MODEL), F32),
        compiler_params=_cparams(("arbitrary",)),
        name="merge_out",
    )(x, o_r, o_f, o_b, proj, proj, proj, proj, mod, norm_w, dn_norm_w, w_ret_o, w_dn_o, w_out)


def _ffn_kernel(x_ref, mod_ref, nw_ref, wi_ref, wo_ref, o_ref, *, tm, rows_per_cond):
    i = pl.program_id(0)
    ci = (i * tm) // rows_per_cond
    m = mod_ref[pl.ds(ci, 1), :]
    sh2 = m[:, 3 * D_MODEL:4 * D_MODEL]
    sc2 = m[:, 4 * D_MODEL:5 * D_MODEL]
    g2 = m[:, 5 * D_MODEL:6 * D_MODEL]
    x = x_ref[...]
    hb = (_rms(x, nw_ref[2:3, :]) * (1.0 + sc2) + sh2).astype(BF16)
    f = None
    lo = 0
    for width in FF_CHUNKS:
        gate = jnp.dot(hb, wi_ref[:, lo:lo + width], preferred_element_type=F32)
        up = jnp.dot(hb, wi_ref[:, D_FF + lo:D_FF + lo + width], preferred_element_type=F32)
        part = _mm(_silu(gate) * up, wo_ref[lo:lo + width, :])
        lo += width
        f = part if f is None else f + part
    o_ref[...] = x + g2 * _rms(f, nw_ref[3:4, :])


def _ffn(x, mod, norm_w, w_in, w_out, rows_per_cond):
    t = x.shape[0]
    tm = 512
    kern = functools.partial(_ffn_kernel, tm=tm, rows_per_cond=rows_per_cond)
    row = lambda i: (i, 0)
    const = lambda i: (0, 0)
    resident = pl.Buffered(1)
    return pl.pallas_call(
        kern,
        grid=(t // tm,),
        in_specs=[pl.BlockSpec((tm, D_MODEL), row),
                  pl.BlockSpec(mod.shape, const),
                  pl.BlockSpec(norm_w.shape, const),
                  pl.BlockSpec((D_MODEL, 2 * D_FF), const, pipeline_mode=resident),
                  pl.BlockSpec((D_FF, D_MODEL), const, pipeline_mode=resident)],
        out_specs=pl.BlockSpec((tm, D_MODEL), row),
        out_shape=jax.ShapeDtypeStruct((t, D_MODEL), F32),
        compiler_params=_cparams(("arbitrary",)),
        name="swiglu",
    )(x, mod, norm_w, w_in, w_out)


def _rope_tables(seq_len):
    rows = seq_len // GRID_W
    row_idx = jnp.repeat(jnp.arange(rows, dtype=F32), GRID_W)
    col_idx = (jnp.arange(seq_len) % GRID_W).astype(F32)
    n_freq = DK // 4
    freqs = ROPE_BASE ** (-jnp.arange(n_freq, dtype=F32) / n_freq)
    ang = jnp.concatenate([row_idx[:, None] * freqs, col_idx[:, None] * freqs], axis=-1)
    cos = jnp.repeat(jnp.cos(ang), 2, axis=-1)
    sin = jnp.repeat(jnp.sin(ang), 2, axis=-1)
    sign = jnp.tile(jnp.array([-1.0, 1.0], F32), DK // 2)
    return cos, sin * sign


def _one_path(x, mod, seq_len, n_seq, wts, rope_tabs, s_ret0, s_dn0, emit_state):
    rows_per_cond = seq_len if mod.shape[0] > 1 else x.shape[0]
    proj, gates = _inproj(x, mod, wts["norm_w"], wts["w_in_t"], wts["conv_w"], rows_per_cond,
                          seq_len, rope_tabs)
    ret = _retention(proj, wts["logit_rep"], wts["ret_gn_w"], n_seq, seq_len,
                     s0=s_ret0, emit_state=emit_state)
    gdn = _gdn(proj, gates, wts["a_log"], wts["dt_bias"], n_seq, seq_len,
               s0=s_dn0, emit_state=emit_state)
    x1 = _merge(x, ret[0], gdn[0], gdn[1], proj, mod, wts["norm_w"], wts["dn_norm_w"],
                wts["w_ret_o"], wts["w_dn_o"], wts["w_out"], rows_per_cond)
    y = _ffn(x1, mod, wts["norm_w"], wts["w_ffn_in"], wts["w_ffn_out"], rows_per_cond)
    if emit_state:
        return y, ret[1], gdn[2]
    return y, None, None


def kernel(x_prompt, x_sample, c, state_ret, state_dn, c_ctx, w_mod, b_mod, norm_w, w_in, conv_w,
           ret_decay_logit, ret_gn_w, dn_a_log, dn_dt_bias, dn_norm_w, w_ret_o, w_dn_o, w_out,
           w_ffn_in, w_ffn_out):
    n_ctx, l_ctx, _ = x_prompt.shape
    n_lat, l_lat, _ = x_sample.shape
    assert w_mod.shape[0] == 1, "single-layer kernel"

    cond8 = jnp.zeros((8, D_MODEL), F32).at[0].set(c_ctx).at[1:1 + n_lat].set(c)
    mod = _modulation(cond8, w_mod[0], b_mod)
    mod_ctx, mod_lat = mod[0:1], mod[1:1 + n_lat]

    wts = {
        "norm_w": norm_w[0],
        "w_in_t": w_in[0].T.astype(BF16),
        "logit_rep": jnp.broadcast_to(ret_decay_logit[0].T[:, :, None], (HEADS, 2, DK)),
        "ret_gn_w": ret_gn_w,
        "conv_w": conv_w[0],
        "a_log": dn_a_log.reshape(N_GATE, 1),
        "dt_bias": dn_dt_bias.reshape(N_GATE, 1),
        "dn_norm_w": dn_norm_w,
        "w_ret_o": w_ret_o[0].astype(BF16),
        "w_dn_o": w_dn_o[0].astype(BF16),
        "w_out": w_out[0].astype(BF16),
        "w_ffn_in": w_ffn_in[0].astype(BF16),
        "w_ffn_out": w_ffn_out[0].astype(BF16),
    }

    y_p, s_ret, s_dn = _one_path(x_prompt.reshape(n_ctx * l_ctx, D_MODEL), mod_ctx, l_ctx, n_ctx,
                                 wts, None, None, None, True)
    y_s, _, _ = _one_path(x_sample.reshape(n_lat * l_lat, D_MODEL), mod_lat, l_lat, n_lat,
                          wts, _rope_tables(l_lat), state_ret, state_dn, False)
    return (y_p.reshape(x_prompt.shape), y_s.reshape(x_sample.shape), s_ret, s_dn)
```

```python
import functools

import jax
import jax.numpy as jnp
from jax import lax
from jax.experimental import pallas as pl
from jax.experimental.pallas import tpu as pltpu

F32 = jnp.float32
BF16 = jnp.bfloat16

D_MODEL = 1024
HEADS = 4
DK = 128
DV = 256
QK_W = HEADS * DK
V_W = HEADS * DV
GDN_CHUNK = 64
TRI_BASE = 8
BLK = 256
SUB = 8
LANES = 128
HALO = 16
CONV_WIDTH = 3
GRID_W = 64
ROPE_BASE = 10000.0
EPS = 1e-6
D_FF = 2816
MXU_DIM = 256
FF_CHUNKS = (6 * MXU_DIM, 5 * MXU_DIM)
assert sum(FF_CHUNKS) == D_FF
N_GATE = 2 * HEADS

COL_RQ, COL_RK, COL_RV, COL_RG = 0, 512, 1024, 2048
COL_DQK, COL_DV, COL_DZ = 3072, 4096, 5120
COL_GATE = 6144
COL_GR, COL_GD = 6144, 7168
PROJ_W = 8192
PROJ_TN = 4096

VMEM_LIMIT = 56 * 1024 * 1024


def _cparams(sem):
    return pltpu.CompilerParams(dimension_semantics=sem, vmem_limit_bytes=VMEM_LIMIT)


def _mm(a, b):
    return jnp.dot(a.astype(BF16), b.astype(BF16), preferred_element_type=F32)


def _mm_nt(a, b):
    return lax.dot_general(a.astype(BF16), b.astype(BF16), (((1,), (1,)), ((), ())),
                           preferred_element_type=F32)


def _mm_tn(a, b):
    return lax.dot_general(a.astype(BF16), b.astype(BF16), (((0,), (0,)), ((), ())),
                           preferred_element_type=F32)


def _sigmoid(x):
    return 1.0 / (1.0 + jnp.exp(-x))


def _silu(x):
    return x * _sigmoid(x)


def _softplus(x):
    return jnp.maximum(x, 0.0) + jnp.log(1.0 + jnp.exp(-jnp.abs(x)))


def _log_sigmoid(x):
    return -_softplus(-x)


def _rms(x, w):
    return x * lax.rsqrt(jnp.mean(x * x, axis=-1, keepdims=True) + EPS) * w


def _mod_kernel(c_ref, w_ref, b_ref, o_ref):
    o_ref[...] = _mm(_silu(c_ref[...]), w_ref[...]) + b_ref[...]


def _modulation(cond8, w_mod, b_mod):
    n = w_mod.shape[1]
    tn = 1536
    return pl.pallas_call(
        _mod_kernel,
        grid=(n // tn,),
        in_specs=[pl.BlockSpec((8, D_MODEL), lambda j: (0, 0)),
                  pl.BlockSpec((D_MODEL, tn), lambda j: (0, j)),
                  pl.BlockSpec((1, tn), lambda j: (0, j))],
        out_specs=pl.BlockSpec((8, tn), lambda j: (0, j)),
        out_shape=jax.ShapeDtypeStruct((8, n), F32),
        compiler_params=_cparams(("arbitrary",)),
        name="modulation",
    )(cond8, w_mod, b_mod)


CHUNK_W = 256
SUBROWS = 128
PLAIN, RET_Q, RET_K, DN_Q, DN_K, DN_V = range(6)
CHUNK_KINDS = ((RET_Q, RET_Q, RET_K, RET_K) + (PLAIN,) * 4
               + (PLAIN,) * 4 + (DN_Q, DN_Q, DN_K, DN_K)
               + (DN_V,) * 4 + (PLAIN,) * 4
               + (PLAIN,) * 8)
assert len(CHUNK_KINDS) * CHUNK_W == PROJ_W and PROJ_TN % CHUNK_W == 0


def _pair_swap(x):
    lane = lax.broadcasted_iota(jnp.int32, x.shape, 1)
    nxt = pltpu.roll(x, x.shape[1] - 1, 1)
    prv = pltpu.roll(x, 1, 1)
    return jnp.where((lane & 1) == 0, nxt, prv)


def _inproj_kernel(*refs, tm, rows_per_cond, seq_len, rope):
    it = iter(refs)
    x_ref, xp_ref, xn_ref, mod_ref, nw_ref, w_ref, cw_ref = (next(it) for _ in range(7))
    cos_ref = next(it) if rope else None
    sin_ref = next(it) if rope else None
    o_ref, g_ref, h_sc, p_sc = next(it), next(it), next(it), next(it)

    i = pl.program_id(0)
    j = pl.program_id(1)
    t0 = i * tm
    tile_rows = slice(HALO, HALO + tm)

    @pl.when(j == 0)
    def _():
        ci = t0 // rows_per_cond
        m = mod_ref[pl.ds(ci, 1), :]
        sh1 = m[:, 0:D_MODEL]
        sc1 = m[:, D_MODEL:2 * D_MODEL]

        def pre(x):
            return (_rms(x, nw_ref[0:1, :]) * (1.0 + sc1) + sh1).astype(BF16)

        hb = pre(x_ref[...])
        h_sc[tile_rows, :] = hb
        h_sc[0:HALO, :] = pre(xp_ref[...])
        h_sc[HALO + tm:2 * HALO + tm, :] = pre(xn_ref[...])
        g_ref[...] = _mm_nt(w_ref[COL_GATE:COL_GATE + 2 * N_GATE, :], hb)

    seq_start = (t0 & (seq_len - 1)) == 0
    seq_end = ((t0 + tm) & (seq_len - 1)) == 0

    def conv_silu(win, r0, cw_lo):
        width = win.shape[1]
        inner = slice(SUB, SUB + SUBROWS)
        cur = win[inner]
        prev = pltpu.roll(win, 1, 0)[inner]
        nxt = pltpu.roll(win, win.shape[0] - 1, 0)[inner]
        row = lax.broadcasted_iota(jnp.int32, cur.shape, 0)
        if seq_len >= tm:
            if r0 == 0:
                prev = jnp.where((row == 0) & seq_start, 0.0, prev)
            if r0 + SUBROWS == tm:
                nxt = jnp.where((row == SUBROWS - 1) & seq_end, 0.0, nxt)
        else:
            if r0 % seq_len == 0:
                prev = jnp.where(row == 0, 0.0, prev)
            if (r0 + SUBROWS) % seq_len == 0:
                nxt = jnp.where(row == SUBROWS - 1, 0.0, nxt)
        cw = cw_ref[:, cw_lo:cw_lo + width]
        return _silu(prev * cw[0:1, :] + cur * cw[1:2, :] + nxt * cw[2:3, :])

    def l2n_heads(y, scale):
        outs = []
        for lo in range(0, y.shape[1], DK):
            yh = y[:, lo:lo + DK]
            outs.append(yh * (lax.rsqrt(jnp.sum(yh * yh, axis=-1, keepdims=True) + EPS) * scale))
        return jnp.concatenate(outs, axis=1)

    def rotary(p, r0):
        reps = p.shape[1] // DK
        cos = jnp.concatenate([cos_ref[r0:r0 + SUBROWS, :]] * reps, axis=1)
        sin = jnp.concatenate([sin_ref[r0:r0 + SUBROWS, :]] * reps, axis=1)
        return p * cos + _pair_swap(p) * sin

    def tile(col0, kinds):
        n_staged = 0
        for c, kind in enumerate(kinds):
            lo = c * CHUNK_W
            col = col0 + lo
            w_row = col if col < COL_GATE else col + 2 * N_GATE
            w = w_ref[w_row:w_row + CHUNK_W, :]
            conv = kind in (DN_Q, DN_K, DN_V)
            if not conv and not (rope and kind in (RET_Q, RET_K)):
                y = _mm_nt(h_sc[tile_rows, :], w)
                if kind == RET_Q:
                    y = y * (DK ** -0.5)
                o_ref[:, lo:lo + CHUNK_W] = y.astype(o_ref.dtype)
                continue
            buf = p_sc.at[n_staged % 2]
            n_staged += 1
            if conv:
                buf[...] = _mm_nt(h_sc[...], w)
                cw_lo = col - COL_DQK
            else:
                buf[tile_rows, :] = _mm_nt(h_sc[tile_rows, :], w)
            for r0 in range(0, tm, SUBROWS):
                if conv:
                    y = conv_silu(buf[HALO + r0 - SUB:HALO + r0 + SUBROWS + SUB, :], r0, cw_lo)
                    if kind == DN_Q:
                        y = l2n_heads(y, DK ** -0.5)
                    elif kind == DN_K:
                        y = l2n_heads(y, 1.0)
                else:
                    y = buf[HALO + r0:HALO + r0 + SUBROWS, :]
                    if kind == RET_Q:
                        y = y * (DK ** -0.5)
                    y = rotary(y, r0)
                o_ref[r0:r0 + SUBROWS, lo:lo + CHUNK_W] = y.astype(o_ref.dtype)

    per_tile = PROJ_TN // CHUNK_W
    for jt in range(PROJ_W // PROJ_TN):
        kinds = CHUNK_KINDS[jt * per_tile:(jt + 1) * per_tile]

        @pl.when(j == jt)
        def _(jt=jt, kinds=kinds):
            tile(jt * PROJ_TN, kinds)


def _inproj(x, mod, norm_w, w_t, conv_w, rows_per_cond, seq_len, rope_tabs):
    t = x.shape[0]
    tm, tn = 1024, PROJ_TN
    rope = rope_tabs is not None
    assert seq_len & (seq_len - 1) == 0 and (tm % seq_len == 0 or seq_len % tm == 0)
    kern = functools.partial(_inproj_kernel, tm=tm, rows_per_cond=rows_per_cond,
                             seq_len=seq_len, rope=rope)
    halo_per_tile = tm // HALO
    n_halo = t // HALO
    in_specs = [pl.BlockSpec((tm, D_MODEL), lambda i, j: (i, 0)),
                pl.BlockSpec((HALO, D_MODEL), lambda i, j: (jnp.maximum(i * halo_per_tile - 1, 0), 0)),
                pl.BlockSpec((HALO, D_MODEL),
                             lambda i, j: (jnp.minimum((i + 1) * halo_per_tile, n_halo - 1), 0)),
                pl.BlockSpec(mod.shape, lambda i, j: (0, 0)),
                pl.BlockSpec(norm_w.shape, lambda i, j: (0, 0)),
                pl.BlockSpec(w_t.shape, lambda i, j: (0, 0), pipeline_mode=pl.Buffered(1)),
                pl.BlockSpec(conv_w.shape, lambda i, j: (0, 0))]
    args = [x, x, x, mod, norm_w, w_t, conv_w]
    if rope:
        tiles_per_seq = max(seq_len // tm, 1)
        in_specs += [pl.BlockSpec((tm, DK), lambda i, j: (i % tiles_per_seq, 0))] * 2
        args += list(rope_tabs)
    return pl.pallas_call(
        kern,
        grid=(t // tm, PROJ_W // tn),
        in_specs=in_specs,
        out_specs=[pl.BlockSpec((tm, tn), lambda i, j: (i, j)),
                   pl.BlockSpec((2 * N_GATE, tm), lambda i, j: (0, i))],
        out_shape=[jax.ShapeDtypeStruct((t, PROJ_W), BF16),
                   jax.ShapeDtypeStruct((2 * N_GATE, t), F32)],
        scratch_shapes=[pltpu.VMEM((tm + 2 * HALO, D_MODEL), BF16),
                        pltpu.VMEM((2, tm + 2 * HALO, CHUNK_W), F32)],
        compiler_params=_cparams(("arbitrary", "arbitrary")),
        name="inproj",
    )(*args)


RET_GROUP = 8


def _ret_kernel(*refs, seq_len, has_s0, emit_state):
    it = iter(refs)
    q_ref, k_ref, v_ref = next(it), next(it), next(it)
    logit_ref, gnw_ref = next(it), next(it)
    s0_ref = next(it) if has_s0 else None
    o_ref = next(it)
    st_ref = next(it) if emit_state else None
    st_sc = next(it)

    nb = seq_len // BLK
    cross = has_s0 or nb > 1

    def rows(c):
        return slice(c * BLK, (c + 1) * BLK)

    def qk_cols(h):
        return slice(h * DK, (h + 1) * DK)

    def v_cols(h):
        return slice(h * DV, (h + 1) * DV)

    pos =lax.broadcasted_iota(jnp.int32, (BLK, DK), 0).astype(F32)

    def log_gammas(h):
        lg = _log_sigmoid(logit_ref[h])
        return lg[0:1, :], lg[1:2, :]

    for h in range(HEADS):
        lgf, lgb = log_gammas(h)
        kdec_f = jnp.exp(lgf * (BLK - 1.0 - pos))
        kdec_b = jnp.exp(lgb * pos)
        cdec_f = jnp.exp(lgf[:, 0:1] * float(BLK))
        cdec_b = jnp.exp(lgb[:, 0:1] * float(BLK))
        af = {c: _mm_tn(k_ref[rows(c), qk_cols(h)].astype(F32) * kdec_f, v_ref[rows(c), v_cols(h)])
              for c in range(nb) if c < nb - 1 or emit_state}
        ab = {c: _mm_tn(k_ref[rows(c), qk_cols(h)].astype(F32) * kdec_b, v_ref[rows(c), v_cols(h)])
              for c in range(nb) if c > 0 or emit_state}
        s = s0_ref[0, h] if has_s0 else jnp.zeros((DK, DV), F32)
        for c in range(nb):
            if cross:
                st_sc[h, c, 0:DK, :] = s.astype(BF16)
            if c in af:
                s = s * cdec_f + af[c]
        if emit_state:
            st_ref[0, h] = s
        s = s0_ref[1, h] if has_s0 else jnp.zeros((DK, DV), F32)
        for c in range(nb - 1, -1, -1):
            if cross:
                st_sc[h, c, DK:2 * DK, :] = s.astype(BF16)
            if c in ab:
                s = s * cdec_b + ab[c]
        if emit_state:
            st_ref[1, h] = s

    ii = lax.broadcasted_iota(jnp.int32, (BLK, BLK), 0)
    jj = lax.broadcasted_iota(jnp.int32, (BLK, BLK), 1)
    diff = (ii - jj).astype(F32)
    pos2 = lax.broadcasted_iota(jnp.int32, (BLK, 2 * DK), 0).astype(F32)
    first_half = lax.broadcasted_iota(jnp.int32, (BLK, 2 * DK), 1) < DK

    def head_consts(h):
        lgf, lgb = log_gammas(h)
        lgf1, lgb1 = lgf[:, 0:1], lgb[:, 0:1]
        dmat = (jnp.where(diff >= 0, jnp.exp(lgf1 * jnp.maximum(diff, 0.0)), 0.0)
                + jnp.where(diff <= 0, jnp.exp(lgb1 * jnp.maximum(-diff, 0.0)), 0.0))
        qdec = jnp.where(first_half, jnp.exp(lgf1 * (pos2 + 1.0)), jnp.exp(lgb1 * (BLK - pos2)))
        return dmat, qdec

    def unit(h, c, consts):
        dmat, qdec = consts
        qc = q_ref[rows(c), qk_cols(h)]
        s = _mm_nt(qc, k_ref[rows(c), qk_cols(h)])
        yield
        o = _mm(s * dmat, v_ref[rows(c), v_cols(h)])
        if cross:
            o_cross = _mm(jnp.concatenate([qc, qc], axis=1).astype(F32) * qdec, st_sc[h, c])
        yield
        if cross:
            o = o + o_cross
        mu = jnp.mean(o, axis=-1, keepdims=True)
        oc = o - mu
        var = jnp.mean(oc * oc, axis=-1, keepdims=True)
        o_ref[rows(c), v_cols(h)] = (oc * lax.rsqrt(var + EPS) * gnw_ref[:, v_cols(h)]).astype(o_ref.dtype)
        yield

    units = [(h, c) for h in range(HEADS) for c in range(nb)]
    for g0 in range(0, len(units), RET_GROUP):
        consts = {}
        gens = []
        for h, c in units[g0:g0 + RET_GROUP]:
            if h not in consts:
                consts[h] = head_consts(h)
            gens.append(unit(h, c, consts[h]))
        for _ in zip(*gens):
            pass


def _retention(proj, logit_rep, gn_w, n_seq, seq_len, s0=None, emit_state=False):
    t = proj.shape[0]
    has_s0 = s0 is not None
    kern = functools.partial(_ret_kernel, seq_len=seq_len, has_s0=has_s0, emit_state=emit_state)
    in_specs = [pl.BlockSpec((seq_len, QK_W), lambda b: (b, COL_RQ // QK_W)),
                pl.BlockSpec((seq_len, QK_W), lambda b: (b, COL_RK // QK_W)),
                pl.BlockSpec((seq_len, V_W), lambda b: (b, COL_RV // V_W))]
    args = [proj, proj, proj]
    in_specs += [pl.BlockSpec((HEADS, 2, DK), lambda b: (0, 0, 0)),
                 pl.BlockSpec((1, V_W), lambda b: (0, 0))]
    args += [logit_rep, gn_w]
    state_spec = pl.BlockSpec((None, None, 2, HEADS, DK, DV), lambda b: (b, 0, 0, 0, 0, 0))
    if has_s0:
        in_specs.append(state_spec)
        args.append(s0)
    out_specs = [pl.BlockSpec((seq_len, V_W), lambda b: (b, 0))]
    out_shape = [jax.ShapeDtypeStruct((t, V_W), BF16)]
    if emit_state:
        out_specs.append(state_spec)
        out_shape.append(jax.ShapeDtypeStruct((n_seq, 1, 2, HEADS, DK, DV), F32))
    nb = seq_len // BLK
    return pl.pallas_call(
        kern,
        grid=(n_seq,),
        in_specs=in_specs,
        out_specs=out_specs,
        out_shape=out_shape,
        scratch_shapes=[pltpu.VMEM((HEADS, nb, 2 * DK, DV), BF16)],
        compiler_params=_cparams(("arbitrary",)),
        name="retention",
    )(*args)


N_MASK = 8
(MASK_EQ_F, MASK_EQ_B, MASK_OFFDIAG, MASK_EYE, MASK_BASE, MASK_PAIR0) = 0, 1, 2, 3, 4, 5


def _build_masks(mask_sc, eye_sc):
    ii = lax.broadcasted_iota(jnp.int32, (BLK, BLK), 0)
    jj = lax.broadcasted_iota(jnp.int32, (BLK, BLK), 1)

    def blk(v, size):
        return lax.shift_right_logical(v, size.bit_length() - 1)

    same = blk(ii, GDN_CHUNK) == blk(jj, GDN_CHUNK)
    mask_sc[MASK_EQ_F] = (same & (jj <= ii)).astype(F32)
    mask_sc[MASK_EQ_B] = (same & (jj >= ii)).astype(F32)
    mask_sc[MASK_OFFDIAG] = (ii != jj).astype(F32)
    mask_sc[MASK_EYE] = (ii == jj).astype(F32)
    mask_sc[MASK_BASE] = (blk(ii, TRI_BASE) == blk(jj, TRI_BASE)).astype(F32)
    size, idx = TRI_BASE, MASK_PAIR0
    while size < GDN_CHUNK:
        pair = (blk(ii, 2 * size) == blk(jj, 2 * size)) & (blk(ii, size) != blk(jj, size))
        mask_sc[idx] = pair.astype(F32)
        size, idx = 2 * size, idx + 1
    eye_sc[...] = (ii == jj).astype(BF16)


def _seg_cumsums(x):
    n = x.shape[1]
    pos = lax.broadcasted_iota(jnp.int32, x.shape, 1) & (GDN_CHUNK - 1)
    up, down = x, x
    s = 1
    while s < GDN_CHUNK:
        up = up + jnp.where(pos >= s, pltpu.roll(up, s, 1), 0.0)
        down = down + jnp.where(pos < GDN_CHUNK - s, pltpu.roll(down, n - s, 1), 0.0)
        s *= 2
    return up, down


def _rows_to_cols(rows, eye_sc):
    p1 = rows.astype(BF16).astype(F32)
    r1 = rows - p1
    p2 = r1.astype(BF16).astype(F32)
    p3 = r1 - p2
    pieces = jnp.concatenate([p1, p2, p3], axis=0).astype(BF16)
    c3 = lax.dot_general(eye_sc[...], pieces, (((1,), (1,)), ((), ())), preferred_element_type=F32)
    return c3[:, 0:SUB] + c3[:, SUB:2 * SUB] + c3[:, 2 * SUB:3 * SUB]


def _gdn_kernel(*refs, nb, has_s0, emit_state):
    it = iter(refs)
    xqk_refs = (next(it), next(it))
    xv_refs = (next(it), next(it))
    g_refs = (next(it), next(it))
    alog_ref, dtb_ref = next(it), next(it)
    s0_ref = next(it) if has_s0 else None
    o_refs = (next(it), next(it))
    st_ref = next(it) if emit_state else None
    st_sc, mask_sc, eye_sc = next(it), next(it), next(it)

    b = pl.program_id(0)
    step = pl.program_id(1)

    @pl.when((b == 0) & (step == 0))
    def _():
        _build_masks(mask_sc, eye_sc)

    @pl.when(step == 0)
    def _():
        for d in range(2):
            for h in range(HEADS):
                st_sc[d, h] = s0_ref[d, h] if has_s0 else jnp.zeros((DK, DV), F32)

    nck = BLK // GDN_CHUNK

    def gates():
        row8 = lax.broadcasted_iota(jnp.int32, (N_GATE, BLK), 0)
        fwd_rows = row8 < HEADS
        logit_b = jnp.where(fwd_rows, g_refs[0][0:N_GATE, :], g_refs[1][0:N_GATE, :])
        logit_a = jnp.where(fwd_rows, g_refs[0][N_GATE:2 * N_GATE, :], g_refs[1][N_GATE:2 * N_GATE, :])
        beta = _sigmoid(logit_b)
        la = -jnp.exp(alog_ref[...]) * _softplus(logit_a + dtb_ref[...])
        up, down = _seg_cumsums(la)
        g8 = jnp.where(fwd_rows, up, down)
        r8 = jnp.where(fwd_rows, down, up) - la
        etot8 = jnp.exp(up + down - la)
        return dict(beta=beta, g8=g8, eg8=jnp.exp(g8), erb8=jnp.exp(r8) * beta, etot8=etot8,
                    gcols=_rows_to_cols(g8, eye_sc))

    gate = {}

    def chain(d, h):
        c = d * HEADS + h
        q16 = xqk_refs[d][:, h * DK:(h + 1) * DK]
        k16 = xqk_refs[d][:, QK_W + h * DK:QK_W + (h + 1) * DK]
        v16 = xv_refs[d][:, h * DV:(h + 1) * DV]
        q = q16.astype(F32)
        k = k16.astype(F32)
        kk = _mm_nt(k16, k16)
        qk = _mm_nt(q16, k16)
        yield
        g_col = gate["gcols"][:, c:c + 1]
        g_row = gate["g8"][c:c + 1, :]
        m_eq = mask_sc[MASK_EQ_F if d == 0 else MASK_EQ_B]
        decay_beta = (jnp.exp(jnp.minimum(g_col - g_row, 0.0)) * m_eq) * gate["beta"][c:c + 1, :]
        lp = kk * decay_beta * mask_sc[MASK_OFFDIAG]
        sc16 = (qk * decay_beta).astype(BF16)
        qg16 = (q * jnp.exp(g_col)).astype(BF16)
        kt16 = (k.T * gate["erb8"][c:c + 1, :]).astype(BF16)

        x = -(lp * mask_sc[MASK_BASE])
        t = mask_sc[MASK_EYE] + x
        for _ in range(TRI_BASE.bit_length() - 2):
            x = _mm(x, x)
            yield
            t = t + _mm(t, x)
            yield
        size, idx = TRI_BASE, MASK_PAIR0
        while size < GDN_CHUNK:
            groups = [slice(r * size, (r + 1) * size) for r in range(BLK // size)]
            late = [(r % 2 == 1) == (d == 0) for r in range(BLK // size)]
            e_late = jnp.concatenate([lp[g, :] * mask_sc[idx, g, :] for g, on in zip(groups, late) if on],
                                     axis=0)
            et = _mm(e_late, t)
            yield
            t_late = jnp.concatenate([t[g, :] for g, on in zip(groups, late) if on], axis=0)
            et_rows = iter(et[j * size:(j + 1) * size, :] for j in range(BLK // size // 2))
            et_full = jnp.concatenate([next(et_rows) if on else jnp.zeros((size, BLK), F32)
                                       for on in late], axis=0)
            t_late = t_late - _mm(t_late, et_full)
            yield
            new_rows = iter(t_late[j * size:(j + 1) * size, :] for j in range(BLK // size // 2))
            t = jnp.concatenate([next(new_rows) if on else t[g, :] for g, on in zip(groups, late)],
                                axis=0)
            size, idx = 2 * size, idx + 1
        u = _mm(t, v16)
        w16 = _mm(t * gate["eg8"][c:c + 1, :], k16).astype(BF16)
        yield

        order = range(nck) if d == 0 else range(nck - 1, -1, -1)
        for cc in order:
            rs = slice(cc * GDN_CHUNK, (cc + 1) * GDN_CHUNK)
            s_prev = st_sc[d, h]
            wq = jnp.concatenate([w16[rs, :], qg16[rs, :]], axis=0)
            ws_qs = _mm(wq, s_prev)
            yield
            v_new = (u[rs, :] - ws_qs[0:GDN_CHUNK]).astype(BF16)
            sk = _mm(jnp.concatenate([sc16[rs, rs], kt16[:, rs]], axis=0), v_new)
            o = ws_qs[GDN_CHUNK:] + sk[0:GDN_CHUNK]
            decay_all = gate["etot8"][c:c + 1, cc * GDN_CHUNK:cc * GDN_CHUNK + 1]
            st_sc[d, h] = s_prev * decay_all + sk[GDN_CHUNK:]
            o_refs[d][rs, h * DV:(h + 1) * DV] = o.astype(o_refs[d].dtype)
            yield

    chains = [chain(d, h) for d in range(2) for h in range(HEADS)]
    for ch in chains:
        next(ch)
    gate.update(gates())
    for _ in zip(*chains):
        pass

    if emit_state:
        @pl.when(step == nb - 1)
        def _():
            for d in range(2):
                for h in range(HEADS):
                    st_ref[d, h] = st_sc[d, h]


def _gdn(proj, gates, a_log, dt_bias, n_seq, seq_len, s0=None, emit_state=False):
    t = proj.shape[0]
    nb = seq_len // BLK
    has_s0 = s0 is not None
    kern = functools.partial(_gdn_kernel, nb=nb, has_s0=has_s0, emit_state=emit_state)

    def fblk(b, s):
        return b * nb + s

    def bblk(b, s):
        return b * nb + nb - 1 - s

    def spec_x(blk, col):
        return pl.BlockSpec((BLK, V_W), lambda b, s: (blk(b, s), col // V_W))

    def spec_g(blk):
        return pl.BlockSpec((2 * N_GATE, BLK), lambda b, s: (0, blk(b, s)))

    const2 = lambda b, s: (0, 0)
    in_specs = [spec_x(fblk, COL_DQK), spec_x(bblk, COL_DQK), spec_x(fblk, COL_DV), spec_x(bblk, COL_DV),
                spec_g(fblk), spec_g(bblk),
                pl.BlockSpec((N_GATE, 1), const2),
                pl.BlockSpec((N_GATE, 1), const2)]
    args = [proj, proj, proj, proj, gates, gates, a_log, dt_bias]
    state_spec = pl.BlockSpec((None, None, 2, HEADS, DK, DV), lambda b, s: (b, 0, 0, 0, 0, 0))
    if has_s0:
        in_specs.append(state_spec)
        args.append(s0)
    out_specs = [pl.BlockSpec((BLK, V_W), lambda b, s: (fblk(b, s), 0)),
                 pl.BlockSpec((BLK, V_W), lambda b, s: (bblk(b, s), 0))]
    out_shape = [jax.ShapeDtypeStruct((t, V_W), BF16), jax.ShapeDtypeStruct((t, V_W), BF16)]
    if emit_state:
        out_specs.append(state_spec)
        out_shape.append(jax.ShapeDtypeStruct((n_seq, 1, 2, HEADS, DK, DV), F32))
    scratch = [pltpu.VMEM((2, HEADS, DK, DV), F32),
               pltpu.VMEM((N_MASK, BLK, BLK), F32),
               pltpu.VMEM((BLK, BLK), BF16)]
    return pl.pallas_call(
        kern,
        grid=(n_seq, nb),
        in_specs=in_specs,
        out_specs=out_specs,
        out_shape=out_shape,
        scratch_shapes=scratch,
        compiler_params=_cparams(("arbitrary", "arbitrary")),
        name="gated_delta",
    )(*args)


def _merge_kernel(x_ref, or_ref, of_ref, ob_ref, rg_ref, dz_ref, gr_ref, gd_ref, mod_ref, nw_ref,
                  dnw_ref, wr_ref, wd_ref, wo_ref, o_ref, *, tm, rows_per_cond):
    i = pl.program_id(0)
    ci = (i * tm) // rows_per_cond
    m = mod_ref[pl.ds(ci, 1), :]
    g1 = m[:, 2 * D_MODEL:3 * D_MODEL]
    y_r = _mm(_silu(rg_ref[...].astype(F32)) * or_ref[...].astype(F32), wr_ref[...])
    dnw = dnw_ref[...]
    heads = []
    for h in range(HEADS):
        cols = slice(h * DV, (h + 1) * DV)
        od = _rms(of_ref[:, cols].astype(F32) + ob_ref[:, cols].astype(F32), dnw)
        heads.append((od * _silu(dz_ref[:, cols].astype(F32))).astype(BF16))
    y_d = jnp.dot(jnp.concatenate(heads, axis=1), wd_ref[...], preferred_element_type=F32)
    merged = _sigmoid(gr_ref[...].astype(F32)) * y_r + _sigmoid(gd_ref[...].astype(F32)) * y_d
    mo = _mm(merged, wo_ref[...])
    o_ref[...] = x_ref[...] + g1 * _rms(mo, nw_ref[1:2, :])


def _merge(x, o_r, o_f, o_b, proj, mod, norm_w, dn_norm_w, w_ret_o, w_dn_o, w_out, rows_per_cond):
    t = x.shape[0]
    tm = 512
    kern = functools.partial(_merge_kernel, tm=tm, rows_per_cond=rows_per_cond)
    row = lambda i: (i, 0)
    const = lambda i: (0, 0)
    wspec = pl.BlockSpec((D_MODEL, D_MODEL), const)
    return pl.pallas_call(
        kern,
        grid=(t // tm,),
        in_specs=[pl.BlockSpec((tm, D_MODEL), row),
                  pl.BlockSpec((tm, V_W), row),
                  pl.BlockSpec((tm, V_W), row),
                  pl.BlockSpec((tm, V_W), row),
                  pl.BlockSpec((tm, V_W), lambda i: (i, COL_RG // V_W)),
                  pl.BlockSpec((tm, V_W), lambda i: (i, COL_DZ // V_W)),
                  pl.BlockSpec((tm, D_MODEL), lambda i: (i, COL_GR // D_MODEL)),
                  pl.BlockSpec((tm, D_MODEL), lambda i: (i, COL_GD // D_MODEL)),
                  pl.BlockSpec(mod.shape, const),
                  pl.BlockSpec(norm_w.shape, const),
                  pl.BlockSpec(dn_norm_w.shape, const),
                  wspec, wspec, wspec],
        out_specs=pl.BlockSpec((tm, D_MODEL), row),
        out_shape=jax.ShapeDtypeStruct((t, D_MODEL), F32),
        compiler_params=_cparams(("arbitrary",)),
        name="merge_out",
    )(x, o_r, o_f, o_b, proj, proj, proj, proj, mod, norm_w, dn_norm_w, w_ret_o, w_dn_o, w_out)


def _ffn_kernel(x_ref, mod_ref, nw_ref, wi_ref, wo_ref, o_ref, *, tm, rows_per_cond):
    i = pl.program_id(0)
    ci = (i * tm) // rows_per_cond
    m = mod_ref[pl.ds(ci, 1), :]
    sh2 = m[:, 3 * D_MODEL:4 * D_MODEL]
    sc2 = m[:, 4 * D_MODEL:5 * D_MODEL]
    g2 = m[:, 5 * D_MODEL:6 * D_MODEL]
    def part_rows(rs):
        x = x_ref[rs, :]
        hb = (_rms(x, nw_ref[2:3, :]) * (1.0 + sc2) + sh2).astype(BF16)
        f = None
        lo = 0
        for width in FF_CHUNKS:
            gate = jnp.dot(hb, wi_ref[:, lo:lo + width], preferred_element_type=F32)
            up = jnp.dot(hb, wi_ref[:, D_FF + lo:D_FF + lo + width], preferred_element_type=F32)
            yield
            part = _mm(_silu(gate) * up, wo_ref[lo:lo + width, :])
            yield
            lo += width
            f = part if f is None else f + part
        o_ref[rs, :] = x + g2 * _rms(f, nw_ref[3:4, :])
        yield

    half = tm // 2
    parts = [part_rows(slice(r0, r0 + half)) for r0 in range(0, tm, half)]
    for _ in zip(*parts):
        pass


def _ffn(x, mod, norm_w, w_in, w_out, rows_per_cond):
    t = x.shape[0]
    tm = 512
    kern = functools.partial(_ffn_kernel, tm=tm, rows_per_cond=rows_per_cond)
    row = lambda i: (i, 0)
    const = lambda i: (0, 0)
    resident = pl.Buffered(1)
    return pl.pallas_call(
        kern,
        grid=(t // tm,),
        in_specs=[pl.BlockSpec((tm, D_MODEL), row),
                  pl.BlockSpec(mod.shape, const),
                  pl.BlockSpec(norm_w.shape, const),
                  pl.BlockSpec((D_MODEL, 2 * D_FF), const, pipeline_mode=resident),
                  pl.BlockSpec((D_FF, D_MODEL), const, pipeline_mode=resident)],
        out_specs=pl.BlockSpec((tm, D_MODEL), row),
        out_shape=jax.ShapeDtypeStruct((t, D_MODEL), F32),
        compiler_params=_cparams(("arbitrary",)),
        name="swiglu",
    )(x, mod, norm_w, w_in, w_out)


def _rope_tables(seq_len):
    rows = seq_len // GRID_W
    row_idx = jnp.repeat(jnp.arange(rows, dtype=F32), GRID_W)
    col_idx = (jnp.arange(seq_len) % GRID_W).astype(F32)
    n_freq = DK // 4
    freqs = ROPE_BASE ** (-jnp.arange(n_freq, dtype=F32) / n_freq)
    ang = jnp.concatenate([row_idx[:, None] * freqs, col_idx[:, None] * freqs], axis=-1)
    cos = jnp.repeat(jnp.cos(ang), 2, axis=-1)
    sin = jnp.repeat(jnp.sin(ang), 2, axis=-1)
    sign = jnp.tile(jnp.array([-1.0, 1.0], F32), DK // 2)
    return cos, sin * sign


def _one_path(x, mod, seq_len, n_seq, wts, rope_tabs, s_ret0, s_dn0, emit_state):
    rows_per_cond = seq_len if mod.shape[0] > 1 else x.shape[0]
    proj, gates = _inproj(x, mod, wts["norm_w"], wts["w_in_t"], wts["conv_w"], rows_per_cond,
                          seq_len, rope_tabs)
    ret = _retention(proj, wts["logit_rep"], wts["ret_gn_w"], n_seq, seq_len,
                     s0=s_ret0, emit_state=emit_state)
    gdn = _gdn(proj, gates, wts["a_log"], wts["dt_bias"], n_seq, seq_len,
               s0=s_dn0, emit_state=emit_state)
    x1 = _merge(x, ret[0], gdn[0], gdn[1], proj, mod, wts["norm_w"], wts["dn_norm_w"],
                wts["w_ret_o"], wts["w_dn_o"], wts["w_out"], rows_per_cond)
    y = _ffn(x1, mod, wts["norm_w"], wts["w_ffn_in"], wts["w_ffn_out"], rows_per_cond)
    if emit_state:
        return y, ret[1], gdn[2]
    return y, None, None


def kernel(x_prompt, x_sample, c, state_ret, state_dn, c_ctx, w_mod, b_mod, norm_w, w_in, conv_w,
           ret_decay_logit, ret_gn_w, dn_a_log, dn_dt_bias, dn_norm_w, w_ret_o, w_dn_o, w_out,
           w_ffn_in, w_ffn_out):
    n_ctx, l_ctx, _ = x_prompt.shape
    n_lat, l_lat, _ = x_sample.shape
    assert w_mod.shape[0] == 1, "single-layer kernel"

    cond8 = jnp.zeros((8, D_MODEL), F32).at[0].set(c_ctx).at[1:1 + n_lat].set(c)
    mod = _modulation(cond8, w_mod[0], b_mod)
    mod_ctx, mod_lat = mod[0:1], mod[1:1 + n_lat]

    wts = {
        "norm_w": norm_w[0],
        "w_in_t": w_in[0].T.astype(BF16),
        "logit_rep": jnp.broadcast_to(ret_decay_logit[0].T[:, :, None], (HEADS, 2, DK)),
        "ret_gn_w": ret_gn_w,
        "conv_w": conv_w[0],
        "a_log": dn_a_log.reshape(N_GATE, 1),
        "dt_bias": dn_dt_bias.reshape(N_GATE, 1),
        "dn_norm_w": dn_norm_w,
        "w_ret_o": w_ret_o[0].astype(BF16),
        "w_dn_o": w_dn_o[0].astype(BF16),
        "w_out": w_out[0].astype(BF16),
        "w_ffn_in": w_ffn_in[0].astype(BF16),
        "w_ffn_out": w_ffn_out[0].astype(BF16),
    }

    y_p, s_ret, s_dn = _one_path(x_prompt.reshape(n_ctx * l_ctx, D_MODEL), mod_ctx, l_ctx, n_ctx,
                                 wts, None, None, None, True)
    y_s, _, _ = _one_path(x_sample.reshape(n_lat * l_lat, D_MODEL), mod_lat, l_lat, n_lat,
                          wts, _rope_tables(l_lat), state_ret, state_dn, False)
    return (y_p.reshape(x_prompt.shape), y_s.reshape(x_sample.shape), s_ret, s_dn)
```

```python
import functools

import jax
import jax.numpy as jnp
from jax import lax
from jax.experimental import pallas as pl
from jax.experimental.pallas import tpu as pltpu

F32 = jnp.float32
BF16 = jnp.bfloat16

D_MODEL = 1024
HEADS = 4
DK = 128
DV = 256
QK_W = HEADS * DK
V_W = HEADS * DV
GDN_CHUNK = 64
TRI_BASE = 8
BLK = 256
SUB = 8
LANES = 128
HALO = 16
CONV_WIDTH = 3
GRID_W = 64
ROPE_BASE = 10000.0
EPS = 1e-6
D_FF = 2816
MXU_DIM = 256
FF_CHUNKS = (6 * MXU_DIM, 5 * MXU_DIM)
assert sum(FF_CHUNKS) == D_FF
N_GATE = 2 * HEADS

COL_RQ, COL_RK, COL_RV, COL_RG = 0, 512, 1024, 2048
COL_DQK, COL_DV, COL_DZ = 3072, 4096, 5120
COL_GATE = 6144
COL_GR, COL_GD = 6144, 7168
PROJ_W = 8192
PROJ_TN = 4096

VMEM_LIMIT = 56 * 1024 * 1024


def _cparams(sem):
    return pltpu.CompilerParams(dimension_semantics=sem, vmem_limit_bytes=VMEM_LIMIT)


def _mm(a, b):
    return jnp.dot(a.astype(BF16), b.astype(BF16), preferred_element_type=F32)


def _mm_nt(a, b):
    return lax.dot_general(a.astype(BF16), b.astype(BF16), (((1,), (1,)), ((), ())),
                           preferred_element_type=F32)


def _mm_tn(a, b):
    return lax.dot_general(a.astype(BF16), b.astype(BF16), (((0,), (0,)), ((), ())),
                           preferred_element_type=F32)


def _sigmoid(x):
    return 1.0 / (1.0 + jnp.exp(-x))


def _silu(x):
    return x * _sigmoid(x)


def _softplus(x):
    return jnp.maximum(x, 0.0) + jnp.log(1.0 + jnp.exp(-jnp.abs(x)))


def _log_sigmoid(x):
    return -_softplus(-x)


def _rms(x, w):
    return x * lax.rsqrt(jnp.mean(x * x, axis=-1, keepdims=True) + EPS) * w


def _mod_kernel(c_ref, w_ref, b_ref, o_ref):
    o_ref[...] = _mm(_silu(c_ref[...]), w_ref[...]) + b_ref[...]


def _modulation(cond8, w_mod, b_mod):
    n = w_mod.shape[1]
    tn = 1536
    return pl.pallas_call(
        _mod_kernel,
        grid=(n // tn,),
        in_specs=[pl.BlockSpec((8, D_MODEL), lambda j: (0, 0)),
                  pl.BlockSpec((D_MODEL, tn), lambda j: (0, j)),
                  pl.BlockSpec((1, tn), lambda j: (0, j))],
        out_specs=pl.BlockSpec((8, tn), lambda j: (0, j)),
        out_shape=jax.ShapeDtypeStruct((8, n), F32),
        compiler_params=_cparams(("arbitrary",)),
        name="modulation",
    )(cond8, w_mod, b_mod)


CHUNK_W = 256
SUBROWS = 128
PLAIN, RET_Q, RET_K, DN_Q, DN_K, DN_V = range(6)
CHUNK_KINDS = ((RET_Q, RET_Q, RET_K, RET_K) + (PLAIN,) * 4
               + (PLAIN,) * 4 + (DN_Q, DN_Q, DN_K, DN_K)
               + (DN_V,) * 4 + (PLAIN,) * 4
               + (PLAIN,) * 8)
assert len(CHUNK_KINDS) * CHUNK_W == PROJ_W and PROJ_TN % CHUNK_W == 0


def _pair_swap(x):
    lane = lax.broadcasted_iota(jnp.int32, x.shape, 1)
    nxt = pltpu.roll(x, x.shape[1] - 1, 1)
    prv = pltpu.roll(x, 1, 1)
    return jnp.where((lane & 1) == 0, nxt, prv)


def _inproj_kernel(*refs, tm, rows_per_cond, seq_len, rope):
    it = iter(refs)
    x_ref, xp_ref, xn_ref, mod_ref, nw_ref, w_ref, cw_ref = (next(it) for _ in range(7))
    cos_ref = next(it) if rope else None
    sin_ref = next(it) if rope else None
    o_ref, g_ref, h_sc, p_sc = next(it), next(it), next(it), next(it)

    i = pl.program_id(0)
    j = pl.program_id(1)
    t0 = i * tm
    tile_rows = slice(HALO, HALO + tm)

    @pl.when(j == 0)
    def _():
        ci = t0 // rows_per_cond
        m = mod_ref[pl.ds(ci, 1), :]
        sh1 = m[:, 0:D_MODEL]
        sc1 = m[:, D_MODEL:2 * D_MODEL]

        def pre(x):
            return (_rms(x, nw_ref[0:1, :]) * (1.0 + sc1) + sh1).astype(BF16)

        hb = pre(x_ref[...])
        h_sc[tile_rows, :] = hb
        h_sc[0:HALO, :] = pre(xp_ref[...])
        h_sc[HALO + tm:2 * HALO + tm, :] = pre(xn_ref[...])
        g_ref[...] = _mm_nt(w_ref[COL_GATE:COL_GATE + 2 * N_GATE, :], hb)

    seq_start = (t0 & (seq_len - 1)) == 0
    seq_end = ((t0 + tm) & (seq_len - 1)) == 0

    def conv_silu(win, r0, cw_lo):
        width = win.shape[1]
        inner = slice(SUB, SUB + SUBROWS)
        cur = win[inner]
        prev = pltpu.roll(win, 1, 0)[inner]
        nxt = pltpu.roll(win, win.shape[0] - 1, 0)[inner]
        row = lax.broadcasted_iota(jnp.int32, cur.shape, 0)
        if seq_len >= tm:
            if r0 == 0:
                prev = jnp.where((row == 0) & seq_start, 0.0, prev)
            if r0 + SUBROWS == tm:
                nxt = jnp.where((row == SUBROWS - 1) & seq_end, 0.0, nxt)
        else:
            if r0 % seq_len == 0:
                prev = jnp.where(row == 0, 0.0, prev)
            if (r0 + SUBROWS) % seq_len == 0:
                nxt = jnp.where(row == SUBROWS - 1, 0.0, nxt)
        cw = cw_ref[:, cw_lo:cw_lo + width]
        return _silu(prev * cw[0:1, :] + cur * cw[1:2, :] + nxt * cw[2:3, :])

    def l2n_heads(y, scale):
        outs = []
        for lo in range(0, y.shape[1], DK):
            yh = y[:, lo:lo + DK]
            outs.append(yh * (lax.rsqrt(jnp.sum(yh * yh, axis=-1, keepdims=True) + EPS) * scale))
        return jnp.concatenate(outs, axis=1)

    def rotary(p, r0):
        reps = p.shape[1] // DK
        cos = jnp.concatenate([cos_ref[r0:r0 + SUBROWS, :]] * reps, axis=1)
        sin = jnp.concatenate([sin_ref[r0:r0 + SUBROWS, :]] * reps, axis=1)
        return p * cos + _pair_swap(p) * sin

    def tile(col0, kinds):
        n_staged = 0
        for c, kind in enumerate(kinds):
            lo = c * CHUNK_W
            col = col0 + lo
            w_row = col if col < COL_GATE else col + 2 * N_GATE
            w = w_ref[w_row:w_row + CHUNK_W, :]
            conv = kind in (DN_Q, DN_K, DN_V)
            if not conv and not (rope and kind in (RET_Q, RET_K)):
                y = _mm_nt(h_sc[tile_rows, :], w)
                if kind == RET_Q:
                    y = y * (DK ** -0.5)
                o_ref[:, lo:lo + CHUNK_W] = y.astype(o_ref.dtype)
                continue
            buf = p_sc.at[n_staged % 2]
            n_staged += 1
            if conv:
                buf[...] = _mm_nt(h_sc[...], w)
                cw_lo = col - COL_DQK
            else:
                buf[tile_rows, :] = _mm_nt(h_sc[tile_rows, :], w)
            for r0 in range(0, tm, SUBROWS):
                if conv:
                    y = conv_silu(buf[HALO + r0 - SUB:HALO + r0 + SUBROWS + SUB, :], r0, cw_lo)
                    if kind == DN_Q:
                        y = l2n_heads(y, DK ** -0.5)
                    elif kind == DN_K:
                        y = l2n_heads(y, 1.0)
                else:
                    y = buf[HALO + r0:HALO + r0 + SUBROWS, :]
                    if kind == RET_Q:
                        y = y * (DK ** -0.5)
                    y = rotary(y, r0)
                o_ref[r0:r0 + SUBROWS, lo:lo + CHUNK_W] = y.astype(o_ref.dtype)

    per_tile = PROJ_TN // CHUNK_W
    for jt in range(PROJ_W // PROJ_TN):
        kinds = CHUNK_KINDS[jt * per_tile:(jt + 1) * per_tile]

        @pl.when(j == jt)
        def _(jt=jt, kinds=kinds):
            tile(jt * PROJ_TN, kinds)


def _inproj(x, mod, norm_w, w_t, conv_w, rows_per_cond, seq_len, rope_tabs):
    t = x.shape[0]
    tm, tn = 1024, PROJ_TN
    rope = rope_tabs is not None
    assert seq_len & (seq_len - 1) == 0 and (tm % seq_len == 0 or seq_len % tm == 0)
    kern = functools.partial(_inproj_kernel, tm=tm, rows_per_cond=rows_per_cond,
                             seq_len=seq_len, rope=rope)
    halo_per_tile = tm // HALO
    n_halo = t // HALO
    in_specs = [pl.BlockSpec((tm, D_MODEL), lambda i, j: (i, 0)),
                pl.BlockSpec((HALO, D_MODEL), lambda i, j: (jnp.maximum(i * halo_per_tile - 1, 0), 0)),
                pl.BlockSpec((HALO, D_MODEL),
                             lambda i, j: (jnp.minimum((i + 1) * halo_per_tile, n_halo - 1), 0)),
                pl.BlockSpec(mod.shape, lambda i, j: (0, 0)),
                pl.BlockSpec(norm_w.shape, lambda i, j: (0, 0)),
                pl.BlockSpec(w_t.shape, lambda i, j: (0, 0), pipeline_mode=pl.Buffered(1)),
                pl.BlockSpec(conv_w.shape, lambda i, j: (0, 0))]
    args = [x, x, x, mod, norm_w, w_t, conv_w]
    if rope:
        tiles_per_seq = max(seq_len // tm, 1)
        in_specs += [pl.BlockSpec((tm, DK), lambda i, j: (i % tiles_per_seq, 0))] * 2
        args += list(rope_tabs)
    return pl.pallas_call(
        kern,
        grid=(t // tm, PROJ_W // tn),
        in_specs=in_specs,
        out_specs=[pl.BlockSpec((tm, tn), lambda i, j: (i, j)),
                   pl.BlockSpec((2 * N_GATE, tm), lambda i, j: (0, i))],
        out_shape=[jax.ShapeDtypeStruct((t, PROJ_W), BF16),
                   jax.ShapeDtypeStruct((2 * N_GATE, t), F32)],
        scratch_shapes=[pltpu.VMEM((tm + 2 * HALO, D_MODEL), BF16),
                        pltpu.VMEM((2, tm + 2 * HALO, CHUNK_W), F32)],
        compiler_params=_cparams(("arbitrary", "arbitrary")),
        name="inproj",
    )(*args)


RET_GROUP = 8


def _ret_kernel(*refs, seq_len, has_s0, emit_state):
    it = iter(refs)
    q_ref, k_ref, v_ref = next(it), next(it), next(it)
    logit_ref, gnw_ref = next(it), next(it)
    s0_ref = next(it) if has_s0 else None
    o_ref = next(it)
    st_ref = next(it) if emit_state else None
    st_sc = next(it)

    nb = seq_len // BLK
    cross = has_s0 or nb > 1

    def rows(c):
        return slice(c * BLK, (c + 1) * BLK)

    def qk_cols(h):
        return slice(h * DK, (h + 1) * DK)

    def v_cols(h):
        return slice(h * DV, (h + 1) * DV)

    pos =lax.broadcasted_iota(jnp.int32, (BLK, DK), 0).astype(F32)

    def log_gammas(h):
        lg = _log_sigmoid(logit_ref[h])
        return lg[0:1, :], lg[1:2, :]

    for h in range(HEADS):
        lgf, lgb = log_gammas(h)
        kdec_f = jnp.exp(lgf * (BLK - 1.0 - pos))
        kdec_b = jnp.exp(lgb * pos)
        cdec_f = jnp.exp(lgf[:, 0:1] * float(BLK))
        cdec_b = jnp.exp(lgb[:, 0:1] * float(BLK))
        af = {c: _mm_tn(k_ref[rows(c), qk_cols(h)].astype(F32) * kdec_f, v_ref[rows(c), v_cols(h)])
              for c in range(nb) if c < nb - 1 or emit_state}
        ab = {c: _mm_tn(k_ref[rows(c), qk_cols(h)].astype(F32) * kdec_b, v_ref[rows(c), v_cols(h)])
              for c in range(nb) if c > 0 or emit_state}
        s = s0_ref[0, h] if has_s0 else jnp.zeros((DK, DV), F32)
        for c in range(nb):
            if cross:
                st_sc[h, c, 0:DK, :] = s.astype(BF16)
            if c in af:
                s = s * cdec_f + af[c]
        if emit_state:
            st_ref[0, h] = s
        s = s0_ref[1, h] if has_s0 else jnp.zeros((DK, DV), F32)
        for c in range(nb - 1, -1, -1):
            if cross:
                st_sc[h, c, DK:2 * DK, :] = s.astype(BF16)
            if c in ab:
                s = s * cdec_b + ab[c]
        if emit_state:
            st_ref[1, h] = s

    ii = lax.broadcasted_iota(jnp.int32, (BLK, BLK), 0)
    jj = lax.broadcasted_iota(jnp.int32, (BLK, BLK), 1)
    diff = (ii - jj).astype(F32)
    pos2 = lax.broadcasted_iota(jnp.int32, (BLK, 2 * DK), 0).astype(F32)
    first_half = lax.broadcasted_iota(jnp.int32, (BLK, 2 * DK), 1) < DK

    def head_consts(h):
        lgf, lgb = log_gammas(h)
        lgf1, lgb1 = lgf[:, 0:1], lgb[:, 0:1]
        dmat = (jnp.where(diff >= 0, jnp.exp(lgf1 * jnp.maximum(diff, 0.0)), 0.0)
                + jnp.where(diff <= 0, jnp.exp(lgb1 * jnp.maximum(-diff, 0.0)), 0.0))
        qdec = jnp.where(first_half, jnp.exp(lgf1 * (pos2 + 1.0)), jnp.exp(lgb1 * (BLK - pos2)))
        return dmat, qdec

    def unit(h, c, consts):
        dmat, qdec = consts
        qc = q_ref[rows(c), qk_cols(h)]
        s = _mm_nt(qc, k_ref[rows(c), qk_cols(h)])
        yield
        o = _mm(s * dmat, v_ref[rows(c), v_cols(h)])
        if cross:
            o_cross = _mm(jnp.concatenate([qc, qc], axis=1).astype(F32) * qdec, st_sc[h, c])
        yield
        if cross:
            o = o + o_cross
        mu = jnp.mean(o, axis=-1, keepdims=True)
        oc = o - mu
        var = jnp.mean(oc * oc, axis=-1, keepdims=True)
        o_ref[rows(c), v_cols(h)] = (oc * lax.rsqrt(var + EPS) * gnw_ref[:, v_cols(h)]).astype(o_ref.dtype)
        yield

    units = [(h, c) for h in range(HEADS) for c in range(nb)]
    for g0 in range(0, len(units), RET_GROUP):
        consts = {}
        gens = []
        for h, c in units[g0:g0 + RET_GROUP]:
            if h not in consts:
                consts[h] = head_consts(h)
            gens.append(unit(h, c, consts[h]))
        for _ in zip(*gens):
            pass


def _retention(proj, logit_rep, gn_w, n_seq, seq_len, s0=None, emit_state=False):
    t = proj.shape[0]
    has_s0 = s0 is not None
    kern = functools.partial(_ret_kernel, seq_len=seq_len, has_s0=has_s0, emit_state=emit_state)
    in_specs = [pl.BlockSpec((seq_len, QK_W), lambda b: (b, COL_RQ // QK_W)),
                pl.BlockSpec((seq_len, QK_W), lambda b: (b, COL_RK // QK_W)),
                pl.BlockSpec((seq_len, V_W), lambda b: (b, COL_RV // V_W))]
    args = [proj, proj, proj]
    in_specs += [pl.BlockSpec((HEADS, 2, DK), lambda b: (0, 0, 0)),
                 pl.BlockSpec((1, V_W), lambda b: (0, 0))]
    args += [logit_rep, gn_w]
    state_spec = pl.BlockSpec((None, None, 2, HEADS, DK, DV), lambda b: (b, 0, 0, 0, 0, 0))
    if has_s0:
        in_specs.append(state_spec)
        args.append(s0)
    out_specs = [pl.BlockSpec((seq_len, V_W), lambda b: (b, 0))]
    out_shape = [jax.ShapeDtypeStruct((t, V_W), BF16)]
    if emit_state:
        out_specs.append(state_spec)
        out_shape.append(jax.ShapeDtypeStruct((n_seq, 1, 2, HEADS, DK, DV), F32))
    nb = seq_len // BLK
    return pl.pallas_call(
        kern,
        grid=(n_seq,),
        in_specs=in_specs,
        out_specs=out_specs,
        out_shape=out_shape,
        scratch_shapes=[pltpu.VMEM((HEADS, nb, 2 * DK, DV), BF16)],
        compiler_params=_cparams(("arbitrary",)),
        name="retention",
    )(*args)


N_MASK = 8
(MASK_EQ_F, MASK_EQ_B, MASK_OFFDIAG, MASK_EYE, MASK_BASE, MASK_PAIR0) = 0, 1, 2, 3, 4, 5


def _build_masks(mask_sc, eye_sc):
    ii = lax.broadcasted_iota(jnp.int32, (BLK, BLK), 0)
    jj = lax.broadcasted_iota(jnp.int32, (BLK, BLK), 1)

    def blk(v, size):
        return lax.shift_right_logical(v, size.bit_length() - 1)

    same = blk(ii, GDN_CHUNK) == blk(jj, GDN_CHUNK)
    mask_sc[MASK_EQ_F] = (same & (jj <= ii)).astype(F32)
    mask_sc[MASK_EQ_B] = (same & (jj >= ii)).astype(F32)
    mask_sc[MASK_OFFDIAG] = (ii != jj).astype(F32)
    mask_sc[MASK_EYE] = (ii == jj).astype(F32)
    mask_sc[MASK_BASE] = (blk(ii, TRI_BASE) == blk(jj, TRI_BASE)).astype(F32)
    size, idx = TRI_BASE, MASK_PAIR0
    while size < GDN_CHUNK:
        pair = (blk(ii, 2 * size) == blk(jj, 2 * size)) & (blk(ii, size) != blk(jj, size))
        mask_sc[idx] = pair.astype(F32)
        size, idx = 2 * size, idx + 1
    eye_sc[...] = (ii == jj).astype(BF16)


def _seg_cumsums(x):
    n = x.shape[1]
    pos = lax.broadcasted_iota(jnp.int32, x.shape, 1) & (GDN_CHUNK - 1)
    up, down = x, x
    s = 1
    while s < GDN_CHUNK:
        up = up + jnp.where(pos >= s, pltpu.roll(up, s, 1), 0.0)
        down = down + jnp.where(pos < GDN_CHUNK - s, pltpu.roll(down, n - s, 1), 0.0)
        s *= 2
    return up, down


def _rows_to_cols(rows, eye_sc):
    p1 = rows.astype(BF16).astype(F32)
    r1 = rows - p1
    p2 = r1.astype(BF16).astype(F32)
    p3 = r1 - p2
    pieces = jnp.concatenate([p1, p2, p3], axis=0).astype(BF16)
    c3 = lax.dot_general(eye_sc[...], pieces, (((1,), (1,)), ((), ())), preferred_element_type=F32)
    return c3[:, 0:SUB] + c3[:, SUB:2 * SUB] + c3[:, 2 * SUB:3 * SUB]


def _gdn_kernel(*refs, nb, has_s0, emit_state):
    it = iter(refs)
    xqk_refs = (next(it), next(it))
    xv_refs = (next(it), next(it))
    g_refs = (next(it), next(it))
    alog_ref, dtb_ref = next(it), next(it)
    s0_ref = next(it) if has_s0 else None
    o_refs = (next(it), next(it))
    st_ref = next(it) if emit_state else None
    st_sc, mask_sc, eye_sc = next(it), next(it), next(it)

    b = pl.program_id(0)
    step = pl.program_id(1)

    @pl.when((b == 0) & (step == 0))
    def _():
        _build_masks(mask_sc, eye_sc)

    @pl.when(step == 0)
    def _():
        for d in range(2):
            for h in range(HEADS):
                st_sc[d, h] = s0_ref[d, h] if has_s0 else jnp.zeros((DK, DV), F32)

    nck = BLK // GDN_CHUNK

    def gates():
        row8 = lax.broadcasted_iota(jnp.int32, (N_GATE, BLK), 0)
        fwd_rows = row8 < HEADS
        logit_b = jnp.where(fwd_rows, g_refs[0][0:N_GATE, :], g_refs[1][0:N_GATE, :])
        logit_a = jnp.where(fwd_rows, g_refs[0][N_GATE:2 * N_GATE, :], g_refs[1][N_GATE:2 * N_GATE, :])
        beta = _sigmoid(logit_b)
        la = -jnp.exp(alog_ref[...]) * _softplus(logit_a + dtb_ref[...])
        up, down = _seg_cumsums(la)
        g8 = jnp.where(fwd_rows, up, down)
        r8 = jnp.where(fwd_rows, down, up) - la
        etot8 = jnp.exp(up + down - la)
        return dict(beta=beta, g8=g8, eg8=jnp.exp(g8), erb8=jnp.exp(r8) * beta, etot8=etot8,
                    gcols=_rows_to_cols(g8, eye_sc))

    gate = {}
    same_block_products = {}

    def chain(d, h):
        c = d * HEADS + h
        q16 = xqk_refs[d][:, h * DK:(h + 1) * DK]
        k16 = xqk_refs[d][:, QK_W + h * DK:QK_W + (h + 1) * DK]
        v16 = xv_refs[d][:, h * DV:(h + 1) * DV]
        q = q16.astype(F32)
        k = k16.astype(F32)
        if nb == 1 and d == 1:
            kk, qk = same_block_products[h]
        else:
            kk = _mm_nt(k16, k16)
            qk = _mm_nt(q16, k16)
            same_block_products[h] = (kk, qk)
        yield
        g_col = gate["gcols"][:, c:c + 1]
        g_row = gate["g8"][c:c + 1, :]
        m_eq = mask_sc[MASK_EQ_F if d == 0 else MASK_EQ_B]
        decay_beta = (jnp.exp(jnp.minimum(g_col - g_row, 0.0)) * m_eq) * gate["beta"][c:c + 1, :]
        lp = kk * decay_beta * mask_sc[MASK_OFFDIAG]
        sc16 = (qk * decay_beta).astype(BF16)
        qg16 = (q * jnp.exp(g_col)).astype(BF16)
        kt16 = (k.T * gate["erb8"][c:c + 1, :]).astype(BF16)

        x = -(lp * mask_sc[MASK_BASE])
        t = mask_sc[MASK_EYE] + x
        for _ in range(TRI_BASE.bit_length() - 2):
            x = _mm(x, x)
            yield
            t = t + _mm(t, x)
            yield
        size, idx = TRI_BASE, MASK_PAIR0
        while size < GDN_CHUNK:
            groups = [slice(r * size, (r + 1) * size) for r in range(BLK // size)]
            late = [(r % 2 == 1) == (d == 0) for r in range(BLK // size)]
            e_late = jnp.concatenate([lp[g, :] * mask_sc[idx, g, :] for g, on in zip(groups, late) if on],
                                     axis=0)
            et = _mm(e_late, t)
            yield
            t_late = jnp.concatenate([t[g, :] for g, on in zip(groups, late) if on], axis=0)
            et_rows = iter(et[j * size:(j + 1) * size, :] for j in range(BLK // size // 2))
            et_full = jnp.concatenate([next(et_rows) if on else jnp.zeros((size, BLK), F32)
                                       for on in late], axis=0)
            t_late = t_late - _mm(t_late, et_full)
            yield
            new_rows = iter(t_late[j * size:(j + 1) * size, :] for j in range(BLK // size // 2))
            t = jnp.concatenate([next(new_rows) if on else t[g, :] for g, on in zip(groups, late)],
                                axis=0)
            size, idx = 2 * size, idx + 1
        u = _mm(t, v16)
        w16 = _mm(t * gate["eg8"][c:c + 1, :], k16).astype(BF16)
        yield

        order = range(nck) if d == 0 else range(nck - 1, -1, -1)
        for cc in order:
            rs = slice(cc * GDN_CHUNK, (cc + 1) * GDN_CHUNK)
            s_prev = st_sc[d, h]
            wq = jnp.concatenate([w16[rs, :], qg16[rs, :]], axis=0)
            ws_qs = _mm(wq, s_prev)
            yield
            v_new = (u[rs, :] - ws_qs[0:GDN_CHUNK]).astype(BF16)
            sk = _mm(jnp.concatenate([sc16[rs, rs], kt16[:, rs]], axis=0), v_new)
            o = ws_qs[GDN_CHUNK:] + sk[0:GDN_CHUNK]
            decay_all = gate["etot8"][c:c + 1, cc * GDN_CHUNK:cc * GDN_CHUNK + 1]
            st_sc[d, h] = s_prev * decay_all + sk[GDN_CHUNK:]
            o_refs[d][rs, h * DV:(h + 1) * DV] = o.astype(o_refs[d].dtype)
            yield

    chains = [chain(d, h) for d in range(2) for h in range(HEADS)]
    for ch in chains:
        next(ch)
    gate.update(gates())
    for _ in zip(*chains):
        pass

    if emit_state:
        @pl.when(step == nb - 1)
        def _():
            for d in range(2):
                for h in range(HEADS):
                    st_ref[d, h] = st_sc[d, h]


def _gdn(proj, gates, a_log, dt_bias, n_seq, seq_len, s0=None, emit_state=False):
    t = proj.shape[0]
    nb = seq_len // BLK
    has_s0 = s0 is not None
    kern = functools.partial(_gdn_kernel, nb=nb, has_s0=has_s0, emit_state=emit_state)

    def fblk(b, s):
        return b * nb + s

    def bblk(b, s):
        return b * nb + nb - 1 - s

    def spec_x(blk, col):
        return pl.BlockSpec((BLK, V_W), lambda b, s: (blk(b, s), col // V_W))

    def spec_g(blk):
        return pl.BlockSpec((2 * N_GATE, BLK), lambda b, s: (0, blk(b, s)))

    const2 = lambda b, s: (0, 0)
    in_specs = [spec_x(fblk, COL_DQK), spec_x(bblk, COL_DQK), spec_x(fblk, COL_DV), spec_x(bblk, COL_DV),
                spec_g(fblk), spec_g(bblk),
                pl.BlockSpec((N_GATE, 1), const2),
                pl.BlockSpec((N_GATE, 1), const2)]
    args = [proj, proj, proj, proj, gates, gates, a_log, dt_bias]
    state_spec = pl.BlockSpec((None, None, 2, HEADS, DK, DV), lambda b, s: (b, 0, 0, 0, 0, 0))
    if has_s0:
        in_specs.append(state_spec)
        args.append(s0)
    out_specs = [pl.BlockSpec((BLK, V_W), lambda b, s: (fblk(b, s), 0)),
                 pl.BlockSpec((BLK, V_W), lambda b, s: (bblk(b, s), 0))]
    out_shape = [jax.ShapeDtypeStruct((t, V_W), BF16), jax.ShapeDtypeStruct((t, V_W), BF16)]
    if emit_state:
        out_specs.append(state_spec)
        out_shape.append(jax.ShapeDtypeStruct((n_seq, 1, 2, HEADS, DK, DV), F32))
    scratch = [pltpu.VMEM((2, HEADS, DK, DV), F32),
               pltpu.VMEM((N_MASK, BLK, BLK), F32),
               pltpu.VMEM((BLK, BLK), BF16)]
    return pl.pallas_call(
        kern,
        grid=(n_seq, nb),
        in_specs=in_specs,
        out_specs=out_specs,
        out_shape=out_shape,
        scratch_shapes=scratch,
        compiler_params=_cparams(("arbitrary", "arbitrary")),
        name="gated_delta",
    )(*args)


def _merge_kernel(x_ref, or_ref, of_ref, ob_ref, rg_ref, dz_ref, gr_ref, gd_ref, mod_ref, nw_ref,
                  dnw_ref, wr_ref, wd_ref, wo_ref, o_ref, *, tm, rows_per_cond):
    i = pl.program_id(0)
    ci = (i * tm) // rows_per_cond
    m = mod_ref[pl.ds(ci, 1), :]
    g1 = m[:, 2 * D_MODEL:3 * D_MODEL]
    y_r = _mm(_silu(rg_ref[...].astype(F32)) * or_ref[...].astype(F32), wr_ref[...])
    dnw = dnw_ref[...]
    heads = []
    for h in range(HEADS):
        cols = slice(h * DV, (h + 1) * DV)
        od = _rms(of_ref[:, cols].astype(F32) + ob_ref[:, cols].astype(F32), dnw)
        heads.append((od * _silu(dz_ref[:, cols].astype(F32))).astype(BF16))
    y_d = jnp.dot(jnp.concatenate(heads, axis=1), wd_ref[...], preferred_element_type=F32)
    merged = _sigmoid(gr_ref[...].astype(F32)) * y_r + _sigmoid(gd_ref[...].astype(F32)) * y_d
    mo = _mm(merged, wo_ref[...])
    o_ref[...] = x_ref[...] + g1 * _rms(mo, nw_ref[1:2, :])


def _merge(x, o_r, o_f, o_b, proj, mod, norm_w, dn_norm_w, w_ret_o, w_dn_o, w_out, rows_per_cond):
    t = x.shape[0]
    tm = 512
    kern = functools.partial(_merge_kernel, tm=tm, rows_per_cond=rows_per_cond)
    row = lambda i: (i, 0)
    const = lambda i: (0, 0)
    wspec = pl.BlockSpec((D_MODEL, D_MODEL), const)
    return pl.pallas_call(
        kern,
        grid=(t // tm,),
        in_specs=[pl.BlockSpec((tm, D_MODEL), row),
                  pl.BlockSpec((tm, V_W), row),
                  pl.BlockSpec((tm, V_W), row),
                  pl.BlockSpec((tm, V_W), row),
                  pl.BlockSpec((tm, V_W), lambda i: (i, COL_RG // V_W)),
                  pl.BlockSpec((tm, V_W), lambda i: (i, COL_DZ // V_W)),
                  pl.BlockSpec((tm, D_MODEL), lambda i: (i, COL_GR // D_MODEL)),
                  pl.BlockSpec((tm, D_MODEL), lambda i: (i, COL_GD // D_MODEL)),
                  pl.BlockSpec(mod.shape, const),
                  pl.BlockSpec(norm_w.shape, const),
                  pl.BlockSpec(dn_norm_w.shape, const),
                  wspec, wspec, wspec],
        out_specs=pl.BlockSpec((tm, D_MODEL), row),
        out_shape=jax.ShapeDtypeStruct((t, D_MODEL), F32),
        compiler_params=_cparams(("arbitrary",)),
        name="merge_out",
    )(x, o_r, o_f, o_b, proj, proj, proj, proj, mod, norm_w, dn_norm_w, w_ret_o, w_dn_o, w_out)


def _ffn_kernel(x_ref, mod_ref, nw_ref, wi_ref, wo_ref, o_ref, *, tm, rows_per_cond):
    i = pl.program_id(0)
    ci = (i * tm) // rows_per_cond
    m = mod_ref[pl.ds(ci, 1), :]
    sh2 = m[:, 3 * D_MODEL:4 * D_MODEL]
    sc2 = m[:, 4 * D_MODEL:5 * D_MODEL]
    g2 = m[:, 5 * D_MODEL:6 * D_MODEL]
    x = x_ref[...]
    hb = (_rms(x, nw_ref[2:3, :]) * (1.0 + sc2) + sh2).astype(BF16)
    f = None
    lo = 0
    for width in FF_CHUNKS:
        gate = jnp.dot(hb, wi_ref[:, lo:lo + width], preferred_element_type=F32)
        up = jnp.dot(hb, wi_ref[:, D_FF + lo:D_FF + lo + width], preferred_element_type=F32)
        part = _mm(_silu(gate) * up, wo_ref[lo:lo + width, :])
        lo += width
        f = part if f is None else f + part
    o_ref[...] = x + g2 * _rms(f, nw_ref[3:4, :])


def _ffn(x, mod, norm_w, w_in, w_out, rows_per_cond):
    t = x.shape[0]
    tm = 512
    kern = functools.partial(_ffn_kernel, tm=tm, rows_per_cond=rows_per_cond)
    row = lambda i: (i, 0)
    const = lambda i: (0, 0)
    resident = pl.Buffered(1)
    return pl.pallas_call(
        kern,
        grid=(t // tm,),
        in_specs=[pl.BlockSpec((tm, D_MODEL), row),
                  pl.BlockSpec(mod.shape, const),
                  pl.BlockSpec(norm_w.shape, const),
                  pl.BlockSpec((D_MODEL, 2 * D_FF), const, pipeline_mode=resident),
                  pl.BlockSpec((D_FF, D_MODEL), const, pipeline_mode=resident)],
        out_specs=pl.BlockSpec((tm, D_MODEL), row),
        out_shape=jax.ShapeDtypeStruct((t, D_MODEL), F32),
        compiler_params=_cparams(("arbitrary",)),
        name="swiglu",
    )(x, mod, norm_w, w_in, w_out)


def _rope_tables(seq_len):
    rows = seq_len // GRID_W
    row_idx = jnp.repeat(jnp.arange(rows, dtype=F32), GRID_W)
    col_idx = (jnp.arange(seq_len) % GRID_W).astype(F32)
    n_freq = DK // 4
    freqs = ROPE_BASE ** (-jnp.arange(n_freq, dtype=F32) / n_freq)
    ang = jnp.concatenate([row_idx[:, None] * freqs, col_idx[:, None] * freqs], axis=-1)
    cos = jnp.repeat(jnp.cos(ang), 2, axis=-1)
    sin = jnp.repeat(jnp.sin(ang), 2, axis=-1)
    sign = jnp.tile(jnp.array([-1.0, 1.0], F32), DK // 2)
    return cos, sin * sign


def _one_path(x, mod, seq_len, n_seq, wts, rope_tabs, s_ret0, s_dn0, emit_state):
    rows_per_cond = seq_len if mod.shape[0] > 1 else x.shape[0]
    proj, gates = _inproj(x, mod, wts["norm_w"], wts["w_in_t"], wts["conv_w"], rows_per_cond,
                          seq_len, rope_tabs)
    ret = _retention(proj, wts["logit_rep"], wts["ret_gn_w"], n_seq, seq_len,
                     s0=s_ret0, emit_state=emit_state)
    gdn = _gdn(proj, gates, wts["a_log"], wts["dt_bias"], n_seq, seq_len,
               s0=s_dn0, emit_state=emit_state)
    x1 = _merge(x, ret[0], gdn[0], gdn[1], proj, mod, wts["norm_w"], wts["dn_norm_w"],
                wts["w_ret_o"], wts["w_dn_o"], wts["w_out"], rows_per_cond)
    y = _ffn(x1, mod, wts["norm_w"], wts["w_ffn_in"], wts["w_ffn_out"], rows_per_cond)
    if emit_state:
        return y, ret[1], gdn[2]
    return y, None, None


def kernel(x_prompt, x_sample, c, state_ret, state_dn, c_ctx, w_mod, b_mod, norm_w, w_in, conv_w,
           ret_decay_logit, ret_gn_w, dn_a_log, dn_dt_bias, dn_norm_w, w_ret_o, w_dn_o, w_out,
           w_ffn_in, w_ffn_out):
    n_ctx, l_ctx, _ = x_prompt.shape
    n_lat, l_lat, _ = x_sample.shape
    assert w_mod.shape[0] == 1, "single-layer kernel"

    cond8 = jnp.zeros((8, D_MODEL), F32).at[0].set(c_ctx).at[1:1 + n_lat].set(c)
    mod = _modulation(cond8, w_mod[0], b_mod)
    mod_ctx, mod_lat = mod[0:1], mod[1:1 + n_lat]

    wts = {
        "norm_w": norm_w[0],
        "w_in_t": w_in[0].T.astype(BF16),
        "logit_rep": jnp.broadcast_to(ret_decay_logit[0].T[:, :, None], (HEADS, 2, DK)),
        "ret_gn_w": ret_gn_w,
        "conv_w": conv_w[0],
        "a_log": dn_a_log.reshape(N_GATE, 1),
        "dt_bias": dn_dt_bias.reshape(N_GATE, 1),
        "dn_norm_w": dn_norm_w,
        "w_ret_o": w_ret_o[0].astype(BF16),
        "w_dn_o": w_dn_o[0].astype(BF16),
        "w_out": w_out[0].astype(BF16),
        "w_ffn_in": w_ffn_in[0].astype(BF16),
        "w_ffn_out": w_ffn_out[0].astype(BF16),
    }

    y_p, s_ret, s_dn = _one_path(x_prompt.reshape(n_ctx * l_ctx, D_MODEL), mod_ctx, l_ctx, n_ctx,
                                 wts, None, None, None, True)
    y_s, _, _ = _one_path(x_sample.reshape(n_lat * l_lat, D_MODEL), mod_lat, l_lat, n_lat,
                          wts, _rope_tables(l_lat), state_ret, state_dn, False)
    return (y_p.reshape(x_prompt.shape), y_s.reshape(x_sample.shape), s_ret, s_dn)
```
